```python
import math
import jax, jax.numpy as jnp
from jax import lax
import numpy as np

D_MODEL = 2048
BATCH = 2
SEQ = 4096
DEPTH = 1
DEC_BATCH = 16
DEC_SEQ = 16
PAST_LEN = 4096

CHUNK = 64
N_META = 16
D_MIX = D_MODEL
D_CONV = D_MIX // 2
N_HEADS = 8
HEAD_DIM = 128
D_ATTN = N_HEADS * HEAD_DIM
D_IN = 2 * D_CONV + 3 * D_ATTN
CONV_WIDTH = 31
Q_BLOCK = 128
N_EXPERTS = 32
TOP_K = 4
D_FF = D_MODEL
SWIGLU_ALPHA = 1.702
SWIGLU_LIMIT = 7.0
EXPERT_BLOCK = 128
EPS = 1e-5

kernel_name = "hymba_conformer_stickbreak_moe_stream_step"


def _rmsnorm(x, g):
    xf = x.astype(jnp.float32)
    y = xf * lax.rsqrt(jnp.mean(xf * xf, axis=-1, keepdims=True) + EPS) * g.astype(jnp.float32)
    return y.astype(x.dtype)


def _layernorm(x, g, b):
    xf = x.astype(jnp.float32)
    mu = jnp.mean(xf, axis=-1, keepdims=True)
    var = jnp.mean(jnp.square(xf - mu), axis=-1, keepdims=True)
    y = (xf - mu) * lax.rsqrt(var + EPS) * g.astype(jnp.float32) + b.astype(jnp.float32)
    return y.astype(x.dtype)


def _causal_dwconv(u_ext, w, bias):
    out = lax.conv_general_dilated(
        u_ext, w[:, None, :].astype(u_ext.dtype), window_strides=(1,), padding='VALID',
        dimension_numbers=('NWC', 'WIO', 'NWC'), feature_group_count=u_ext.shape[-1])
    return out + bias.astype(out.dtype)


def _stick_breaking(q, k, v, q_pos, k_pos):
    b, tq = q.shape[0], q.shape[1]
    blk = min(Q_BLOCK, tq)
    nb = -(-tq // blk)
    pad = nb * blk - tq
    scale = 1.0 / math.sqrt(HEAD_DIM)
    qf = jnp.pad(q.astype(jnp.float32) * scale, ((0, 0), (0, pad), (0, 0), (0, 0)))
    qp = jnp.pad(q_pos, (0, pad))
    qb = qf.reshape(b, nb, blk, N_HEADS, HEAD_DIM).transpose(1, 0, 2, 3, 4)
    qpb = qp.reshape(nb, blk)
    kf = k.astype(jnp.float32)
    vf = v.astype(jnp.float32)

    def one_block(args):
        qi, pi = args
        z = jnp.einsum('bqhd,bkhd->bhqk', qi, kf)
        mask = (k_pos[None, :] < pi[:, None])[None, None]
        log_keep = jnp.where(mask, jax.nn.log_sigmoid(-z), 0.0)
        log_after = lax.cumsum(log_keep, axis=3, reverse=True) - log_keep
        w = jnp.where(mask, jnp.exp(jax.nn.log_sigmoid(z) + log_after), 0.0)
        return jnp.einsum('bhqk,bkhd->bqhd', w, vf)

    o = lax.map(one_block, (qb, qpb))
    o = o.transpose(1, 0, 2, 3, 4).reshape(b, nb * blk, N_HEADS, HEAD_DIM)[:, :tq]
    return o.astype(q.dtype)


def _moe(x, w_router, b_router, w_up, b_up, w_down, b_down):
    lead = x.shape[:-1]
    xf = x.reshape(-1, D_MODEL)
    n = xf.shape[0]
    logits = (xf @ w_router + b_router).astype(jnp.float32)
    top_v, top_e = lax.top_k(logits, TOP_K)
    gates = jax.nn.softmax(top_v, axis=-1)
    m = n * TOP_K
    flat_e = top_e.reshape(-1).astype(jnp.int32)
    flat_tok = jnp.repeat(jnp.arange(n, dtype=jnp.int32), TOP_K)
    flat_g = gates.reshape(-1)
    order = jnp.argsort(flat_e)
    se = flat_e[order]
    counts = jnp.bincount(flat_e, length=N_EXPERTS).astype(jnp.int32)
    start = jnp.cumsum(counts) - counts
    padded = (counts + EXPERT_BLOCK - 1) // EXPERT_BLOCK * EXPERT_BLOCK
    pend = jnp.cumsum(padded)
    pstart = pend - padded
    dest = pstart[se] + jnp.arange(m, dtype=jnp.int32) - start[se]
    nb = -(-m // EXPERT_BLOCK) + N_EXPERTS
    slots = nb * EXPERT_BLOCK
    slot_tok = jnp.full((slots,), n, jnp.int32).at[dest].set(flat_tok[order])
    slot_g = jnp.zeros((slots,), jnp.float32).at[dest].set(flat_g[order])
    block_e = jnp.minimum(
        jnp.searchsorted(pend, jnp.arange(nb, dtype=jnp.int32) * EXPERT_BLOCK, side='right'),
        N_EXPERTS - 1)
    x_ext = jnp.concatenate([xf, jnp.zeros((1, D_MODEL), xf.dtype)], axis=0)
    xs = x_ext[slot_tok].reshape(nb, EXPERT_BLOCK, D_MODEL)

    def expert_block(args):
        xb, e = args
        h = xb @ w_up[e] + b_up[e]
        h_glu = jnp.minimum(h[:, :D_FF], SWIGLU_LIMIT)
        h_lin = jnp.clip(h[:, D_FF:], -SWIGLU_LIMIT, SWIGLU_LIMIT)
        a = h_glu * jax.nn.sigmoid(SWIGLU_ALPHA * h_glu) * (h_lin + 1)
        return a @ w_down[e] + b_down[e]

    out = lax.map(expert_block, (xs, block_e)).reshape(slots, D_MODEL)
    y = jnp.zeros((n + 1, D_MODEL), out.dtype).at[slot_tok].add(
        out * slot_g[:, None].astype(out.dtype))[:n]
    return y.reshape(*lead, D_MODEL).astype(x.dtype)


def _layer(h, conv_ctx, k_past, v_past, g_mix, w_in, conv_w, conv_b, conv_ln_g, conv_ln_b,
           g_attn_out, w_out, g_ffn, w_router, b_router, w_up, b_up, w_down, b_down):
    b, t, _ = h.shape
    p = k_past.shape[1]
    n = _rmsnorm(h, g_mix)
    proj = n @ w_in
    a, g, q, k, v = jnp.split(
        proj, [D_CONV, 2 * D_CONV, 2 * D_CONV + D_ATTN, 2 * D_CONV + 2 * D_ATTN], axis=-1)
    u = a * jax.nn.sigmoid(g)
    u_ext = jnp.concatenate([conv_ctx.astype(u.dtype), u], axis=1)
    c = jax.nn.silu(_layernorm(_causal_dwconv(u_ext, conv_w, conv_b), conv_ln_g, conv_ln_b))
    q = q.reshape(b, t, N_HEADS, HEAD_DIM)
    k = k.reshape(b, t, N_HEADS, HEAD_DIM)
    v = v.reshape(b, t, N_HEADS, HEAD_DIM)
    k_all = jnp.concatenate([k_past.astype(k.dtype), k], axis=1)
    v_all = jnp.concatenate([v_past.astype(v.dtype), v], axis=1)
    q_pos = p + jnp.arange(t, dtype=jnp.int32)
    k_pos = jnp.arange(p + t, dtype=jnp.int32)
    o = _stick_breaking(q, k_all, v_all, q_pos, k_pos)
    o = _rmsnorm(o, g_attn_out.reshape(N_HEADS, HEAD_DIM)).reshape(b, t, D_ATTN)
    h = h + jnp.concatenate([c, o], axis=-1) @ w_out
    h = h + _moe(_rmsnorm(h, g_ffn), w_router, b_router, w_up, b_up, w_down, b_down)
    return h, k, v, u_ext[:, -(CONV_WIDTH - 1):]


def setup_inputs(seed: int = 0) -> dict:
    key = jax.random.key(seed)
    ks = jax.random.split(key, 24)
    f32 = jnp.float32
    nrm = lambda k, shape, s: jax.random.normal(k, shape, f32) * s
    return {
        "x_prompt": nrm(ks[0], (BATCH, SEQ, D_MODEL), 1.0),
        "x_sample": nrm(ks[1], (DEC_BATCH, DEC_SEQ, D_MODEL), 1.0),
        "cache_k": nrm(ks[2], (DEPTH, DEC_BATCH, PAST_LEN, N_HEADS, HEAD_DIM), 1.0),
        "cache_v": nrm(ks[3], (DEPTH, DEC_BATCH, PAST_LEN, N_HEADS, HEAD_DIM), 1.0),
        "state_conv": nrm(ks[4], (DEPTH, DEC_BATCH, CONV_WIDTH - 1, D_CONV), 0.5),
        "meta_tokens": nrm(ks[5], (N_META, D_MODEL), 1.0),
        "g_mix": 1.0 + nrm(ks[6], (DEPTH, D_MODEL), 0.02),
        "w_in": nrm(ks[7], (DEPTH, D_MODEL, D_IN), D_MODEL ** -0.5),
        "conv_w": nrm(ks[8], (DEPTH, CONV_WIDTH, D_CONV), CONV_WIDTH ** -0.5),
        "conv_b": nrm(ks[9], (DEPTH, D_CONV), 0.01),
        "conv_ln_g": 1.0 + nrm(ks[10], (DEPTH, D_CONV), 0.02),
        "conv_ln_b": nrm(ks[11], (DEPTH, D_CONV), 0.01),
        "g_attn_out": 1.0 + nrm(ks[12], (DEPTH, D_ATTN), 0.02),
        "w_out": nrm(ks[13], (DEPTH, D_MIX, D_MODEL), D_MIX ** -0.5),
        "g_ffn": 1.0 + nrm(ks[14], (DEPTH, D_MODEL), 0.02),
        "w_router": nrm(ks[15], (DEPTH, D_MODEL, N_EXPERTS), D_MODEL ** -0.5),
        "b_router": nrm(ks[16], (DEPTH, N_EXPERTS), 0.01),
        "w_up": nrm(ks[17], (DEPTH, N_EXPERTS, D_MODEL, 2 * D_FF), D_MODEL ** -0.5),
        "b_up": nrm(ks[18], (DEPTH, N_EXPERTS, 2 * D_FF), 0.01),
        "w_down": nrm(ks[19], (DEPTH, N_EXPERTS, D_FF, D_MODEL), D_FF ** -0.5),
        "b_down": nrm(ks[20], (DEPTH, N_EXPERTS, D_MODEL), 0.01),
        "g_final": 1.0 + nrm(ks[21], (D_MODEL,), 0.02),
    }


def reference(x_prompt, x_sample, cache_k, cache_v, state_conv, meta_tokens, g_mix, w_in,
              conv_w, conv_b, conv_ln_g, conv_ln_b, g_attn_out, w_out, g_ffn, w_router,
              b_router, w_up, b_up, w_down, b_down, g_final):
    b = x_prompt.shape[0]
    hp = jnp.concatenate(
        [jnp.broadcast_to(meta_tokens[None].astype(x_prompt.dtype), (b, N_META, D_MODEL)), x_prompt],
        axis=1)
    hs = x_sample
    kp_l, vp_l, cp_l, ks_l, vs_l, cs_l = [], [], [], [], [], []
    for l in range(DEPTH):
        w = (g_mix[l], w_in[l], conv_w[l], conv_b[l], conv_ln_g[l], conv_ln_b[l], g_attn_out[l],
             w_out[l], g_ffn[l], w_router[l], b_router[l], w_up[l], b_up[l], w_down[l], b_down[l])
        empty_kv = jnp.zeros((b, 0, N_HEADS, HEAD_DIM), hp.dtype)
        zero_ctx = jnp.zeros((b, CONV_WIDTH - 1, D_CONV), hp.dtype)
        hp, kp, vp, cp = _layer(hp, zero_ctx, empty_kv, empty_kv, *w)
        hs, ks_, vs_, cs_ = _layer(hs, state_conv[l], cache_k[l], cache_v[l], *w)
        kp_l.append(kp); vp_l.append(vp); cp_l.append(cp)
        ks_l.append(ks_); vs_l.append(vs_); cs_l.append(cs_)
    y_prompt = _rmsnorm(hp, g_final)[:, N_META:]
    y_sample = _rmsnorm(hs, g_final)
    k_prompt = jnp.stack(kp_l)
    v_prompt = jnp.stack(vp_l)
    conv_prompt = jnp.stack(cp_l)
    k_sample = jnp.stack(ks_l)
    v_sample = jnp.stack(vs_l)
    conv_sample = jnp.stack(cs_l)
    return (y_prompt, y_sample, k_prompt, v_prompt, conv_prompt, k_sample, v_sample, conv_sample)
```

```python
import functools
import math

import jax
import jax.numpy as jnp
from jax import lax
from jax.experimental import pallas as pl
from jax.experimental.pallas import tpu as pltpu

F32 = jnp.float32
BF16 = jnp.bfloat16

D_MODEL = 2048
N_META = 16
D_CONV = 1024
N_HEADS = 8
HEAD_DIM = 128
D_ATTN = N_HEADS * HEAD_DIM
D_IN = 2 * D_CONV + 3 * D_ATTN
CONV_WIDTH = 31
N_EXPERTS = 32
TOP_K = 4
D_FF = D_MODEL
SWIGLU_ALPHA = 1.702
SWIGLU_LIMIT = 7.0
EPS = 1e-5

BLK = 256
HALO = 32
COL = 512
ROW_A = 640
ROW_E = 640
ROW_R = 640
TM = 1152
SUB = 384
TF = 256
TB = 256
LOG_ZERO = -104.0
VMEM_LIMIT = 56 * 1024 * 1024


def _cparams(sem, vmem=VMEM_LIMIT):
    return pltpu.CompilerParams(dimension_semantics=sem, vmem_limit_bytes=vmem)


def _inproj_body(x_ref, g_ref, w_ref, u_ref, q_ref, k_ref, v_ref, kb_ref, vb_ref, xn_s, a_s):
    j = pl.program_id(1)

    @pl.when(j == 0)
    def _():
        x = x_ref[...]
        ms = jnp.mean(x * x, axis=-1, keepdims=True)
        xn_s[...] = (x * lax.rsqrt(ms + EPS) * g_ref[...]).astype(BF16)

    p = jnp.dot(xn_s[...], w_ref[...], preferred_element_type=F32)

    @pl.when(j < 2)
    def _():
        a_s[j] = p

    @pl.when((j >= 2) & (j < 4))
    def _():
        u_ref[...] = a_s[j - 2] * jax.nn.sigmoid(p)

    @pl.when((j >= 4) & (j < 6))
    def _():
        q_ref[...] = (p * (1.0 / math.sqrt(HEAD_DIM))).astype(BF16)

    @pl.when((j >= 6) & (j < 8))
    def _():
        k_ref[...] = p
        kb_ref[...] = p.astype(BF16)

    @pl.when(j >= 8)
    def _():
        v_ref[...] = p
        vb_ref[...] = p.astype(BF16)


def _inproj(x_all, g_mix, w_in_bf):
    n = x_all.shape[0]
    n_col = D_IN // COL

    def out_map(first):
        return lambda i, j: (i, jnp.clip(j - first, 0, 1))

    wide = lambda dt: jax.ShapeDtypeStruct((n, D_CONV), dt)
    return pl.pallas_call(
        _inproj_body,
        grid=(n // ROW_A, n_col),
        in_specs=[
            pl.BlockSpec((ROW_A, D_MODEL), lambda i, j: (i, 0)),
            pl.BlockSpec((1, D_MODEL), lambda i, j: (0, 0)),
            pl.BlockSpec((D_MODEL, COL), lambda i, j: (0, j)),
        ],
        out_specs=[
            pl.BlockSpec((ROW_A, COL), out_map(2)),
            pl.BlockSpec((ROW_A, COL), out_map(4)),
            pl.BlockSpec((ROW_A, COL), out_map(6)),
            pl.BlockSpec((ROW_A, COL), out_map(8)),
            pl.BlockSpec((ROW_A, COL), out_map(6)),
            pl.BlockSpec((ROW_A, COL), out_map(8)),
        ],
        out_shape=[wide(F32), wide(BF16), wide(F32), wide(F32), wide(BF16), wide(BF16)],
        scratch_shapes=[pltpu.VMEM((ROW_A, D_MODEL), BF16), pltpu.VMEM((2, ROW_A, COL), F32)],
        compiler_params=_cparams(("arbitrary", "arbitrary")),
        name="inproj",
    )(x_all, g_mix.reshape(1, D_MODEL), w_in_bf)


def _conv_taps(ext_ref, w_ref, b_ref, dst_ref, src_row0, dst_row0, nrows):
    def chunk(cc, carry):
        lanes = pl.ds(pl.multiple_of(cc * 128, 128), 128)
        acc = jnp.broadcast_to(b_ref[:, lanes], (nrows, 128))
        for j in range(CONV_WIDTH):
            acc = acc + ext_ref[pl.ds(src_row0 + j, nrows), lanes] * w_ref[pl.ds(j, 1), lanes]
        dst_ref[pl.ds(dst_row0, nrows), lanes] = acc
        return carry

    lax.fori_loop(0, D_CONV // 128, chunk, 0)


def _conv_body(blocks_per_batch, n_prompt_blocks, u_ref, prev_ref, state_ref, w_ref, b_ref,
               lg_ref, lb_ref, c_ref, ext_s, conv_s):
    i = pl.program_id(0)
    off = HALO - (CONV_WIDTH - 1)

    @pl.when(i < n_prompt_blocks)
    def _():
        first = (i % blocks_per_batch) == 0
        ext_s[0:HALO, :] = jnp.where(first, 0.0, prev_ref[...])
        ext_s[HALO:HALO + BLK, :] = u_ref[...]
        for rc in range(BLK // 64):
            _conv_taps(ext_s, w_ref, b_ref, conv_s, off + rc * 64, rc * 64, 64)

    @pl.when(i >= n_prompt_blocks)
    def _():
        def stream(s, carry):
            r0 = pl.multiple_of(s * 16, 16)
            ext_s[off:HALO, :] = state_ref[s]
            ext_s[HALO:HALO + 16, :] = u_ref[pl.ds(r0, 16), :]
            _conv_taps(ext_s, w_ref, b_ref, conv_s, off, r0, 16)
            return carry

        lax.fori_loop(0, BLK // 16, stream, 0)

    x = conv_s[...]
    mu = jnp.mean(x, axis=-1, keepdims=True)
    xc = x - mu
    var = jnp.mean(xc * xc, axis=-1, keepdims=True)
    y = xc * lax.rsqrt(var + EPS) * lg_ref[...] + lb_ref[...]
    c_ref[...] = (y * jax.nn.sigmoid(y)).astype(BF16)


def _conv(u_all, state_conv, conv_w, conv_b, ln_g, ln_b, blocks_per_batch, n_prompt_blocks):
    n = u_all.shape[0]
    row = lambda a: a.reshape(1, D_CONV)
    const = lambda shape: pl.BlockSpec(shape, lambda i: (0,) * len(shape))
    return pl.pallas_call(
        functools.partial(_conv_body, blocks_per_batch, n_prompt_blocks),
        grid=(n // BLK,),
        in_specs=[
            pl.BlockSpec((BLK, D_CONV), lambda i: (i, 0)),
            pl.BlockSpec((HALO, D_CONV), lambda i: (jnp.maximum(i * (BLK // HALO) - 1, 0), 0)),
            const(state_conv.shape),
            const((CONV_WIDTH, D_CONV)),
            const((1, D_CONV)), const((1, D_CONV)), const((1, D_CONV)),
        ],
        out_specs=pl.BlockSpec((BLK, D_CONV), lambda i: (i, 0)),
        out_shape=jax.ShapeDtypeStruct((n, D_CONV), BF16),
        scratch_shapes=[pltpu.VMEM((HALO + BLK, D_CONV), F32), pltpu.VMEM((BLK, D_CONV), F32)],
        compiler_params=_cparams(("arbitrary",)),
        name="conv",
    )(u_all, u_all, state_conv, conv_w, row(conv_b), row(ln_g), row(ln_b))


def _suffix_matrix(n, extra):
    r = lax.broadcasted_iota(jnp.int32, (n, n + extra), 0)
    c = lax.broadcasted_iota(jnp.int32, (n, n + extra), 1)
    return jnp.where((r > c) | (c >= n), 1.0, 0.0).astype(BF16)


def _split_dot(x, m):
    hi = x.astype(BF16)
    lo = (x - hi.astype(F32)).astype(BF16)
    return (jnp.dot(hi, m, preferred_element_type=F32) + jnp.dot(lo, m, preferred_element_type=F32))


def _log_keep(z):
    return -(jnp.maximum(z, 0.0) + jnp.log(1.0 + jnp.exp(-jnp.abs(z))))


def _attn_prompt_body(q_ref, k_ref, v_ref, g_ref, o_ref, m_s):
    h = pl.program_id(1)
    qi = pl.program_id(2)

    @pl.when((pl.program_id(0) == 0) & (h == 0) & (qi == 0))
    def _():
        m_s[...] = _suffix_matrix(BLK, 128)

    q = q_ref[...]
    m = m_s[...]

    def tile(kb, carry, diagonal):
        rows = pl.ds(pl.multiple_of(kb * BLK, BLK), BLK)
        z = lax.dot_general(q, k_ref[rows, :], (((1,), (1,)), ((), ())), preferred_element_type=F32)
        lk = _log_keep(z)
        if diagonal:
            keep = (lax.broadcasted_iota(jnp.int32, (BLK, BLK), 1)
                    < lax.broadcasted_iota(jnp.int32, (BLK, BLK), 0))
            lk = jnp.where(keep, lk, 0.0)
        cs = _split_dot(lk, m)
        after = cs[:, :BLK]
        if carry is not None:
            after = after + jnp.concatenate([carry, carry], axis=1)
        w = jnp.exp(z + lk + after)
        if diagonal:
            w = jnp.where(keep, w, 0.0)
        pv = jnp.dot(w.astype(BF16), v_ref[rows, :], preferred_element_type=F32)
        return pv, cs[:, BLK:]

    acc0, carry0 = tile(qi, None, True)

    def cond(st):
        kb, _, carry = st
        return (kb >= 0) & (jnp.max(carry) > LOG_ZERO)

    def step(st):
        kb, acc, carry = st
        pv, rs = tile(kb, carry, False)
        return kb - 1, acc + pv, carry + rs

    _, acc, _ = lax.while_loop(cond, step, (qi - 1, acc0, carry0))
    ms = jnp.mean(acc * acc, axis=-1, keepdims=True)
    o_ref[...] = (acc * lax.rsqrt(ms + EPS) * g_ref[pl.ds(h, 1), :]).astype(BF16)


def _attn_prompt(q_all, kb_all, vb_all, g_heads, n_batch, t_pad):
    nq = t_pad // BLK
    return pl.pallas_call(
        _attn_prompt_body,
        grid=(n_batch, N_HEADS, nq),
        in_specs=[
            pl.BlockSpec((BLK, HEAD_DIM), lambda b, h, i: (b * nq + i, h)),
            pl.BlockSpec((t_pad, HEAD_DIM), lambda b, h, i: (b, h)),
            pl.BlockSpec((t_pad, HEAD_DIM), lambda b, h, i: (b, h)),
            pl.BlockSpec((N_HEADS, HEAD_DIM), lambda b, h, i: (0, 0)),
        ],
        out_specs=pl.BlockSpec((BLK, HEAD_DIM), lambda b, h, i: (b * nq + i, h)),
        out_shape=jax.ShapeDtypeStruct((n_batch * t_pad, D_ATTN), BF16),
        scratch_shapes=[pltpu.VMEM((BLK, BLK + 128), BF16)],
        compiler_params=_cparams(("arbitrary", "arbitrary", "arbitrary")),
        name="attn_prompt",
    )(q_all, kb_all, vb_all, g_heads)


def _attn_sample_body(n_kb, kblk, q_ref, kn_ref, vn_ref, kc_ref, vc_ref, g_ref, o_ref,
                      qbd_s, acc_s, carry_s, m_s, mn_s):
    j = pl.program_id(1)
    dq = q_ref.shape[0]

    def cat_heads(ref):
        return jnp.concatenate([ref[0, 0, :, hh, :] for hh in range(N_HEADS)], axis=1).astype(BF16)

    def scores(kcat):
        return jnp.dot(kcat, qbd_s[...], preferred_element_type=F32)

    def add_values(w, vcat):
        full = lax.dot_general(w.astype(BF16), vcat, (((0,), (0,)), ((), ())), preferred_element_type=F32)
        for hh in range(N_HEADS):
            acc_s[hh] += full[hh * dq:(hh + 1) * dq, hh * HEAD_DIM:(hh + 1) * HEAD_DIM]

    @pl.when(j == 0)
    def _():
        qrep = jnp.concatenate([q_ref[...].astype(F32)] * N_HEADS, axis=0)
        qt = qrep.T
        row_head = lax.broadcasted_iota(jnp.int32, (D_ATTN, 128), 0) // HEAD_DIM
        col_head = lax.broadcasted_iota(jnp.int32, (D_ATTN, 128), 1) // dq
        qbd_s[...] = jnp.where(row_head == col_head, qt, 0.0).astype(BF16)
        r = lax.broadcasted_iota(jnp.int32, (kblk, kblk), 0)
        c = lax.broadcasted_iota(jnp.int32, (kblk, kblk), 1)
        m_s[...] = jnp.where(c > r, 1.0, 0.0).astype(BF16)
        mn_s[...] = m_s[0:dq, 0:dq]
        acc_s[...] = jnp.zeros_like(acc_s)
        z = scores(kn_ref[...].astype(BF16))
        lk = _log_keep(z)
        key = lax.broadcasted_iota(jnp.int32, (dq, 128), 0)
        qry = lax.broadcasted_iota(jnp.int32, (dq, 128), 1) % dq
        keep = key < qry
        lk = jnp.where(keep, lk, 0.0)
        after = _split_dot_left(mn_s[...], lk)
        w = jnp.where(keep, jnp.exp(z + lk + after), 0.0)
        add_values(w, vn_ref[...].astype(BF16))
        carry_s[...] = jnp.sum(lk, axis=0, keepdims=True)

    @pl.when(jnp.max(carry_s[...]) > LOG_ZERO)
    def _():
        z = scores(cat_heads(kc_ref))
        lk = _log_keep(z)
        after = _split_dot_left(m_s[...], lk) + carry_s[...]
        w = jnp.exp(z + lk + after)
        add_values(w, cat_heads(vc_ref))
        carry_s[...] += jnp.sum(lk, axis=0, keepdims=True)

    @pl.when(j == n_kb - 1)
    def _():
        for hh in range(N_HEADS):
            a = acc_s[hh]
            ms = jnp.mean(a * a, axis=-1, keepdims=True)
            o_ref[:, hh * HEAD_DIM:(hh + 1) * HEAD_DIM] = (
                a * lax.rsqrt(ms + EPS) * g_ref[pl.ds(hh, 1), :]).astype(BF16)


def _split_dot_left(m, x):
    hi = x.astype(BF16)
    lo = (x - hi.astype(F32)).astype(BF16)
    return (jnp.dot(m, hi, preferred_element_type=F32) + jnp.dot(m, lo, preferred_element_type=F32))


def _attn_sample(q_s, k_s, v_s, cache_k, cache_v, g_heads, dec_seq, kblk):
    n_streams = cache_k.shape[1]
    past = cache_k.shape[2]
    n_kb = past // kblk
    cache_spec = pl.BlockSpec((1, 1, kblk, N_HEADS, HEAD_DIM), lambda s, j: (0, s, n_kb - 1 - j, 0, 0))
    row_spec = pl.BlockSpec((dec_seq, D_ATTN), lambda s, j: (s, 0))
    return pl.pallas_call(
        functools.partial(_attn_sample_body, n_kb, kblk),
        grid=(n_streams, n_kb),
        in_specs=[row_spec, row_spec, row_spec, cache_spec, cache_spec,
                  pl.BlockSpec((N_HEADS, HEAD_DIM), lambda s, j: (0, 0))],
        out_specs=row_spec,
        out_shape=jax.ShapeDtypeStruct((n_streams * dec_seq, D_ATTN), BF16),
        scratch_shapes=[
            pltpu.VMEM((D_ATTN, 128), BF16),
            pltpu.VMEM((N_HEADS, dec_seq, HEAD_DIM), F32),
            pltpu.VMEM((1, 128), F32),
            pltpu.VMEM((kblk, kblk), BF16),
            pltpu.VMEM((dec_seq, dec_seq), BF16),
        ],
        compiler_params=_cparams(("arbitrary", "arbitrary")),
        name="attn_sample",
    )(q_s, k_s, v_s, cache_k, cache_v, g_heads)


def _outproj_body(x_ref, c_ref, o_ref, w_ref, g_ref, wr_ref, br_ref, h_ref, xn_ref, lg_ref):
    h = (x_ref[...]
         + jnp.dot(c_ref[...], w_ref[0:D_CONV, :], preferred_element_type=F32)
         + jnp.dot(o_ref[...], w_ref[D_CONV:, :], preferred_element_type=F32))
    h_ref[...] = h
    ms = jnp.mean(h * h, axis=-1, keepdims=True)
    xn = h * lax.rsqrt(ms + EPS) * g_ref[...]
    xn_ref[...] = xn
    lg_ref[...] = lax.dot_general(wr_ref[...], xn.astype(BF16), (((1,), (1,)), ((), ())),
                                  preferred_element_type=F32) + br_ref[...]


def _outproj(x_all, c_all, o_all, w_out_bf, g_ffn, w_router_t_bf, b_router):
    n = x_all.shape[0]
    const = lambda shape: pl.BlockSpec(shape, lambda i: (0,) * len(shape))
    return pl.pallas_call(
        _outproj_body,
        grid=(n // ROW_E,),
        in_specs=[
            pl.BlockSpec((ROW_E, D_MODEL), lambda i: (i, 0)),
            pl.BlockSpec((ROW_E, D_CONV), lambda i: (i, 0)),
            pl.BlockSpec((ROW_E, D_ATTN), lambda i: (i, 0)),
            const((D_MODEL, D_MODEL)),
            const((1, D_MODEL)),
            const((N_EXPERTS, D_MODEL)),
            const((N_EXPERTS, 1)),
        ],
        out_specs=[
            pl.BlockSpec((ROW_E, D_MODEL), lambda i: (i, 0)),
            pl.BlockSpec((ROW_E, D_MODEL), lambda i: (i, 0)),
            pl.BlockSpec((N_EXPERTS, ROW_E), lambda i: (0, i)),
        ],
        out_shape=[jax.ShapeDtypeStruct((n, D_MODEL), F32), jax.ShapeDtypeStruct((n, D_MODEL), F32),
                   jax.ShapeDtypeStruct((N_EXPERTS, n), F32)],
        compiler_params=_cparams(("arbitrary",)),
        name="outproj",
    )(x_all, c_all, o_all, w_out_bf, g_ffn.reshape(1, D_MODEL), w_router_t_bf,
      b_router.reshape(N_EXPERTS, 1))


def _route_body(t_p, t_pad, n_p, lg_ref, e_ref, r_ref, gt_ref, cnt_ref, tri_s, run_s):
    i = pl.program_id(0)
    tb = lg_ref.shape[1]

    @pl.when(i == 0)
    def _():
        r = lax.broadcasted_iota(jnp.int32, (tb, tb + 128), 0)
        c = lax.broadcasted_iota(jnp.int32, (tb, tb + 128), 1)
        tri_s[...] = jnp.where((r < c) | (c >= tb), 1.0, 0.0).astype(BF16)
        run_s[...] = jnp.zeros_like(run_s)

    tok = i * tb + lax.broadcasted_iota(jnp.int32, (1, tb), 1)
    valid = ((tok % t_pad) < t_p) | (tok >= n_p)
    eid = lax.broadcasted_iota(jnp.int32, (N_EXPERTS, tb), 0)
    lg = lg_ref[...]
    sel = jnp.zeros((N_EXPERTS, tb), F32)
    hot, top, chosen = [], [], []
    for _ in range(TOP_K):
        mx = jnp.max(lg, axis=0, keepdims=True)
        idx = jnp.min(jnp.where(lg == mx, eid, N_EXPERTS), axis=0, keepdims=True)
        one = eid == idx
        lg = jnp.where(one, -jnp.inf, lg)
        hot.append(one)
        top.append(mx)
        chosen.append(idx)
        sel = sel + jnp.where(one & valid, 1.0, 0.0)
    ex = [jnp.exp(t - top[0]) for t in top]
    den = ex[0] + ex[1] + ex[2] + ex[3]
    cs = jnp.dot(sel.astype(BF16), tri_s[...], preferred_element_type=F32)
    run = run_s[...]
    before = cs[:, :tb] + jnp.concatenate([run] * (tb // 128), axis=1)
    for k in range(TOP_K):
        e_ref[pl.ds(k, 1), :] = chosen[k]
        r_ref[pl.ds(k, 1), :] = jnp.sum(jnp.where(hot[k], before, 0.0), axis=0,
                                        keepdims=True).astype(jnp.int32)
        gt_ref[pl.ds(k, 1), :] = ex[k] / den
    run = run + cs[:, tb:]
    run_s[...] = run
    cnt_ref[...] = run.astype(jnp.int32)


def _route(logits_t, t_p, t_pad, n_p):
    n = logits_t.shape[1]
    blk = lambda: pl.BlockSpec((TOP_K, ROW_R), lambda i: (0, i))
    return pl.pallas_call(
        functools.partial(_route_body, t_p, t_pad, n_p),
        grid=(n // ROW_R,),
        in_specs=[pl.BlockSpec((N_EXPERTS, ROW_R), lambda i: (0, i))],
        out_specs=[blk(), blk(), blk(), pl.BlockSpec((N_EXPERTS, 128), lambda i: (0, 0))],
        out_shape=[jax.ShapeDtypeStruct((TOP_K, n), jnp.int32), jax.ShapeDtypeStruct((TOP_K, n), jnp.int32),
                   jax.ShapeDtypeStruct((TOP_K, n), F32), jax.ShapeDtypeStruct((N_EXPERTS, 128), jnp.int32)],
        scratch_shapes=[pltpu.VMEM((ROW_R, ROW_R + 128), BF16), pltpu.VMEM((N_EXPERTS, 128), F32)],
        compiler_params=_cparams(("arbitrary",)),
        name="route",
    )(logits_t)


def _moe_body(ie_ref, nr_ref, nlive_ref, tok_ref, xn_hbm, wg_ref, wl_ref, bg_ref, bl_ref, wd_ref,
              bd_ref, y_ref, xbuf, xb16, wg16, wl16, wd16, sem):
    it = pl.program_id(0)
    j = pl.program_id(1)
    nrows = nr_ref[it]

    def row_copy(tok, r):
        return pltpu.make_async_copy(xn_hbm.at[pl.ds(tok, 1), :], xbuf.at[pl.ds(r, 1), :], sem)

    @pl.when((it < nlive_ref[0]) & (j == 0))
    def _():
        def issue(r, c):
            row_copy(tok_ref[0, 0, r], r).start()
            return c

        def drain(r, c):
            row_copy(0, r).wait()
            return c

        lax.fori_loop(0, nrows, issue, 0)
        lax.fori_loop(0, nrows, drain, 0)
        rid = lax.broadcasted_iota(jnp.int32, (TM, 1), 0)
        xb16[...] = jnp.where(rid < nrows, xbuf[...], 0.0).astype(BF16)
        y_ref[...] = jnp.broadcast_to(bd_ref[0], (TM, D_MODEL))

    @pl.when(it < nlive_ref[0])
    def _():
        wg16[...] = wg_ref[0].astype(BF16)
        wl16[...] = wl_ref[0].astype(BF16)
        wd16[...] = wd_ref[0].astype(BF16)
        for sb in range(TM // SUB):
            @pl.when(sb * SUB < nrows)
            def _():
                rows = pl.ds(sb * SUB, SUB)
                x = xb16[rows, :]
                hg = jnp.dot(x, wg16[...], preferred_element_type=F32) + bg_ref[0]
                hl = jnp.dot(x, wl16[...], preferred_element_type=F32) + bl_ref[0]
                hg = jnp.minimum(hg, SWIGLU_LIMIT)
                hl = jnp.clip(hl, -SWIGLU_LIMIT, SWIGLU_LIMIT)
                a = hg * jax.nn.sigmoid(SWIGLU_ALPHA * hg) * (hl + 1.0)
                y_ref[rows, :] += jnp.dot(a.astype(BF16), wd16[...], preferred_element_type=F32)


def _moe(xn_all, slot_tok, item_expert, item_rows, n_live, w_up, b_up, w_down, b_down):
    max_items = slot_tok.shape[0]
    n_f = D_FF // TF

    def live(it, nl):
        return jnp.minimum(it, nl[0] - 1)

    return pl.pallas_call(
        _moe_body,
        grid_spec=pltpu.PrefetchScalarGridSpec(
            num_scalar_prefetch=3,
            grid=(max_items, n_f),
            in_specs=[
                pl.BlockSpec((1, 1, TM), lambda it, j, ie, nr, nl: (live(it, nl), 0, 0),
                             memory_space=pltpu.SMEM),
                pl.BlockSpec(memory_space=pl.ANY),
                pl.BlockSpec((1, D_MODEL, TF), lambda it, j, ie, nr, nl: (ie[it], 0, jnp.where(it < nl[0], j, n_f - 1))),
                pl.BlockSpec((1, D_MODEL, TF), lambda it, j, ie, nr, nl: (ie[it], 0, n_f + jnp.where(it < nl[0], j, n_f - 1))),
                pl.BlockSpec((1, 1, TF), lambda it, j, ie, nr, nl: (ie[it], 0, jnp.where(it < nl[0], j, n_f - 1))),
                pl.BlockSpec((1, 1, TF), lambda it, j, ie, nr, nl: (ie[it], 0, n_f + jnp.where(it < nl[0], j, n_f - 1))),
                pl.BlockSpec((1, TF, D_MODEL), lambda it, j, ie, nr, nl: (ie[it], jnp.where(it < nl[0], j, n_f - 1), 0)),
                pl.BlockSpec((1, 1, D_MODEL), lambda it, j, ie, nr, nl: (ie[it], 0, 0)),
            ],
            out_specs=pl.BlockSpec((TM, D_MODEL), lambda it, j, ie, nr, nl: (live(it, nl), 0)),
            scratch_shapes=[
                pltpu.VMEM((TM, D_MODEL), F32),
                pltpu.VMEM((TM, D_MODEL), BF16),
                pltpu.VMEM((D_MODEL, TF), BF16),
                pltpu.VMEM((D_MODEL, TF), BF16),
                pltpu.VMEM((TF, D_MODEL), BF16),
                pltpu.SemaphoreType.DMA(()),
            ],
        ),
        out_shape=jax.ShapeDtypeStruct((max_items * TM, D_MODEL), F32),
        compiler_params=_cparams(("arbitrary", "arbitrary")),
        name="moe",
    )(item_expert, item_rows, n_live, slot_tok, xn_all, w_up, w_up,
      b_up.reshape(N_EXPERTS, 1, 2 * D_FF), b_up.reshape(N_EXPERTS, 1, 2 * D_FF), w_down,
      b_down.reshape(N_EXPERTS, 1, D_MODEL))


def _combine_body(dest_ref, ys_hbm, h_ref, gt_ref, gf_ref, y_ref, gbuf, sem):
    def row_copy(slot, k, t):
        return pltpu.make_async_copy(ys_hbm.at[pl.ds(slot, 1), :], gbuf.at[k, pl.ds(t, 1), :], sem)

    def issue(t, c):
        for k in range(TOP_K):
            row_copy(dest_ref[0, k, t], k, t).start()
        return c

    def drain(t, c):
        for k in range(TOP_K):
            row_copy(0, k, t).wait()
        return c

    lax.fori_loop(0, TB, issue, 0)
    lax.fori_loop(0, TB, drain, 0)
    y = h_ref[...]
    for k in range(TOP_K):
        y = y + gbuf[k] * gt_ref[:, k:k + 1]
    ms = jnp.mean(y * y, axis=-1, keepdims=True)
    y_ref[...] = y * lax.rsqrt(ms + EPS) * gf_ref[...]


def _combine(dest, ys, h_all, gates_t, g_final):
    n = h_all.shape[0]
    return pl.pallas_call(
        _combine_body,
        grid=(n // TB,),
        in_specs=[
            pl.BlockSpec((1, TOP_K, TB), lambda i: (i, 0, 0), memory_space=pltpu.SMEM),
            pl.BlockSpec(memory_space=pl.ANY),
            pl.BlockSpec((TB, D_MODEL), lambda i: (i, 0)),
            pl.BlockSpec((TB, TOP_K), lambda i: (i, 0)),
            pl.BlockSpec((1, D_MODEL), lambda i: (0, 0)),
        ],
        out_specs=pl.BlockSpec((TB, D_MODEL), lambda i: (i, 0)),
        out_shape=jax.ShapeDtypeStruct((n, D_MODEL), F32),
        scratch_shapes=[pltpu.VMEM((TOP_K, TB, D_MODEL), F32), pltpu.SemaphoreType.DMA(())],
        compiler_params=_cparams(("arbitrary",)),
        name="combine",
    )(dest, ys, h_all, gates_t, g_final.reshape(1, D_MODEL))


def _dispatch_plan(top_e, rank, counts, valid, max_items):
    n = top_e.shape[1]
    items_per = (counts + TM - 1) // TM
    item_end = jnp.cumsum(items_per)
    item_start = item_end - items_per
    n_live = item_end[-1]
    dest = item_start[top_e] * TM + rank
    dest = jnp.where(valid[None, :], dest, 0)
    tok = jnp.broadcast_to(jnp.arange(n, dtype=jnp.int32)[None, :], dest.shape)
    slot_tok = jnp.zeros((max_items * TM,), jnp.int32).at[
        jnp.where(valid[None, :], dest, max_items * TM).reshape(-1)].set(tok.reshape(-1), mode="drop")
    it = jnp.arange(max_items, dtype=jnp.int32)
    it_live = jnp.minimum(it, n_live - 1)
    item_expert = jnp.minimum(jnp.searchsorted(item_end, it_live, side="right"), N_EXPERTS - 1).astype(jnp.int32)
    item_rows = jnp.clip(counts[item_expert] - (it_live - item_start[item_expert]) * TM, 0, TM)
    item_rows = jnp.where(it < n_live, item_rows, 0).astype(jnp.int32)
    return dest, slot_tok.reshape(max_items, 1, TM), item_expert, item_rows, n_live.reshape(1).astype(jnp.int32)


def kernel(x_prompt, x_sample, cache_k, cache_v, state_conv, meta_tokens, g_mix, w_in, conv_w, conv_b,
           conv_ln_g, conv_ln_b, g_attn_out, w_out, g_ffn, w_router, b_router, w_up, b_up, w_down,
           b_down, g_final):
    n_batch, seq, _ = x_prompt.shape
    n_streams, dec_seq, _ = x_sample.shape
    depth = w_in.shape[0]
    assert depth == 1 and n_streams * dec_seq == BLK
    t_p = N_META + seq
    t_pad = -(-t_p // BLK) * BLK
    n_p = n_batch * t_pad
    n_s = n_streams * dec_seq
    n_all = n_p + n_s
    assert n_all % ROW_A == 0 and n_all % ROW_E == 0 and n_all % ROW_R == 0 and n_all % TB == 0

    meta = jnp.broadcast_to(meta_tokens[None].astype(F32), (n_batch, N_META, D_MODEL))
    pad = jnp.zeros((n_batch, t_pad - t_p, D_MODEL), F32)
    x_all = jnp.concatenate(
        [jnp.concatenate([meta, x_prompt, pad], axis=1).reshape(n_p, D_MODEL),
         x_sample.reshape(n_s, D_MODEL)], axis=0)

    l = 0
    u_all, q_all, k_all, v_all, kb_all, vb_all = _inproj(x_all, g_mix[l], w_in[l].astype(BF16))
    c_all = _conv(u_all, state_conv[l], conv_w[l], conv_b[l], conv_ln_g[l], conv_ln_b[l],
                  t_pad // BLK, n_p // BLK)
    g_heads = g_attn_out[l].reshape(N_HEADS, HEAD_DIM)
    o_p = _attn_prompt(q_all, kb_all, vb_all, g_heads, n_batch, t_pad)
    o_s = _attn_sample(q_all[n_p:], k_all[n_p:], v_all[n_p:], cache_k[l:l + 1], cache_v[l:l + 1],
                       g_heads, dec_seq, 512)
    o_all = jnp.concatenate([o_p, o_s], axis=0)
    h_all, xn_all, logits_t = _outproj(x_all, c_all, o_all, w_out[l].astype(BF16), g_ffn[l],
                                       w_router[l].T.astype(BF16), b_router[l])

    top_e, rank, gates, counts = _route(logits_t, t_p, t_pad, n_p)
    tok = jnp.arange(n_all, dtype=jnp.int32)
    valid = ((tok % t_pad) < t_p) | (tok >= n_p)
    n_valid = n_batch * t_p + n_s
    max_items = -(-(n_valid * TOP_K) // TM) + N_EXPERTS
    dest, slot_tok, item_expert, item_rows, n_live = _dispatch_plan(top_e, rank, counts[:, 0], valid, max_items)
    ys = _moe(xn_all, slot_tok, item_expert, item_rows, n_live, w_up[l], b_up[l], w_down[l], b_down[l])
    dest_blk = dest.reshape(TOP_K, n_all // TB, TB).transpose(1, 0, 2)
    y_all = _combine(dest_blk, ys, h_all, gates.T, g_final)

    def prompt_rows(a, width):
        return a[:n_p].reshape(n_batch, t_pad, width)[:, :t_p]

    y_prompt = prompt_rows(y_all, D_MODEL)[:, N_META:]
    y_sample = y_all[n_p:].reshape(n_streams, dec_seq, D_MODEL)
    k_prompt = prompt_rows(k_all, D_ATTN).reshape(1, n_batch, t_p, N_HEADS, HEAD_DIM)
    v_prompt = prompt_rows(v_all, D_ATTN).reshape(1, n_batch, t_p, N_HEADS, HEAD_DIM)
    conv_prompt = prompt_rows(u_all, D_CONV)[:, t_p - (CONV_WIDTH - 1):][None]
    k_sample = k_all[n_p:].reshape(1, n_streams, dec_seq, N_HEADS, HEAD_DIM)
    v_sample = v_all[n_p:].reshape(1, n_streams, dec_seq, N_HEADS, HEAD_DIM)
    u_s = u_all[n_p:].reshape(n_streams, dec_seq, D_CONV)
    conv_sample = jnp.concatenate([state_conv[l], u_s], axis=1)[:, -(CONV_WIDTH - 1):][None]
    return (y_prompt, y_sample, k_prompt, v_prompt, conv_prompt, k_sample, v_sample, conv_sample)
```

```python
import functools
import math

import jax
import jax.numpy as jnp
from jax import lax
from jax.experimental import pallas as pl
from jax.experimental.pallas import tpu as pltpu

F32 = jnp.float32
BF16 = jnp.bfloat16

D_MODEL = 2048
N_META = 16
D_CONV = 1024
N_HEADS = 8
HEAD_DIM = 128
D_ATTN = N_HEADS * HEAD_DIM
D_IN = 2 * D_CONV + 3 * D_ATTN
CONV_WIDTH = 31
N_EXPERTS = 32
TOP_K = 4
D_FF = D_MODEL
SWIGLU_ALPHA = 1.702
SWIGLU_LIMIT = 7.0
EPS = 1e-5

BLK = 256
HALO = 32
COL = 512
ROW_A = 640
ROW_E = 640
ROW_R = 640
TM = 2048
CH = 128
SUB = 3 * CH
TF = 256
TB = 256
UNROLL = 4
LOG_ZERO = -104.0
VMEM_LIMIT = 56 * 1024 * 1024


def _cparams(sem, vmem=VMEM_LIMIT):
    return pltpu.CompilerParams(dimension_semantics=sem, vmem_limit_bytes=vmem)


def _inproj_body(t_p, t_pad, n_p, x_ref, g_ref, w_ref, u_ref, q_ref, kb_ref, vb_ref,
                 kp_hbm, vp_hbm, ks_hbm, vs_hbm, xn_s, a_s, k4_s, v4_s, ksem, vsem):
    i = pl.program_id(0)
    j = pl.program_id(1)
    heads_per_tile = COL // HEAD_DIM

    @pl.when(j == 0)
    def _():
        x = x_ref[...]
        ms = jnp.mean(x * x, axis=-1, keepdims=True)
        xn_s[...] = (x * lax.rsqrt(ms + EPS) * g_ref[...]).astype(BF16)

    p = jnp.dot(xn_s[...], w_ref[...], preferred_element_type=F32)

    def chunk_copies(tile, src, dst_p, dst_s, sem):
        out = []
        per_batch, full, rem = t_pad // CH, t_p // CH, t_p % CH
        for c in range(ROW_A // CH):
            g = tile * (ROW_A // CH) + c
            is_p = g < n_p // CH
            b, wi = g // per_batch, g % per_batch
            out.append((is_p & (wi < full), pltpu.make_async_copy(
                src.at[pl.ds(c * CH, CH)], dst_p.at[b, pl.ds(pl.multiple_of(wi * CH, CH), CH)], sem)))
            if rem:
                out.append((is_p & (wi == full), pltpu.make_async_copy(
                    src.at[pl.ds(c * CH, rem)], dst_p.at[b, pl.ds(full * CH, rem)], sem)))
            out.append((jnp.logical_not(is_p), pltpu.make_async_copy(
                src.at[pl.ds(c * CH, CH)],
                dst_s.at[pl.ds(pl.multiple_of((g - n_p // CH) * CH, CH), CH)], sem)))
        return out

    def start_all(copies):
        for cond, cp in copies:
            @pl.when(cond)
            def _():
                cp.start()

    def wait_all(copies):
        for cond, cp in copies:
            @pl.when(cond)
            def _():
                cp.wait()

    def store_heads(dst, first_head):
        for hh in range(heads_per_tile):
            dst[:, first_head + hh, :] = p[:, hh * HEAD_DIM:(hh + 1) * HEAD_DIM]

    @pl.when(j < 2)
    def _():
        a_s[j] = p

    @pl.when((j >= 2) & (j < 4))
    def _():
        u_ref[...] = a_s[j - 2] * jax.nn.sigmoid(p)

    @pl.when((j >= 4) & (j < 6))
    def _():
        q_ref[...] = (p * (1.0 / math.sqrt(HEAD_DIM))).astype(BF16)

    @pl.when(j == 6)
    def _():
        @pl.when(i > 0)
        def _():
            wait_all(chunk_copies(i - 1, k4_s, kp_hbm, ks_hbm, ksem))
        kb_ref[...] = p.astype(BF16)
        store_heads(k4_s, 0)

    @pl.when(j == 7)
    def _():
        kb_ref[...] = p.astype(BF16)
        store_heads(k4_s, heads_per_tile)
        start_all(chunk_copies(i, k4_s, kp_hbm, ks_hbm, ksem))

    @pl.when(j == 8)
    def _():
        @pl.when(i > 0)
        def _():
            wait_all(chunk_copies(i - 1, v4_s, vp_hbm, vs_hbm, vsem))
        vb_ref[...] = p.astype(BF16)
        store_heads(v4_s, 0)

    @pl.when(j == 9)
    def _():
        vb_ref[...] = p.astype(BF16)
        store_heads(v4_s, heads_per_tile)
        start_all(chunk_copies(i, v4_s, vp_hbm, vs_hbm, vsem))

        @pl.when(i == pl.num_programs(0) - 1)
        def _():
            wait_all(chunk_copies(i, k4_s, kp_hbm, ks_hbm, ksem))
            wait_all(chunk_copies(i, v4_s, vp_hbm, vs_hbm, vsem))


def _inproj(x_all, g_mix, w_in_bf, n_batch, t_p, t_pad):
    n = x_all.shape[0]
    n_p = n_batch * t_pad
    n_col = D_IN // COL
    assert n_col == 10 and COL == 4 * HEAD_DIM and ROW_A % CH == 0 and t_pad % CH == 0

    def out_map(first):
        return lambda i, j: (i, jnp.clip(j - first, 0, 1))

    wide = lambda dt: jax.ShapeDtypeStruct((n, D_CONV), dt)
    cache_p = jax.ShapeDtypeStruct((n_batch, t_p, N_HEADS, HEAD_DIM), F32)
    cache_s = jax.ShapeDtypeStruct((n - n_p, N_HEADS, HEAD_DIM), F32)
    any_spec = pl.BlockSpec(memory_space=pl.ANY)
    return pl.pallas_call(
        functools.partial(_inproj_body, t_p, t_pad, n_p),
        grid=(n // ROW_A, n_col),
        in_specs=[
            pl.BlockSpec((ROW_A, D_MODEL), lambda i, j: (i, 0)),
            pl.BlockSpec((1, D_MODEL), lambda i, j: (0, 0)),
            pl.BlockSpec((D_MODEL, COL), lambda i, j: (0, j)),
        ],
        out_specs=[
            pl.BlockSpec((ROW_A, COL), out_map(2)),
            pl.BlockSpec((ROW_A, COL), out_map(4)),
            pl.BlockSpec((ROW_A, COL), out_map(6)),
            pl.BlockSpec((ROW_A, COL), out_map(8)),
            any_spec, any_spec, any_spec, any_spec,
        ],
        out_shape=[wide(F32), wide(BF16), wide(BF16), wide(BF16), cache_p, cache_p, cache_s, cache_s],
        scratch_shapes=[pltpu.VMEM((ROW_A, D_MODEL), BF16), pltpu.VMEM((2, ROW_A, COL), F32),
                        pltpu.VMEM((ROW_A, N_HEADS, HEAD_DIM), F32), pltpu.VMEM((ROW_A, N_HEADS, HEAD_DIM), F32),
                        pltpu.SemaphoreType.DMA(()), pltpu.SemaphoreType.DMA(())],
        compiler_params=_cparams(("arbitrary", "arbitrary")),
        name="inproj",
    )(x_all, g_mix.reshape(1, D_MODEL), w_in_bf)


def _conv_taps(ext_ref, w_ref, b_ref, dst_ref, src_row0, dst_row0, nrows):
    def chunk(cc, carry):
        lanes = pl.ds(pl.multiple_of(cc * 128, 128), 128)
        acc = jnp.broadcast_to(b_ref[:, lanes], (nrows, 128))
        for j in range(CONV_WIDTH):
            acc = acc + ext_ref[pl.ds(src_row0 + j, nrows), lanes] * w_ref[pl.ds(j, 1), lanes]
        dst_ref[pl.ds(dst_row0, nrows), lanes] = acc
        return carry

    lax.fori_loop(0, D_CONV // 128, chunk, 0)


def _conv_body(blocks_per_batch, n_prompt_blocks, u_ref, prev_ref, state_ref, w_ref, b_ref,
               lg_ref, lb_ref, c_ref, ext_s, conv_s):
    i = pl.program_id(0)
    off = HALO - (CONV_WIDTH - 1)

    @pl.when(i < n_prompt_blocks)
    def _():
        first = (i % blocks_per_batch) == 0
        ext_s[0:HALO, :] = jnp.where(first, 0.0, prev_ref[...])
        ext_s[HALO:HALO + BLK, :] = u_ref[...]
        for rc in range(BLK // 64):
            _conv_taps(ext_s, w_ref, b_ref, conv_s, off + rc * 64, rc * 64, 64)

    @pl.when(i >= n_prompt_blocks)
    def _():
        def stream(s, carry):
            r0 = pl.multiple_of(s * 16, 16)
            ext_s[off:HALO, :] = state_ref[s]
            ext_s[HALO:HALO + 16, :] = u_ref[pl.ds(r0, 16), :]
            _conv_taps(ext_s, w_ref, b_ref, conv_s, off, r0, 16)
            return carry

        lax.fori_loop(0, BLK // 16, stream, 0)

    x = conv_s[...]
    mu = jnp.mean(x, axis=-1, keepdims=True)
    xc = x - mu
    var = jnp.mean(xc * xc, axis=-1, keepdims=True)
    y = xc * lax.rsqrt(var + EPS) * lg_ref[...] + lb_ref[...]
    c_ref[...] = (y * jax.nn.sigmoid(y)).astype(BF16)


def _conv(u_all, state_conv, conv_w, conv_b, ln_g, ln_b, blocks_per_batch, n_prompt_blocks):
    n = u_all.shape[0]
    row = lambda a: a.reshape(1, D_CONV)
    const = lambda shape: pl.BlockSpec(shape, lambda i: (0,) * len(shape))
    return pl.pallas_call(
        functools.partial(_conv_body, blocks_per_batch, n_prompt_blocks),
        grid=(n // BLK,),
        in_specs=[
            pl.BlockSpec((BLK, D_CONV), lambda i: (i, 0)),
            pl.BlockSpec((HALO, D_CONV), lambda i: (jnp.maximum(i * (BLK // HALO) - 1, 0), 0)),
            const(state_conv.shape),
            const((CONV_WIDTH, D_CONV)),
            const((1, D_CONV)), const((1, D_CONV)), const((1, D_CONV)),
        ],
        out_specs=pl.BlockSpec((BLK, D_CONV), lambda i: (i, 0)),
        out_shape=jax.ShapeDtypeStruct((n, D_CONV), BF16),
        scratch_shapes=[pltpu.VMEM((HALO + BLK, D_CONV), F32), pltpu.VMEM((BLK, D_CONV), F32)],
        compiler_params=_cparams(("arbitrary",)),
        name="conv",
    )(u_all, u_all, state_conv, conv_w, row(conv_b), row(ln_g), row(ln_b))


def _suffix_matrix(n, extra):
    r = lax.broadcasted_iota(jnp.int32, (n, n + extra), 0)
    c = lax.broadcasted_iota(jnp.int32, (n, n + extra), 1)
    return jnp.where((r > c) | (c >= n), 1.0, 0.0).astype(BF16)


def _split_dot(x, m):
    hi = x.astype(BF16)
    lo = (x - hi.astype(F32)).astype(BF16)
    return (jnp.dot(hi, m, preferred_element_type=F32) + jnp.dot(lo, m, preferred_element_type=F32))


def _log_keep(z):
    return -(jnp.maximum(z, 0.0) + jnp.log(1.0 + jnp.exp(-jnp.abs(z))))


def _attn_prompt_body(q_ref, k_ref, v_ref, g_ref, o_ref, m_s):
    h = pl.program_id(1)
    qi = pl.program_id(2)

    @pl.when((pl.program_id(0) == 0) & (h == 0) & (qi == 0))
    def _():
        m_s[...] = _suffix_matrix(BLK, 128)

    q = q_ref[...]
    m = m_s[...]

    def tile(kb, carry, diagonal):
        rows = pl.ds(pl.multiple_of(kb * BLK, BLK), BLK)
        z = lax.dot_general(q, k_ref[rows, :], (((1,), (1,)), ((), ())), preferred_element_type=F32)
        lk = _log_keep(z)
        if diagonal:
            keep = (lax.broadcasted_iota(jnp.int32, (BLK, BLK), 1)
                    < lax.broadcasted_iota(jnp.int32, (BLK, BLK), 0))
            lk = jnp.where(keep, lk, 0.0)
        cs = _split_dot(lk, m)
        after = cs[:, :BLK]
        if carry is not None:
            after = after + jnp.concatenate([carry, carry], axis=1)
        w = jnp.exp(z + lk + after)
        if diagonal:
            w = jnp.where(keep, w, 0.0)
        pv = jnp.dot(w.astype(BF16), v_ref[rows, :], preferred_element_type=F32)
        return pv, cs[:, BLK:]

    acc0, carry0 = tile(qi, None, True)

    def cond(st):
        kb, _, carry = st
        return (kb >= 0) & (jnp.max(carry) > LOG_ZERO)

    def step(st):
        kb, acc, carry = st
        pv, rs = tile(kb, carry, False)
        return kb - 1, acc + pv, carry + rs

    _, acc, _ = lax.while_loop(cond, step, (qi - 1, acc0, carry0))
    ms = jnp.mean(acc * acc, axis=-1, keepdims=True)
    o_ref[...] = (acc * lax.rsqrt(ms + EPS) * g_ref[pl.ds(h, 1), :]).astype(BF16)


def _attn_prompt(q_all, kb_all, vb_all, g_heads, n_batch, t_pad):
    nq = t_pad // BLK
    return pl.pallas_call(
        _attn_prompt_body,
        grid=(n_batch, N_HEADS, nq),
        in_specs=[
            pl.BlockSpec((BLK, HEAD_DIM), lambda b, h, i: (b * nq + i, h)),
            pl.BlockSpec((t_pad, HEAD_DIM), lambda b, h, i: (b, h)),
            pl.BlockSpec((t_pad, HEAD_DIM), lambda b, h, i: (b, h)),
            pl.BlockSpec((N_HEADS, HEAD_DIM), lambda b, h, i: (0, 0)),
        ],
        out_specs=pl.BlockSpec((BLK, HEAD_DIM), lambda b, h, i: (b * nq + i, h)),
        out_shape=jax.ShapeDtypeStruct((n_batch * t_pad, D_ATTN), BF16),
        scratch_shapes=[pltpu.VMEM((BLK, BLK + 128), BF16)],
        compiler_params=_cparams(("arbitrary", "arbitrary", "arbitrary")),
        name="attn_prompt",
    )(q_all, kb_all, vb_all, g_heads)


def _attn_sample_body(n_kb, kblk, q_ref, kn_ref, vn_ref, kc_hbm, vc_hbm, g_ref, o_ref,
                      kbuf, vbuf, qbd_s, acc_s, carry_s, m_s, mn_s, ksem, vsem):
    s = pl.program_id(0)
    n_streams = pl.num_programs(0)
    dq = q_ref.shape[0]
    last = n_kb - 1

    def cache_copies(stream, jb, slot):
        rows = pl.ds(pl.multiple_of(jb * kblk, kblk), kblk)
        return (pltpu.make_async_copy(kc_hbm.at[0, stream, rows], kbuf.at[slot], ksem.at[slot]),
                pltpu.make_async_copy(vc_hbm.at[0, stream, rows], vbuf.at[slot], vsem.at[slot]))

    def start(stream, jb, slot):
        for cp in cache_copies(stream, jb, slot):
            cp.start()

    def wait(stream, jb, slot):
        for cp in cache_copies(stream, jb, slot):
            cp.wait()

    def cat_heads(buf, slot):
        return jnp.concatenate([buf[slot, :, hh, :] for hh in range(N_HEADS)], axis=1).astype(BF16)

    def scores(kcat):
        return jnp.dot(kcat, qbd_s[...], preferred_element_type=F32)

    def add_values(w, vcat):
        full = lax.dot_general(w.astype(BF16), vcat, (((0,), (0,)), ((), ())), preferred_element_type=F32)
        for hh in range(N_HEADS):
            acc_s[hh] += full[hh * dq:(hh + 1) * dq, hh * HEAD_DIM:(hh + 1) * HEAD_DIM]

    @pl.when(s == 0)
    def _():
        start(0, last, last % 2)
        r = lax.broadcasted_iota(jnp.int32, (kblk, kblk), 0)
        c = lax.broadcasted_iota(jnp.int32, (kblk, kblk), 1)
        m_s[...] = jnp.where(c > r, 1.0, 0.0).astype(BF16)
        mn_s[...] = m_s[0:dq, 0:dq]

    qrep = jnp.concatenate([q_ref[...].astype(F32)] * N_HEADS, axis=0)
    qt = qrep.T
    row_head = lax.broadcasted_iota(jnp.int32, (D_ATTN, 128), 0) // HEAD_DIM
    col_head = lax.broadcasted_iota(jnp.int32, (D_ATTN, 128), 1) // dq
    qbd_s[...] = jnp.where(row_head == col_head, qt, 0.0).astype(BF16)
    acc_s[...] = jnp.zeros_like(acc_s)
    z = scores(kn_ref[...])
    lk = _log_keep(z)
    key = lax.broadcasted_iota(jnp.int32, (dq, 128), 0)
    qry = lax.broadcasted_iota(jnp.int32, (dq, 128), 1) % dq
    keep = key < qry
    lk = jnp.where(keep, lk, 0.0)
    after = _split_dot_left(mn_s[...], lk)
    w = jnp.where(keep, jnp.exp(z + lk + after), 0.0)
    add_values(w, vn_ref[...])
    carry_s[...] = jnp.sum(lk, axis=0, keepdims=True)

    def cond(jb):
        return (jb >= 0) & (jnp.max(carry_s[...]) > LOG_ZERO)

    def step(jb):
        slot = jb % 2
        wait(s, jb, slot)

        @pl.when(jb > 0)
        def _():
            start(s, jb - 1, 1 - slot)

        z = scores(cat_heads(kbuf, slot))
        lk = _log_keep(z)
        after = _split_dot_left(m_s[...], lk) + carry_s[...]
        w = jnp.exp(z + lk + after)
        add_values(w, cat_heads(vbuf, slot))
        carry_s[...] += jnp.sum(lk, axis=0, keepdims=True)
        return jb - 1

    jb_end = lax.while_loop(cond, step, last)

    @pl.when(jb_end >= 0)
    def _():
        wait(s, jb_end, jb_end % 2)

    @pl.when(s + 1 < n_streams)
    def _():
        start(s + 1, last, last % 2)

    for hh in range(N_HEADS):
        a = acc_s[hh]
        ms = jnp.mean(a * a, axis=-1, keepdims=True)
        o_ref[:, hh * HEAD_DIM:(hh + 1) * HEAD_DIM] = (
            a * lax.rsqrt(ms + EPS) * g_ref[pl.ds(hh, 1), :]).astype(BF16)


def _split_dot_left(m, x):
    hi = x.astype(BF16)
    lo = (x - hi.astype(F32)).astype(BF16)
    return (jnp.dot(m, hi, preferred_element_type=F32) + jnp.dot(m, lo, preferred_element_type=F32))


def _attn_sample(q_s, k_s, v_s, cache_k, cache_v, g_heads, dec_seq, kblk):
    n_streams = cache_k.shape[1]
    past = cache_k.shape[2]
    n_kb = past // kblk
    assert n_kb * kblk == past
    row_spec = pl.BlockSpec((dec_seq, D_ATTN), lambda s: (s, 0))
    cache_buf = pltpu.VMEM((2, kblk, N_HEADS, HEAD_DIM), cache_k.dtype)
    return pl.pallas_call(
        functools.partial(_attn_sample_body, n_kb, kblk),
        grid=(n_streams,),
        in_specs=[row_spec, row_spec, row_spec, pl.BlockSpec(memory_space=pl.ANY),
                  pl.BlockSpec(memory_space=pl.ANY), pl.BlockSpec((N_HEADS, HEAD_DIM), lambda s: (0, 0))],
        out_specs=row_spec,
        out_shape=jax.ShapeDtypeStruct((n_streams * dec_seq, D_ATTN), BF16),
        scratch_shapes=[
            cache_buf, cache_buf,
            pltpu.VMEM((D_ATTN, 128), BF16),
            pltpu.VMEM((N_HEADS, dec_seq, HEAD_DIM), F32),
            pltpu.VMEM((1, 128), F32),
            pltpu.VMEM((kblk, kblk), BF16),
            pltpu.VMEM((dec_seq, dec_seq), BF16),
            pltpu.SemaphoreType.DMA((2,)), pltpu.SemaphoreType.DMA((2,)),
        ],
        compiler_params=_cparams(("arbitrary",)),
        name="attn_sample",
    )(q_s, k_s, v_s, cache_k, cache_v, g_heads)


def _outproj_body(x_ref, c_ref, o_ref, w_ref, g_ref, wr_ref, br_ref, h_ref, xn_ref, lg_ref):
    h = (x_ref[...]
         + jnp.dot(c_ref[...], w_ref[0:D_CONV, :], preferred_element_type=F32)
         + jnp.dot(o_ref[...], w_ref[D_CONV:, :], preferred_element_type=F32))
    h_ref[...] = h
    ms = jnp.mean(h * h, axis=-1, keepdims=True)
    xn = (h * lax.rsqrt(ms + EPS) * g_ref[...]).astype(BF16)
    bits = lax.bitcast_convert_type(xn.astype(F32), jnp.uint32)
    xn_ref[...] = (bits[:, :D_MODEL // 2] >> 16) | bits[:, D_MODEL // 2:]
    lg_ref[...] = lax.dot_general(wr_ref[...], xn, (((1,), (1,)), ((), ())),
                                  preferred_element_type=F32) + br_ref[...]


def _outproj(x_all, c_all, o_all, w_out_bf, g_ffn, w_router_t_bf, b_router):
    n = x_all.shape[0]
    const = lambda shape: pl.BlockSpec(shape, lambda i: (0,) * len(shape))
    return pl.pallas_call(
        _outproj_body,
        grid=(n // ROW_E,),
        in_specs=[
            pl.BlockSpec((ROW_E, D_MODEL), lambda i: (i, 0)),
            pl.BlockSpec((ROW_E, D_CONV), lambda i: (i, 0)),
            pl.BlockSpec((ROW_E, D_ATTN), lambda i: (i, 0)),
            const((D_MODEL, D_MODEL)),
            const((1, D_MODEL)),
            const((N_EXPERTS, D_MODEL)),
            const((N_EXPERTS, 1)),
        ],
        out_specs=[
            pl.BlockSpec((ROW_E, D_MODEL), lambda i: (i, 0)),
            pl.BlockSpec((ROW_E, D_MODEL // 2), lambda i: (i, 0)),
            pl.BlockSpec((N_EXPERTS, ROW_E), lambda i: (0, i)),
        ],
        out_shape=[jax.ShapeDtypeStruct((n, D_MODEL), F32), jax.ShapeDtypeStruct((n, D_MODEL // 2), jnp.uint32),
                   jax.ShapeDtypeStruct((N_EXPERTS, n), F32)],
        compiler_params=_cparams(("arbitrary",)),
        name="outproj",
    )(x_all, c_all, o_all, w_out_bf, g_ffn.reshape(1, D_MODEL), w_router_t_bf,
      b_router.reshape(N_EXPERTS, 1))


def _route_body(t_p, t_pad, n_p, cap, lg_ref, d_ref, gt_ref, cnt_ref, tri_s, run_s):
    i = pl.program_id(0)
    tb = lg_ref.shape[1]

    @pl.when(i == 0)
    def _():
        r = lax.broadcasted_iota(jnp.int32, (tb, tb + 128), 0)
        c = lax.broadcasted_iota(jnp.int32, (tb, tb + 128), 1)
        tri_s[...] = jnp.where((r < c) | (c >= tb), 1.0, 0.0).astype(BF16)
        run_s[...] = jnp.zeros_like(run_s)

    tok = i * tb + lax.broadcasted_iota(jnp.int32, (1, tb), 1)
    valid = ((tok % t_pad) < t_p) | (tok >= n_p)
    eid = lax.broadcasted_iota(jnp.int32, (N_EXPERTS, tb), 0)
    lg = lg_ref[...]
    sel = jnp.zeros((N_EXPERTS, tb), F32)
    hot, top = [], []
    for _ in range(TOP_K):
        mx = jnp.max(lg, axis=0, keepdims=True)
        idx = jnp.min(jnp.where(lg == mx, eid, N_EXPERTS), axis=0, keepdims=True)
        one = eid == idx
        lg = jnp.where(one, -jnp.inf, lg)
        hot.append(one)
        top.append(mx)
        sel = sel + jnp.where(one & valid, 1.0, 0.0)
    ex = [jnp.exp(t - top[0]) for t in top]
    den = ex[0] + ex[1] + ex[2] + ex[3]
    cs = jnp.dot(sel.astype(BF16), tri_s[...], preferred_element_type=F32)
    run = run_s[...]
    slot = (cs[:, :tb] + jnp.concatenate([run] * (tb // 128), axis=1)
            + (eid * cap).astype(F32))
    for k in range(TOP_K):
        mine = jnp.sum(jnp.where(hot[k], slot, 0.0), axis=0, keepdims=True).astype(jnp.int32)
        trash = N_EXPERTS * cap + (tok % TB) * TOP_K + k
        d_ref[pl.ds(k, 1), :] = jnp.where(valid, mine, trash)
        gt_ref[pl.ds(k, 1), :] = ex[k] / den
    run = run + cs[:, tb:]
    run_s[...] = run
    cnt_ref[...] = run.astype(jnp.int32)


def _route(logits_t, t_p, t_pad, n_p, cap):
    n = logits_t.shape[1]
    blk = lambda: pl.BlockSpec((TOP_K, ROW_R), lambda i: (0, i))
    return pl.pallas_call(
        functools.partial(_route_body, t_p, t_pad, n_p, cap),
        grid=(n // ROW_R,),
        in_specs=[pl.BlockSpec((N_EXPERTS, ROW_R), lambda i: (0, i))],
        out_specs=[blk(), blk(), pl.BlockSpec((N_EXPERTS, 128), lambda i: (0, 0))],
        out_shape=[jax.ShapeDtypeStruct((TOP_K, n), jnp.int32),
                   jax.ShapeDtypeStruct((TOP_K, n), F32), jax.ShapeDtypeStruct((N_EXPERTS, 128), jnp.int32)],
        scratch_shapes=[pltpu.VMEM((ROW_R, ROW_R + 128), BF16), pltpu.VMEM((N_EXPERTS, 128), F32)],
        compiler_params=_cparams(("arbitrary",)),
        name="route",
    )(logits_t)


def _dispatch_body(dest_ref, xp_ref, xs_hbm, sem):
    def row_copy(t, slot):
        return pltpu.make_async_copy(xp_ref.at[pl.ds(t, 1), :], xs_hbm.at[pl.ds(slot, 1), :], sem)

    def issue(i, c):
        for u in range(UNROLL):
            t = i * UNROLL + u
            for k in range(TOP_K):
                row_copy(t, dest_ref[0, k, t]).start()
        return c

    def drain(i, c):
        for _ in range(UNROLL * TOP_K):
            row_copy(0, 0).wait()
        return c

    lax.fori_loop(0, TB // UNROLL, issue, 0)
    lax.fori_loop(0, TB // UNROLL, drain, 0)


def _dispatch(dest_blk, xn_packed, n_slots):
    n = xn_packed.shape[0]
    return pl.pallas_call(
        _dispatch_body,
        grid=(n // TB,),
        in_specs=[
            pl.BlockSpec((1, TOP_K, TB), lambda i: (i, 0, 0), memory_space=pltpu.SMEM),
            pl.BlockSpec((TB, D_MODEL // 2), lambda i: (i, 0)),
        ],
        out_specs=pl.BlockSpec(memory_space=pl.ANY),
        out_shape=jax.ShapeDtypeStruct((n_slots, D_MODEL // 2), jnp.uint32),
        scratch_shapes=[pltpu.SemaphoreType.DMA(())],
        compiler_params=_cparams(("arbitrary",)),
        name="dispatch",
    )(dest_blk, xn_packed)


def _moe_body(ie_ref, r0_ref, nr_ref, nlive_ref, xs_hbm, wg_ref, wl_ref, bg_ref, bl_ref, wd_ref, bd_ref,
              ys_hbm, xraw, xb16, yacc, wg16, wl16, wd16, pend_s, xsem, ysem):
    it = pl.program_id(0)
    j = pl.program_id(1)
    n_live = nlive_ref[0]
    n_f = pl.num_programs(1)

    def chunks(item):
        return (nr_ref[item] + CH - 1) // CH

    def x_copy(item, c):
        src = pl.multiple_of(r0_ref[item] + c * CH, CH)
        return pltpu.make_async_copy(xs_hbm.at[pl.ds(src, CH), :],
                                     xraw.at[pl.ds(pl.multiple_of(c * CH, CH), CH), :], xsem)

    def y_copy(item, c):
        dst = pl.multiple_of(r0_ref[item] + c * CH, CH)
        return pltpu.make_async_copy(yacc.at[pl.ds(pl.multiple_of(c * CH, CH), CH), :],
                                     ys_hbm.at[pl.ds(dst, CH), :], ysem)

    def for_chunks(n, fn):
        def body(c, carry):
            fn(c)
            return carry
        lax.fori_loop(0, n, body, 0)

    @pl.when((it == 0) & (j == 0))
    def _():
        pend_s[0] = 0
        for_chunks(chunks(0), lambda c: x_copy(0, c).start())

    @pl.when((it < n_live) & (j == 0))
    def _():
        nrows = nr_ref[it]
        for_chunks(chunks(it), lambda c: x_copy(it, c).wait())

        def unpack(c):
            rows = pl.ds(pl.multiple_of(c * CH, CH), CH)
            p = xraw[rows, :]
            rid = c * CH + lax.broadcasted_iota(jnp.int32, (CH, 1), 0)
            live = rid < nrows
            lo = lax.bitcast_convert_type(p << 16, F32)
            hi = lax.bitcast_convert_type(p & jnp.uint32(0xFFFF0000), F32)
            xb16[rows, 0:D_MODEL // 2] = jnp.where(live, lo, 0.0).astype(BF16)
            xb16[rows, D_MODEL // 2:] = jnp.where(live, hi, 0.0).astype(BF16)

        for_chunks(chunks(it), unpack)

        @pl.when(it + 1 < n_live)
        def _():
            for_chunks(chunks(it + 1), lambda c: x_copy(it + 1, c).start())

    @pl.when(it < n_live)
    def _():
        nrows = nr_ref[it]
        wg16[...] = wg_ref[0].astype(BF16)
        wl16[...] = wl_ref[0].astype(BF16)
        wd16[...] = wd_ref[0].astype(BF16)

        @pl.when(j == 0)
        def _():
            for_chunks(pend_s[0], lambda c: y_copy(it, 0).wait())
            pend_s[0] = 0

        def ffn_rows(r0, n):
            rows = pl.ds(r0, n)
            x = xb16[rows, :]
            hg = jnp.dot(x, wg16[...], preferred_element_type=F32) + bg_ref[0]
            hl = jnp.dot(x, wl16[...], preferred_element_type=F32) + bl_ref[0]
            hg = jnp.minimum(hg, SWIGLU_LIMIT)
            hl = jnp.clip(hl, -SWIGLU_LIMIT, SWIGLU_LIMIT)
            a = hg * jax.nn.sigmoid(SWIGLU_ALPHA * hg) * (hl + 1.0)
            part = jnp.dot(a.astype(BF16), wd16[...], preferred_element_type=F32)
            yacc[rows, :] = jnp.where(j == 0, bd_ref[0], yacc[rows, :]) + part

        n_ch = chunks(it)
        n_full = n_ch // (SUB // CH)
        tail = n_ch - n_full * (SUB // CH)

        def full(s, carry):
            ffn_rows(pl.multiple_of(s * SUB, CH), SUB)
            return carry

        lax.fori_loop(0, n_full, full, 0)
        for t in range(1, SUB // CH):
            @pl.when(tail == t)
            def _():
                ffn_rows(pl.multiple_of(n_full * SUB, CH), t * CH)

        @pl.when(j == n_f - 1)
        def _():
            for_chunks(n_ch, lambda c: y_copy(it, c).start())
            pend_s[0] = n_ch

            @pl.when(it == n_live - 1)
            def _():
                for_chunks(n_ch, lambda c: y_copy(it, c).wait())
                pend_s[0] = 0


def _moe(xs, item_expert, item_row0, item_rows, n_live, w_up, b_up, w_down, b_down):
    max_items = item_expert.shape[0]
    n_f = D_FF // TF

    def tile(it, j, nl):
        return jnp.where(it < nl[0], j, n_f - 1)

    return pl.pallas_call(
        _moe_body,
        grid_spec=pltpu.PrefetchScalarGridSpec(
            num_scalar_prefetch=4,
            grid=(max_items, n_f),
            in_specs=[
                pl.BlockSpec(memory_space=pl.ANY),
                pl.BlockSpec((1, D_MODEL, TF), lambda it, j, ie, r0, nr, nl: (ie[it], 0, tile(it, j, nl))),
                pl.BlockSpec((1, D_MODEL, TF), lambda it, j, ie, r0, nr, nl: (ie[it], 0, n_f + tile(it, j, nl))),
                pl.BlockSpec((1, 1, TF), lambda it, j, ie, r0, nr, nl: (ie[it], 0, tile(it, j, nl))),
                pl.BlockSpec((1, 1, TF), lambda it, j, ie, r0, nr, nl: (ie[it], 0, n_f + tile(it, j, nl))),
                pl.BlockSpec((1, TF, D_MODEL), lambda it, j, ie, r0, nr, nl: (ie[it], tile(it, j, nl), 0)),
                pl.BlockSpec((1, 1, D_MODEL), lambda it, j, ie, r0, nr, nl: (ie[it], 0, 0)),
            ],
            out_specs=pl.BlockSpec(memory_space=pl.ANY),
            scratch_shapes=[
                pltpu.VMEM((TM, D_MODEL // 2), jnp.uint32),
                pltpu.VMEM((TM, D_MODEL), BF16),
                pltpu.VMEM((TM, D_MODEL), F32),
                pltpu.VMEM((D_MODEL, TF), BF16),
                pltpu.VMEM((D_MODEL, TF), BF16),
                pltpu.VMEM((TF, D_MODEL), BF16),
                pltpu.SMEM((1,), jnp.int32),
                pltpu.SemaphoreType.DMA(()),
                pltpu.SemaphoreType.DMA(()),
            ],
        ),
        out_shape=jax.ShapeDtypeStruct((xs.shape[0], D_MODEL), F32),
        compiler_params=_cparams(("arbitrary", "arbitrary")),
        name="moe",
    )(item_expert, item_row0, item_rows, n_live, xs, w_up, w_up,
      b_up.reshape(N_EXPERTS, 1, 2 * D_FF), b_up.reshape(N_EXPERTS, 1, 2 * D_FF), w_down,
      b_down.reshape(N_EXPERTS, 1, D_MODEL))


def _combine_body(blocks_per_batch, n_prompt_blocks, t_pad, n_p, dest_ref, ys_hbm, h_hbm, gt_ref, gf_ref,
                  yp_ref, ysm_ref, gbuf, hbuf, gsem, hsem):
    i = pl.program_id(0)
    n_blocks = pl.num_programs(0)

    def h_copy(blk, slot):
        row = jnp.where(blk < n_prompt_blocks,
                        (blk // blocks_per_batch) * t_pad + N_META + (blk % blocks_per_batch) * TB, n_p)
        return pltpu.make_async_copy(h_hbm.at[pl.ds(pl.multiple_of(row, 8), TB), :], hbuf.at[slot],
                                     hsem.at[slot])

    def row_copy(slot, k, t):
        return pltpu.make_async_copy(ys_hbm.at[pl.ds(slot, 1), :], gbuf.at[k, pl.ds(t, 1), :], gsem)

    def issue(c, carry):
        for u in range(UNROLL):
            t = c * UNROLL + u
            for k in range(TOP_K):
                row_copy(dest_ref[0, k, t], k, t).start()
        return carry

    def drain(c, carry):
        for _ in range(UNROLL * TOP_K):
            row_copy(0, 0, 0).wait()
        return carry

    @pl.when(i == 0)
    def _():
        h_copy(0, 0).start()

    @pl.when(i + 1 < n_blocks)
    def _():
        h_copy(i + 1, (i + 1) % 2).start()

    lax.fori_loop(0, TB // UNROLL, issue, 0)
    h_copy(i, i % 2).wait()
    lax.fori_loop(0, TB // UNROLL, drain, 0)
    y = hbuf[i % 2]
    for k in range(TOP_K):
        y = y + gbuf[k] * gt_ref[:, k:k + 1]
    ms = jnp.mean(y * y, axis=-1, keepdims=True)
    y = y * lax.rsqrt(ms + EPS) * gf_ref[...]

    @pl.when(i < n_prompt_blocks)
    def _():
        yp_ref[...] = y

    @pl.when(i >= n_prompt_blocks)
    def _():
        ysm_ref[...] = y


def _combine(dest_blk, ys, h_all, gates_t, g_final, blocks_per_batch, n_prompt_blocks, t_pad, n_p):
    return pl.pallas_call(
        functools.partial(_combine_body, blocks_per_batch, n_prompt_blocks, t_pad, n_p),
        grid=(n_prompt_blocks + 1,),
        in_specs=[
            pl.BlockSpec((1, TOP_K, TB), lambda i: (i, 0, 0), memory_space=pltpu.SMEM),
            pl.BlockSpec(memory_space=pl.ANY),
            pl.BlockSpec(memory_space=pl.ANY),
            pl.BlockSpec((TB, TOP_K), lambda i: (i, 0)),
            pl.BlockSpec((1, D_MODEL), lambda i: (0, 0)),
        ],
        out_specs=[
            pl.BlockSpec((TB, D_MODEL), lambda i: (jnp.minimum(i, n_prompt_blocks - 1), 0)),
            pl.BlockSpec((TB, D_MODEL), lambda i: (0, 0)),
        ],
        out_shape=[jax.ShapeDtypeStruct((n_prompt_blocks * TB, D_MODEL), F32),
                   jax.ShapeDtypeStruct((TB, D_MODEL), F32)],
        scratch_shapes=[pltpu.VMEM((TOP_K, TB, D_MODEL), F32), pltpu.VMEM((2, TB, D_MODEL), F32),
                        pltpu.SemaphoreType.DMA(()), pltpu.SemaphoreType.DMA((2,))],
        compiler_params=_cparams(("arbitrary",)),
        name="combine",
    )(dest_blk, ys, h_all, gates_t, g_final.reshape(1, D_MODEL))


def _work_items(counts, cap, max_items):
    items_per = (counts + TM - 1) // TM
    item_end = jnp.cumsum(items_per)
    item_start = item_end - items_per
    n_live = item_end[-1]
    it = jnp.arange(max_items, dtype=jnp.int32)
    it_live = jnp.minimum(it, n_live - 1)
    expert = jnp.minimum(jnp.searchsorted(item_end, it_live, side="right"), N_EXPERTS - 1).astype(jnp.int32)
    part = it_live - item_start[expert]
    rows = jnp.clip(counts[expert] - part * TM, 0, TM)
    rows = jnp.where(it < n_live, rows, 0).astype(jnp.int32)
    row0 = (expert * cap + part * TM).astype(jnp.int32)
    return expert, row0, rows, n_live.reshape(1).astype(jnp.int32)


def kernel(x_prompt, x_sample, cache_k, cache_v, state_conv, meta_tokens, g_mix, w_in, conv_w, conv_b,
           conv_ln_g, conv_ln_b, g_attn_out, w_out, g_ffn, w_router, b_router, w_up, b_up, w_down,
           b_down, g_final):
    n_batch, seq, _ = x_prompt.shape
    n_streams, dec_seq, _ = x_sample.shape
    depth = w_in.shape[0]
    assert depth == 1 and n_streams * dec_seq == BLK
    t_p = N_META + seq
    t_pad = -(-t_p // BLK) * BLK
    n_p = n_batch * t_pad
    n_s = n_streams * dec_seq
    n_all = n_p + n_s
    assert n_all % ROW_A == 0 and n_all % ROW_E == 0 and n_all % ROW_R == 0 and n_all % TB == 0

    meta = jnp.broadcast_to(meta_tokens[None].astype(F32), (n_batch, N_META, D_MODEL))
    pad = jnp.zeros((n_batch, t_pad - t_p, D_MODEL), F32)
    x_all = jnp.concatenate(
        [jnp.concatenate([meta, x_prompt, pad], axis=1).reshape(n_p, D_MODEL),
         x_sample.reshape(n_s, D_MODEL)], axis=0)

    l = 0
    u_all, q_all, kb_all, vb_all, k_p, v_p, k_s, v_s = _inproj(
        x_all, g_mix[l], w_in[l].astype(BF16), n_batch, t_p, t_pad)
    c_all = _conv(u_all, state_conv[l], conv_w[l], conv_b[l], conv_ln_g[l], conv_ln_b[l],
                  t_pad // BLK, n_p // BLK)
    g_heads = g_attn_out[l].reshape(N_HEADS, HEAD_DIM)
    o_p = _attn_prompt(q_all, kb_all, vb_all, g_heads, n_batch, t_pad)
    o_s = _attn_sample(q_all[n_p:], kb_all[n_p:], vb_all[n_p:], cache_k[l:l + 1], cache_v[l:l + 1],
                       g_heads, dec_seq, BLK)
    o_all = jnp.concatenate([o_p, o_s], axis=0)
    h_all, xn_packed, logits_t = _outproj(x_all, c_all, o_all, w_out[l].astype(BF16), g_ffn[l],
                                          w_router[l].T.astype(BF16), b_router[l])

    n_valid = n_batch * t_p + n_s
    cap = -(-n_valid // CH) * CH
    n_slots = N_EXPERTS * cap + TB * TOP_K
    max_items = -(-(n_valid * TOP_K) // TM) + N_EXPERTS
    dest, gates, counts = _route(logits_t, t_p, t_pad, n_p, cap)
    xs = _dispatch(dest.reshape(TOP_K, n_all // TB, TB).transpose(1, 0, 2), xn_packed, n_slots)
    item_expert, item_row0, item_rows, n_live = _work_items(counts[:, 0], cap, max_items)
    ys = _moe(xs, item_expert, item_row0, item_rows, n_live, w_up[l], b_up[l], w_down[l], b_down[l])

    def out_tokens(a):
        a_p = a[:, :n_p].reshape(-1, n_batch, t_pad)[:, :, N_META:t_p].reshape(-1, n_batch * seq)
        return jnp.concatenate([a_p, a[:, n_p:]], axis=1)

    assert seq % TB == 0 and n_s == TB
    n_out_blocks = n_batch * seq // TB
    dest_out = out_tokens(dest).reshape(TOP_K, n_out_blocks + 1, TB).transpose(1, 0, 2)
    y_p, y_s = _combine(dest_out, ys, h_all, out_tokens(gates).T, g_final, seq // TB, n_out_blocks, t_pad, n_p)

    def prompt_rows(a, width):
        return a[:n_p].reshape(n_batch, t_pad, width)[:, :t_p]

    y_prompt = y_p.reshape(n_batch, seq, D_MODEL)
    y_sample = y_s.reshape(n_streams, dec_seq, D_MODEL)
    k_prompt = k_p[None]
    v_prompt = v_p[None]
    conv_prompt = prompt_rows(u_all, D_CONV)[:, t_p - (CONV_WIDTH - 1):][None]
    k_sample = k_s.reshape(1, n_streams, dec_seq, N_HEADS, HEAD_DIM)
    v_sample = v_s.reshape(1, n_streams, dec_seq, N_HEADS, HEAD_DIM)
    u_s = u_all[n_p:].reshape(n_streams, dec_seq, D_CONV)
    conv_sample = jnp.concatenate([state_conv[l], u_s], axis=1)[:, -(CONV_WIDTH - 1):][None]
    return (y_prompt, y_sample, k_prompt, v_prompt, conv_prompt, k_sample, v_sample, conv_sample)
```

```python
import functools
import math

import jax
import jax.numpy as jnp
from jax import lax
from jax.experimental import pallas as pl
from jax.experimental.pallas import tpu as pltpu

F32 = jnp.float32
BF16 = jnp.bfloat16

D_MODEL = 2048
N_META = 16
D_CONV = 1024
N_HEADS = 8
HEAD_DIM = 128
D_ATTN = N_HEADS * HEAD_DIM
D_IN = 2 * D_CONV + 3 * D_ATTN
CONV_WIDTH = 31
N_EXPERTS = 32
TOP_K = 4
D_FF = D_MODEL
SWIGLU_ALPHA = 1.702
SWIGLU_LIMIT = 7.0
EPS = 1e-5

BLK = 256
HPS = 8
HALO = 32
COL = 512
ROW_A = 640
ROW_E = 640
ROW_R = 640
TM = 2048
CH = 128
SUB = 3 * CH
TF = 256
TB = 256
LOG_ZERO = -104.0
VMEM_LIMIT = 56 * 1024 * 1024


def _cparams(sem, vmem=VMEM_LIMIT):
    return pltpu.CompilerParams(dimension_semantics=sem, vmem_limit_bytes=vmem)


def _inproj_body(t_p, t_pad, n_p, x_ref, g_ref, wa_ref, wg_ref, wq_ref, w1_ref, w2_ref,
                 u_ref, q_ref, kvb_ref, kp_hbm, vp_hbm, ks_hbm, vs_hbm, xn_s, kv4_s, ksem, vsem):
    i = pl.program_id(0)
    j = pl.program_id(1)
    heads_per_tile = COL // HEAD_DIM
    k4_s, v4_s = kv4_s.at[0], kv4_s.at[1]

    def chunk_copies(tile, src, dst_p, dst_s, sem):
        out = []
        per_batch, full, rem = t_pad // CH, t_p // CH, t_p % CH
        for c in range(ROW_A // CH):
            g = tile * (ROW_A // CH) + c
            is_p = g < n_p // CH
            b, wi = g // per_batch, g % per_batch
            out.append((is_p & (wi < full), pltpu.make_async_copy(
                src.at[pl.ds(c * CH, CH)], dst_p.at[b, pl.ds(pl.multiple_of(wi * CH, CH), CH)], sem)))
            if rem:
                out.append((is_p & (wi == full), pltpu.make_async_copy(
                    src.at[pl.ds(c * CH, rem)], dst_p.at[b, pl.ds(full * CH, rem)], sem)))
            out.append((jnp.logical_not(is_p), pltpu.make_async_copy(
                src.at[pl.ds(c * CH, CH)],
                dst_s.at[pl.ds(pl.multiple_of((g - n_p // CH) * CH, CH), CH)], sem)))
        return out

    def start_all(copies):
        for cond, cp in copies:
            @pl.when(cond)
            def _():
                cp.start()

    def wait_all(copies):
        for cond, cp in copies:
            @pl.when(cond)
            def _():
                cp.wait()

    def store_heads(dst, p, first_head):
        for hh in range(heads_per_tile):
            dst[:, first_head + hh, :] = p[:, hh * HEAD_DIM:(hh + 1) * HEAD_DIM]

    @pl.when(j == 0)
    def _():
        x = x_ref[...]
        ms = jnp.mean(x * x, axis=-1, keepdims=True)
        xn_s[...] = (x * lax.rsqrt(ms + EPS) * g_ref[...]).astype(BF16)

        @pl.when(i > 0)
        def _():
            wait_all(chunk_copies(i - 1, k4_s, kp_hbm, ks_hbm, ksem))

    @pl.when((j == 1) & (i > 0))
    def _():
        wait_all(chunk_copies(i - 1, v4_s, vp_hbm, vs_hbm, vsem))

    xn = xn_s[...]
    proj = lambda w_ref: jnp.dot(xn, w_ref[...], preferred_element_type=F32)
    u_ref[...] = proj(wa_ref) * jax.nn.sigmoid(proj(wg_ref))
    q_ref[...] = (proj(wq_ref) * (1.0 / math.sqrt(HEAD_DIM))).astype(BF16)
    for t, w_ref in enumerate((w1_ref, w2_ref)):
        p = proj(w_ref)
        kvb_ref[0, :, t * COL:(t + 1) * COL] = p.astype(BF16)
        store_heads(kv4_s.at[j], p, t * heads_per_tile)

    @pl.when(j == 0)
    def _():
        start_all(chunk_copies(i, k4_s, kp_hbm, ks_hbm, ksem))

    @pl.when(j == 1)
    def _():
        start_all(chunk_copies(i, v4_s, vp_hbm, vs_hbm, vsem))

        @pl.when(i == pl.num_programs(0) - 1)
        def _():
            wait_all(chunk_copies(i, k4_s, kp_hbm, ks_hbm, ksem))
            wait_all(chunk_copies(i, v4_s, vp_hbm, vs_hbm, vsem))


def _inproj(x_all, g_mix, w_in_bf, n_batch, t_p, t_pad):
    n = x_all.shape[0]
    n_p = n_batch * t_pad
    assert D_IN == 10 * COL and COL == 4 * HEAD_DIM and ROW_A % CH == 0 and t_pad % CH == 0

    def w_spec(first, stride):
        return pl.BlockSpec((D_MODEL, COL), lambda i, j: (0, first + stride * j))

    wide = lambda dt: jax.ShapeDtypeStruct((n, D_CONV), dt)
    cache_p = jax.ShapeDtypeStruct((n_batch, t_p, N_HEADS, HEAD_DIM), F32)
    cache_s = jax.ShapeDtypeStruct((n - n_p, N_HEADS, HEAD_DIM), F32)
    any_spec = pl.BlockSpec(memory_space=pl.ANY)
    half_spec = pl.BlockSpec((ROW_A, COL), lambda i, j: (i, j))
    return pl.pallas_call(
        functools.partial(_inproj_body, t_p, t_pad, n_p),
        grid=(n // ROW_A, 2),
        in_specs=[
            pl.BlockSpec((ROW_A, D_MODEL), lambda i, j: (i, 0)),
            pl.BlockSpec((1, D_MODEL), lambda i, j: (0, 0)),
            w_spec(0, 1), w_spec(2, 1), w_spec(4, 1), w_spec(6, 2), w_spec(7, 2),
        ],
        out_specs=[half_spec, half_spec, pl.BlockSpec((1, ROW_A, D_ATTN), lambda i, j: (j, i, 0)),
                   any_spec, any_spec, any_spec, any_spec],
        out_shape=[wide(F32), wide(BF16), jax.ShapeDtypeStruct((2, n, D_ATTN), BF16),
                   cache_p, cache_p, cache_s, cache_s],
        scratch_shapes=[pltpu.VMEM((ROW_A, D_MODEL), BF16),
                        pltpu.VMEM((2, ROW_A, N_HEADS, HEAD_DIM), F32),
                        pltpu.SemaphoreType.DMA(()), pltpu.SemaphoreType.DMA(())],
        compiler_params=_cparams(("arbitrary", "arbitrary")),
        name="inproj",
    )(x_all, g_mix.reshape(1, D_MODEL), w_in_bf, w_in_bf, w_in_bf, w_in_bf, w_in_bf)


def _conv_taps(ext_ref, w_ref, b_ref, dst_ref, src_row0, dst_row0, nrows):
    def chunk(cc, carry):
        lanes = pl.ds(pl.multiple_of(cc * 128, 128), 128)
        acc = jnp.broadcast_to(b_ref[:, lanes], (nrows, 128))
        for j in range(CONV_WIDTH):
            acc = acc + ext_ref[pl.ds(src_row0 + j, nrows), lanes] * w_ref[pl.ds(j, 1), lanes]
        dst_ref[pl.ds(dst_row0, nrows), lanes] = acc
        return carry

    lax.fori_loop(0, D_CONV // 128, chunk, 0)


def _conv_block(ext_ref, sh_ref, w_ref, b_ref, dst_ref, off):
    span = HALO + BLK

    def chunk(cc, carry):
        lanes = pl.ds(pl.multiple_of(cc * 128, 128), 128)
        for r in range(8):
            n = span - (8 if r else 0)
            sh_ref[r, 0:n, :] = ext_ref[pl.ds(r, n), lanes]
        for rc in range(BLK // 64):
            acc = jnp.broadcast_to(b_ref[:, lanes], (64, 128))
            for j in range(CONV_WIDTH):
                r = (off + j) % 8
                acc = acc + sh_ref[r, pl.ds(off + j - r + rc * 64, 64), :] * w_ref[pl.ds(j, 1), lanes]
            dst_ref[pl.ds(rc * 64, 64), lanes] = acc
        return carry

    lax.fori_loop(0, D_CONV // 128, chunk, 0)


def _conv_body(blocks_per_batch, n_prompt_blocks, u_ref, prev_ref, state_ref, w_ref, b_ref,
               lg_ref, lb_ref, c_ref, ext_s, conv_s, sh_s):
    i = pl.program_id(0)
    off = HALO - (CONV_WIDTH - 1)

    @pl.when(i < n_prompt_blocks)
    def _():
        first = (i % blocks_per_batch) == 0
        ext_s[0:HALO, :] = jnp.where(first, 0.0, prev_ref[...])
        ext_s[HALO:HALO + BLK, :] = u_ref[...]
        _conv_block(ext_s, sh_s, w_ref, b_ref, conv_s, off)

    @pl.when(i >= n_prompt_blocks)
    def _():
        def stream(s, carry):
            r0 = pl.multiple_of(s * 16, 16)
            ext_s[off:HALO, :] = state_ref[s]
            ext_s[HALO:HALO + 16, :] = u_ref[pl.ds(r0, 16), :]
            _conv_taps(ext_s, w_ref, b_ref, conv_s, off, r0, 16)
            return carry

        lax.fori_loop(0, BLK // 16, stream, 0)

    x = conv_s[...]
    mu = jnp.mean(x, axis=-1, keepdims=True)
    xc = x - mu
    var = jnp.mean(xc * xc, axis=-1, keepdims=True)
    y = xc * lax.rsqrt(var + EPS) * lg_ref[...] + lb_ref[...]
    c_ref[...] = (y * jax.nn.sigmoid(y)).astype(BF16)


def _conv(u_all, state_conv, conv_w, conv_b, ln_g, ln_b, blocks_per_batch, n_prompt_blocks):
    n = u_all.shape[0]
    row = lambda a: a.reshape(1, D_CONV)
    const = lambda shape: pl.BlockSpec(shape, lambda i: (0,) * len(shape))
    return pl.pallas_call(
        functools.partial(_conv_body, blocks_per_batch, n_prompt_blocks),
        grid=(n // BLK,),
        in_specs=[
            pl.BlockSpec((BLK, D_CONV), lambda i: (i, 0)),
            pl.BlockSpec((HALO, D_CONV), lambda i: (jnp.maximum(i * (BLK // HALO) - 1, 0), 0)),
            const(state_conv.shape),
            const((CONV_WIDTH, D_CONV)),
            const((1, D_CONV)), const((1, D_CONV)), const((1, D_CONV)),
        ],
        out_specs=pl.BlockSpec((BLK, D_CONV), lambda i: (i, 0)),
        out_shape=jax.ShapeDtypeStruct((n, D_CONV), BF16),
        scratch_shapes=[pltpu.VMEM((HALO + BLK, D_CONV), F32), pltpu.VMEM((BLK, D_CONV), F32),
                        pltpu.VMEM((8, HALO + BLK, 128), F32)],
        compiler_params=_cparams(("arbitrary",)),
        name="conv",
    )(u_all, u_all, state_conv, conv_w, row(conv_b), row(ln_g), row(ln_b))


def _suffix_matrix(n, extra):
    r = lax.broadcasted_iota(jnp.int32, (n, n + extra), 0)
    c = lax.broadcasted_iota(jnp.int32, (n, n + extra), 1)
    return jnp.where((r > c) | (c >= n), 1.0, 0.0).astype(BF16)


def _split_dot(x, m):
    hi = x.astype(BF16)
    lo = (x - hi.astype(F32)).astype(BF16)
    return (jnp.dot(hi, m, preferred_element_type=F32) + jnp.dot(lo, m, preferred_element_type=F32))


def _log_keep(z):
    return -(jnp.maximum(z, 0.0) + jnp.log(1.0 + jnp.exp(-jnp.abs(z))))


def _attn_prompt_body(q_ref, k_ref, v_ref, g_ref, o_ref, m_s):
    hg = pl.program_id(1)
    qi = pl.program_id(2)

    @pl.when((pl.program_id(0) == 0) & (hg == 0) & (qi == 0))
    def _():
        m_s[...] = _suffix_matrix(BLK, 128)

    m = m_s[...]

    def tile(hh, kb, carry, diagonal):
        lanes = slice(hh * HEAD_DIM, (hh + 1) * HEAD_DIM)
        rows = pl.ds(pl.multiple_of(kb * BLK, BLK), BLK)
        z = lax.dot_general(q_ref[:, lanes], k_ref[rows, lanes], (((1,), (1,)), ((), ())),
                            preferred_element_type=F32)
        lk = _log_keep(z)
        if diagonal:
            keep = (lax.broadcasted_iota(jnp.int32, (BLK, BLK), 1)
                    < lax.broadcasted_iota(jnp.int32, (BLK, BLK), 0))
            lk = jnp.where(keep, lk, 0.0)
        cs = _split_dot(lk, m)
        after = cs[:, :BLK]
        if carry is not None:
            after = after + jnp.concatenate([carry, carry], axis=1)
        w = jnp.exp(z + lk + after)
        if diagonal:
            w = jnp.where(keep, w, 0.0)
        pv = jnp.dot(w.astype(BF16), v_ref[rows, lanes], preferred_element_type=F32)
        return pv, cs[:, BLK:]

    first = [tile(hh, qi, None, True) for hh in range(HPS)]

    def cond(st):
        kb, _, carries = st
        live = jnp.max(carries[0])
        for c in carries[1:]:
            live = jnp.maximum(live, jnp.max(c))
        return (kb >= 0) & (live > LOG_ZERO)

    def step(st):
        kb, accs, carries = st
        out = [tile(hh, kb, carries[hh], False) for hh in range(HPS)]
        return (kb - 1, tuple(a + o[0] for a, o in zip(accs, out)),
                tuple(c + o[1] for c, o in zip(carries, out)))

    _, accs, _ = lax.while_loop(cond, step, (qi - 1, tuple(f[0] for f in first), tuple(f[1] for f in first)))
    for hh in range(HPS):
        acc = accs[hh]
        ms = jnp.mean(acc * acc, axis=-1, keepdims=True)
        o_ref[:, hh * HEAD_DIM:(hh + 1) * HEAD_DIM] = (
            acc * lax.rsqrt(ms + EPS) * g_ref[pl.ds(hg * HPS + hh, 1), :]).astype(BF16)


def _attn_prompt(q_all, kvb, g_heads, n_batch, t_pad):
    nq = t_pad // BLK
    wide = HPS * HEAD_DIM
    return pl.pallas_call(
        _attn_prompt_body,
        grid=(n_batch, N_HEADS // HPS, nq),
        in_specs=[
            pl.BlockSpec((BLK, wide), lambda b, h, i: (b * nq + i, h)),
            pl.BlockSpec((None, t_pad, wide), lambda b, h, i: (0, b, h)),
            pl.BlockSpec((None, t_pad, wide), lambda b, h, i: (1, b, h)),
            pl.BlockSpec((N_HEADS, HEAD_DIM), lambda b, h, i: (0, 0)),
        ],
        out_specs=pl.BlockSpec((BLK, wide), lambda b, h, i: (b * nq + i, h)),
        out_shape=jax.ShapeDtypeStruct((q_all.shape[0], D_ATTN), BF16),
        scratch_shapes=[pltpu.VMEM((BLK, BLK + 128), BF16)],
        compiler_params=_cparams(("arbitrary", "arbitrary", "arbitrary")),
        name="attn_prompt",
    )(q_all, kvb, kvb, g_heads)


def _attn_sample_body(n_kb, kblk, q_ref, kn_ref, vn_ref, kc_hbm, vc_hbm, g_ref, o_in_hbm, o_ref,
                      kbuf, vbuf, qbd_s, acc_s, carry_s, m_s, mn_s, ksem, vsem):
    s = pl.program_id(0)
    n_streams = pl.num_programs(0)
    dq = q_ref.shape[0]
    last = n_kb - 1

    def cache_copies(stream, jb, slot):
        rows = pl.ds(pl.multiple_of(jb * kblk, kblk), kblk)
        return (pltpu.make_async_copy(kc_hbm.at[0, stream, rows], kbuf.at[slot], ksem.at[slot]),
                pltpu.make_async_copy(vc_hbm.at[0, stream, rows], vbuf.at[slot], vsem.at[slot]))

    def start(stream, jb, slot):
        for cp in cache_copies(stream, jb, slot):
            cp.start()

    def wait(stream, jb, slot):
        for cp in cache_copies(stream, jb, slot):
            cp.wait()

    def cat_heads(buf, slot):
        return jnp.concatenate([buf[slot, :, hh, :] for hh in range(N_HEADS)], axis=1).astype(BF16)

    def scores(kcat):
        return jnp.dot(kcat, qbd_s[...], preferred_element_type=F32)

    def add_values(w, vcat):
        full = lax.dot_general(w.astype(BF16), vcat, (((0,), (0,)), ((), ())), preferred_element_type=F32)
        for hh in range(N_HEADS):
            acc_s[hh] += full[hh * dq:(hh + 1) * dq, hh * HEAD_DIM:(hh + 1) * HEAD_DIM]

    @pl.when(s == 0)
    def _():
        start(0, last, last % 2)
        r = lax.broadcasted_iota(jnp.int32, (kblk, kblk), 0)
        c = lax.broadcasted_iota(jnp.int32, (kblk, kblk), 1)
        m_s[...] = jnp.where(c > r, 1.0, 0.0).astype(BF16)
        mn_s[...] = m_s[0:dq, 0:dq]

    qrep = jnp.concatenate([q_ref[...].astype(F32)] * N_HEADS, axis=0)
    qt = qrep.T
    row_head = lax.broadcasted_iota(jnp.int32, (D_ATTN, 128), 0) // HEAD_DIM
    col_head = lax.broadcasted_iota(jnp.int32, (D_ATTN, 128), 1) // dq
    qbd_s[...] = jnp.where(row_head == col_head, qt, 0.0).astype(BF16)
    acc_s[...] = jnp.zeros_like(acc_s)
    z = scores(kn_ref[...])
    lk = _log_keep(z)
    key = lax.broadcasted_iota(jnp.int32, (dq, 128), 0)
    qry = lax.broadcasted_iota(jnp.int32, (dq, 128), 1) % dq
    keep = key < qry
    lk = jnp.where(keep, lk, 0.0)
    after = _split_dot_left(mn_s[...], lk)
    w = jnp.where(keep, jnp.exp(z + lk + after), 0.0)
    add_values(w, vn_ref[...])
    carry_s[...] = jnp.sum(lk, axis=0, keepdims=True)

    def cond(jb):
        return (jb >= 0) & (jnp.max(carry_s[...]) > LOG_ZERO)

    def step(jb):
        slot = jb % 2
        wait(s, jb, slot)

        @pl.when(jb > 0)
        def _():
            start(s, jb - 1, 1 - slot)

        z = scores(cat_heads(kbuf, slot))
        lk = _log_keep(z)
        after = _split_dot_left(m_s[...], lk) + carry_s[...]
        w = jnp.exp(z + lk + after)
        add_values(w, cat_heads(vbuf, slot))
        carry_s[...] += jnp.sum(lk, axis=0, keepdims=True)
        return jb - 1

    jb_end = lax.while_loop(cond, step, last)

    @pl.when(jb_end >= 0)
    def _():
        wait(s, jb_end, jb_end % 2)

    @pl.when(s + 1 < n_streams)
    def _():
        start(s + 1, last, last % 2)

    for hh in range(N_HEADS):
        a = acc_s[hh]
        ms = jnp.mean(a * a, axis=-1, keepdims=True)
        o_ref[:, hh * HEAD_DIM:(hh + 1) * HEAD_DIM] = (
            a * lax.rsqrt(ms + EPS) * g_ref[pl.ds(hh, 1), :]).astype(BF16)


def _split_dot_left(m, x):
    hi = x.astype(BF16)
    lo = (x - hi.astype(F32)).astype(BF16)
    return (jnp.dot(m, hi, preferred_element_type=F32) + jnp.dot(m, lo, preferred_element_type=F32))


def _attn_sample(q_all, kvb, cache_k, cache_v, g_heads, o_all, row0, dec_seq, kblk):
    n_streams = cache_k.shape[1]
    past = cache_k.shape[2]
    n_kb = past // kblk
    assert n_kb * kblk == past and row0 % dec_seq == 0
    row_spec = pl.BlockSpec((dec_seq, D_ATTN), lambda s: (row0 // dec_seq + s, 0))
    kv_spec = lambda which: pl.BlockSpec((None, dec_seq, D_ATTN), lambda s: (which, row0 // dec_seq + s, 0))
    cache_buf = pltpu.VMEM((2, kblk, N_HEADS, HEAD_DIM), cache_k.dtype)
    any_spec = pl.BlockSpec(memory_space=pl.ANY)
    return pl.pallas_call(
        functools.partial(_attn_sample_body, n_kb, kblk),
        grid=(n_streams,),
        in_specs=[row_spec, kv_spec(0), kv_spec(1), any_spec, any_spec,
                  pl.BlockSpec((N_HEADS, HEAD_DIM), lambda s: (0, 0)), any_spec],
        out_specs=row_spec,
        out_shape=jax.ShapeDtypeStruct(o_all.shape, o_all.dtype),
        input_output_aliases={6: 0},
        scratch_shapes=[
            cache_buf, cache_buf,
            pltpu.VMEM((D_ATTN, 128), BF16),
            pltpu.VMEM((N_HEADS, dec_seq, HEAD_DIM), F32),
            pltpu.VMEM((1, 128), F32),
            pltpu.VMEM((kblk, kblk), BF16),
            pltpu.VMEM((dec_seq, dec_seq), BF16),
            pltpu.SemaphoreType.DMA((2,)), pltpu.SemaphoreType.DMA((2,)),
        ],
        compiler_params=_cparams(("arbitrary",)),
        name="attn_sample",
    )(q_all, kvb, kvb, cache_k, cache_v, g_heads, o_all)


def _outproj_body(x_ref, c_ref, o_ref, w_ref, g_ref, wr_ref, br_ref, h_ref, xn_ref, lg_ref):
    h = (x_ref[...]
         + jnp.dot(c_ref[...], w_ref[0:D_CONV, :], preferred_element_type=F32)
         + jnp.dot(o_ref[...], w_ref[D_CONV:, :], preferred_element_type=F32))
    h_ref[...] = h
    ms = jnp.mean(h * h, axis=-1, keepdims=True)
    xn = (h * lax.rsqrt(ms + EPS) * g_ref[...]).astype(BF16)
    bits = lax.bitcast_convert_type(xn.astype(F32), jnp.uint32)
    xn_ref[...] = (bits[:, :D_MODEL // 2] >> 16) | bits[:, D_MODEL // 2:]
    lg_ref[...] = lax.dot_general(wr_ref[...], xn, (((1,), (1,)), ((), ())),
                                  preferred_element_type=F32) + br_ref[...]


def _outproj(x_all, c_all, o_all, w_out_bf, g_ffn, w_router_t_bf, b_router):
    n = x_all.shape[0]
    const = lambda shape: pl.BlockSpec(shape, lambda i: (0,) * len(shape))
    return pl.pallas_call(
        _outproj_body,
        grid=(n // ROW_E,),
        in_specs=[
            pl.BlockSpec((ROW_E, D_MODEL), lambda i: (i, 0)),
            pl.BlockSpec((ROW_E, D_CONV), lambda i: (i, 0)),
            pl.BlockSpec((ROW_E, D_ATTN), lambda i: (i, 0)),
            const((D_MODEL, D_MODEL)),
            const((1, D_MODEL)),
            const((N_EXPERTS, D_MODEL)),
            const((N_EXPERTS, 1)),
        ],
        out_specs=[
            pl.BlockSpec((ROW_E, D_MODEL), lambda i: (i, 0)),
            pl.BlockSpec((ROW_E, D_MODEL // 2), lambda i: (i, 0)),
            pl.BlockSpec((N_EXPERTS, ROW_E), lambda i: (0, i)),
        ],
        out_shape=[jax.ShapeDtypeStruct((n, D_MODEL), F32), jax.ShapeDtypeStruct((n, D_MODEL // 2), jnp.uint32),
                   jax.ShapeDtypeStruct((N_EXPERTS, n), F32)],
        compiler_params=_cparams(("arbitrary",)),
        name="outproj",
    )(x_all, c_all, o_all, w_out_bf, g_ffn.reshape(1, D_MODEL), w_router_t_bf,
      b_router.reshape(N_EXPERTS, 1))


def _route_body(t_p, t_pad, n_p, cap, lg_ref, d_ref, gt_ref, cnt_ref, tri_s, run_s):
    i = pl.program_id(0)
    tb = lg_ref.shape[1]

    @pl.when(i == 0)
    def _():
        r = lax.broadcasted_iota(jnp.int32, (tb, tb + 128), 0)
        c = lax.broadcasted_iota(jnp.int32, (tb, tb + 128), 1)
        tri_s[...] = jnp.where((r < c) | (c >= tb), 1.0, 0.0).astype(BF16)
        run_s[...] = jnp.zeros_like(run_s)

    tok = i * tb + lax.broadcasted_iota(jnp.int32, (1, tb), 1)
    valid = ((tok % t_pad) < t_p) | (tok >= n_p)
    eid = lax.broadcasted_iota(jnp.int32, (N_EXPERTS, tb), 0)
    lg = lg_ref[...]
    sel = jnp.zeros((N_EXPERTS, tb), F32)
    hot, top = [], []
    for _ in range(TOP_K):
        mx = jnp.max(lg, axis=0, keepdims=True)
        idx = jnp.min(jnp.where(lg == mx, eid, N_EXPERTS), axis=0, keepdims=True)
        one = eid == idx
        lg = jnp.where(one, -jnp.inf, lg)
        hot.append(one)
        top.append(mx)
        sel = sel + jnp.where(one & valid, 1.0, 0.0)
    ex = [jnp.exp(t - top[0]) for t in top]
    den = ex[0] + ex[1] + ex[2] + ex[3]
    cs = jnp.dot(sel.astype(BF16), tri_s[...], preferred_element_type=F32)
    run = run_s[...]
    slot = (cs[:, :tb] + jnp.concatenate([run] * (tb // 128), axis=1)
            + (eid * cap).astype(F32))
    for k in range(TOP_K):
        mine = jnp.sum(jnp.where(hot[k], slot, 0.0), axis=0, keepdims=True).astype(jnp.int32)
        trash = N_EXPERTS * cap + (tok % TB) * TOP_K + k
        d_ref[pl.ds(k, 1), :] = jnp.where(valid, mine, trash)
        gt_ref[pl.ds(k, 1), :] = ex[k] / den
    run = run + cs[:, tb:]
    run_s[...] = run
    cnt_ref[...] = run.astype(jnp.int32)


def _route(logits_t, t_p, t_pad, n_p, cap):
    n = logits_t.shape[1]
    blk = lambda: pl.BlockSpec((TOP_K, ROW_R), lambda i: (0, i))
    return pl.pallas_call(
        functools.partial(_route_body, t_p, t_pad, n_p, cap),
        grid=(n // ROW_R,),
        in_specs=[pl.BlockSpec((N_EXPERTS, ROW_R), lambda i: (0, i))],
        out_specs=[blk(), blk(), pl.BlockSpec((N_EXPERTS, 128), lambda i: (0, 0))],
        out_shape=[jax.ShapeDtypeStruct((TOP_K, n), jnp.int32),
                   jax.ShapeDtypeStruct((TOP_K, n), F32), jax.ShapeDtypeStruct((N_EXPERTS, 128), jnp.int32)],
        scratch_shapes=[pltpu.VMEM((ROW_R, ROW_R + 128), BF16), pltpu.VMEM((N_EXPERTS, 128), F32)],
        compiler_params=_cparams(("arbitrary",)),
        name="route",
    )(logits_t)


def _slot_table(dest, n_blocks):
    return dest.T.reshape(n_blocks, TB // 8, 8 * TOP_K)


def _slot_of(dest_ref, i, u, k):
    return dest_ref[0, i, u * TOP_K + k]


SLOT_BLOCK = (1, TB // 8, 8 * TOP_K)


def _dispatch_body(dest_ref, xp_ref, xs_hbm, sem):
    def row_copy(i, u, slot):
        return pltpu.make_async_copy(xp_ref.at[i, pl.ds(u, 1), :], xs_hbm.at[pl.ds(slot, 1), :], sem)

    def issue(i, c):
        for u in range(8):
            for k in range(TOP_K):
                row_copy(i, u, _slot_of(dest_ref, i, u, k)).start()
        return c

    def drain(i, c):
        for _ in range(8 * TOP_K):
            row_copy(0, 0, 0).wait()
        return c

    lax.fori_loop(0, TB // 8, issue, 0)
    lax.fori_loop(0, TB // 8, drain, 0)


def _dispatch(dest_tab, xn_packed, n_slots):
    n, width = xn_packed.shape
    return pl.pallas_call(
        _dispatch_body,
        grid=(n // TB,),
        in_specs=[
            pl.BlockSpec(SLOT_BLOCK, lambda i: (i, 0, 0), memory_space=pltpu.SMEM),
            pl.BlockSpec((TB // 8, 8, width), lambda i: (i, 0, 0)),
        ],
        out_specs=pl.BlockSpec(memory_space=pl.ANY),
        out_shape=jax.ShapeDtypeStruct((n_slots, width), jnp.uint32),
        scratch_shapes=[pltpu.SemaphoreType.DMA(())],
        compiler_params=_cparams(("arbitrary",)),
        name="dispatch",
    )(dest_tab, xn_packed.reshape(n // 8, 8, width))


def _moe_body(ie_ref, r0_ref, nr_ref, nlive_ref, xs_hbm, wg_ref, wl_ref, bg_ref, bl_ref, wd_ref, bd_ref,
              ys_hbm, xraw, xb16, yacc, pend_s, xsem, ysem):
    it = pl.program_id(0)
    j = pl.program_id(1)
    n_live = nlive_ref[0]
    n_f = pl.num_programs(1)

    def chunks(item):
        return (nr_ref[item] + CH - 1) // CH

    def x_copy(item, c):
        src = pl.multiple_of(r0_ref[item] + c * CH, CH)
        return pltpu.make_async_copy(xs_hbm.at[pl.ds(src, CH), :],
                                     xraw.at[pl.ds(pl.multiple_of(c * CH, CH), CH), :], xsem)

    def y_copy(item, c):
        dst = pl.multiple_of(r0_ref[item] + c * CH, CH)
        return pltpu.make_async_copy(yacc.at[pl.ds(pl.multiple_of(c * CH, CH), CH), :],
                                     ys_hbm.at[pl.ds(dst, CH), :], ysem)

    def for_chunks(n, fn):
        def body(c, carry):
            fn(c)
            return carry
        lax.fori_loop(0, n, body, 0)

    @pl.when((it == 0) & (j == 0))
    def _():
        pend_s[0] = 0
        for_chunks(chunks(0), lambda c: x_copy(0, c).start())

    @pl.when((it < n_live) & (j == 0))
    def _():
        nrows = nr_ref[it]
        for_chunks(chunks(it), lambda c: x_copy(it, c).wait())

        def unpack(c):
            rows = pl.ds(pl.multiple_of(c * CH, CH), CH)
            p = xraw[rows, :]
            rid = c * CH + lax.broadcasted_iota(jnp.int32, (CH, 1), 0)
            live = rid < nrows
            lo = lax.bitcast_convert_type(p << 16, F32)
            hi = lax.bitcast_convert_type(p & jnp.uint32(0xFFFF0000), F32)
            xb16[rows, 0:D_MODEL // 2] = jnp.where(live, lo, 0.0).astype(BF16)
            xb16[rows, D_MODEL // 2:] = jnp.where(live, hi, 0.0).astype(BF16)

        for_chunks(chunks(it), unpack)

        @pl.when(it + 1 < n_live)
        def _():
            for_chunks(chunks(it + 1), lambda c: x_copy(it + 1, c).start())

    @pl.when(it < n_live)
    def _():
        @pl.when(j == 0)
        def _():
            for_chunks(pend_s[0], lambda c: y_copy(it, 0).wait())
            pend_s[0] = 0

        def ffn_rows(r0, n):
            rows = pl.ds(r0, n)
            x = xb16[rows, :]
            hg = jnp.dot(x, wg_ref[0].astype(BF16), preferred_element_type=F32) + bg_ref[0]
            hl = jnp.dot(x, wl_ref[0].astype(BF16), preferred_element_type=F32) + bl_ref[0]
            hg = jnp.minimum(hg, SWIGLU_LIMIT)
            hl = jnp.clip(hl, -SWIGLU_LIMIT, SWIGLU_LIMIT)
            a = hg * jax.nn.sigmoid(SWIGLU_ALPHA * hg) * (hl + 1.0)
            part = jnp.dot(a.astype(BF16), wd_ref[0].astype(BF16), preferred_element_type=F32)
            yacc[rows, :] = jnp.where(j == 0, bd_ref[0], yacc[rows, :]) + part

        n_ch = chunks(it)
        n_full = n_ch // (SUB // CH)
        tail = n_ch - n_full * (SUB // CH)

        def full(s, carry):
            ffn_rows(pl.multiple_of(s * SUB, CH), SUB)
            return carry

        lax.fori_loop(0, n_full, full, 0)
        for t in range(1, SUB // CH):
            @pl.when(tail == t)
            def _():
                ffn_rows(pl.multiple_of(n_full * SUB, CH), t * CH)

        @pl.when(j == n_f - 1)
        def _():
            for_chunks(n_ch, lambda c: y_copy(it, c).start())
            pend_s[0] = n_ch

            @pl.when(it == n_live - 1)
            def _():
                for_chunks(n_ch, lambda c: y_copy(it, c).wait())
                pend_s[0] = 0


def _moe(xs, item_expert, item_row0, item_rows, n_live, w_up, b_up, w_down, b_down):
    max_items = item_expert.shape[0]
    n_f = D_FF // TF

    def tile(it, j, nl):
        return jnp.where(it < nl[0], j, n_f - 1)

    return pl.pallas_call(
        _moe_body,
        grid_spec=pltpu.PrefetchScalarGridSpec(
            num_scalar_prefetch=4,
            grid=(n_live[0], n_f),
            in_specs=[
                pl.BlockSpec(memory_space=pl.ANY),
                pl.BlockSpec((1, D_MODEL, TF), lambda it, j, ie, r0, nr, nl: (ie[it], 0, tile(it, j, nl))),
                pl.BlockSpec((1, D_MODEL, TF), lambda it, j, ie, r0, nr, nl: (ie[it], 0, n_f + tile(it, j, nl))),
                pl.BlockSpec((1, 1, TF), lambda it, j, ie, r0, nr, nl: (ie[it], 0, tile(it, j, nl))),
                pl.BlockSpec((1, 1, TF), lambda it, j, ie, r0, nr, nl: (ie[it], 0, n_f + tile(it, j, nl))),
                pl.BlockSpec((1, TF, D_MODEL), lambda it, j, ie, r0, nr, nl: (ie[it], tile(it, j, nl), 0)),
                pl.BlockSpec((1, 1, D_MODEL), lambda it, j, ie, r0, nr, nl: (ie[it], 0, 0)),
            ],
            out_specs=pl.BlockSpec(memory_space=pl.ANY),
            scratch_shapes=[
                pltpu.VMEM((TM, D_MODEL // 2), jnp.uint32),
                pltpu.VMEM((TM, D_MODEL), BF16),
                pltpu.VMEM((TM, D_MODEL), F32),
                pltpu.SMEM((1,), jnp.int32),
                pltpu.SemaphoreType.DMA(()),
                pltpu.SemaphoreType.DMA(()),
            ],
        ),
        out_shape=jax.ShapeDtypeStruct((xs.shape[0], D_MODEL), F32),
        compiler_params=_cparams(("arbitrary", "arbitrary")),
        name="moe",
    )(item_expert, item_row0, item_rows, n_live, xs, w_up, w_up,
      b_up.reshape(N_EXPERTS, 1, 2 * D_FF), b_up.reshape(N_EXPERTS, 1, 2 * D_FF), w_down,
      b_down.reshape(N_EXPERTS, 1, D_MODEL))


def _combine_body(blocks_per_batch, n_prompt_blocks, t_pad, n_p, dest_ref, ys_hbm, h_hbm, gt_ref, gf_ref,
                  yp_ref, ysm_ref, gbuf, hbuf, gsem, hsem):
    i = pl.program_id(0)
    n_blocks = pl.num_programs(0)

    def h_copy(blk, slot):
        row = jnp.where(blk < n_prompt_blocks,
                        (blk // blocks_per_batch) * t_pad + N_META + (blk % blocks_per_batch) * TB, n_p)
        return pltpu.make_async_copy(h_hbm.at[pl.ds(pl.multiple_of(row, 8), TB), :], hbuf.at[slot],
                                     hsem.at[slot])

    def row_copy(slot, k, c, u):
        return pltpu.make_async_copy(ys_hbm.at[pl.ds(slot, 1), :], gbuf.at[k, c, pl.ds(u, 1), :], gsem)

    def issue(c, carry):
        for u in range(8):
            for k in range(TOP_K):
                row_copy(_slot_of(dest_ref, c, u, k), k, c, u).start()
        return carry

    def drain(c, carry):
        for _ in range(8 * TOP_K):
            row_copy(0, 0, 0, 0).wait()
        return carry

    @pl.when(i == 0)
    def _():
        h_copy(0, 0).start()

    @pl.when(i + 1 < n_blocks)
    def _():
        h_copy(i + 1, (i + 1) % 2).start()

    lax.fori_loop(0, TB // 8, issue, 0)
    h_copy(i, i % 2).wait()
    lax.fori_loop(0, TB // 8, drain, 0)
    y = hbuf[i % 2]
    for k in range(TOP_K):
        y = y + gbuf[k].reshape(TB, D_MODEL) * gt_ref[:, k:k + 1]
    ms = jnp.mean(y * y, axis=-1, keepdims=True)
    y = y * lax.rsqrt(ms + EPS) * gf_ref[...]

    @pl.when(i < n_prompt_blocks)
    def _():
        yp_ref[...] = y

    @pl.when(i >= n_prompt_blocks)
    def _():
        ysm_ref[...] = y


def _combine(dest_blk, ys, h_all, gates_t, g_final, blocks_per_batch, n_prompt_blocks, t_pad, n_p):
    return pl.pallas_call(
        functools.partial(_combine_body, blocks_per_batch, n_prompt_blocks, t_pad, n_p),
        grid=(n_prompt_blocks + 1,),
        in_specs=[
            pl.BlockSpec(SLOT_BLOCK, lambda i: (i, 0, 0), memory_space=pltpu.SMEM),
            pl.BlockSpec(memory_space=pl.ANY),
            pl.BlockSpec(memory_space=pl.ANY),
            pl.BlockSpec((TB, TOP_K), lambda i: (i, 0)),
            pl.BlockSpec((1, D_MODEL), lambda i: (0, 0)),
        ],
        out_specs=[
            pl.BlockSpec((TB, D_MODEL), lambda i: (jnp.minimum(i, n_prompt_blocks - 1), 0)),
            pl.BlockSpec((TB, D_MODEL), lambda i: (0, 0)),
        ],
        out_shape=[jax.ShapeDtypeStruct((n_prompt_blocks * TB, D_MODEL), F32),
                   jax.ShapeDtypeStruct((TB, D_MODEL), F32)],
        scratch_shapes=[pltpu.VMEM((TOP_K, TB // 8, 8, D_MODEL), F32), pltpu.VMEM((2, TB, D_MODEL), F32),
                        pltpu.SemaphoreType.DMA(()), pltpu.SemaphoreType.DMA((2,))],
        compiler_params=_cparams(("arbitrary",)),
        name="combine",
    )(dest_blk, ys, h_all, gates_t, g_final.reshape(1, D_MODEL))


def _work_items(counts, cap, max_items):
    items_per = (counts + TM - 1) // TM
    item_end = jnp.cumsum(items_per)
    item_start = item_end - items_per
    n_live = item_end[-1]
    it = jnp.arange(max_items, dtype=jnp.int32)
    it_live = jnp.minimum(it, n_live - 1)
    expert = jnp.minimum(jnp.searchsorted(item_end, it_live, side="right"), N_EXPERTS - 1).astype(jnp.int32)
    part = it_live - item_start[expert]
    rows = jnp.clip(counts[expert] - part * TM, 0, TM)
    rows = jnp.where(it < n_live, rows, 0).astype(jnp.int32)
    row0 = (expert * cap + part * TM).astype(jnp.int32)
    return expert, row0, rows, n_live.reshape(1).astype(jnp.int32)


def kernel(x_prompt, x_sample, cache_k, cache_v, state_conv, meta_tokens, g_mix, w_in, conv_w, conv_b,
           conv_ln_g, conv_ln_b, g_attn_out, w_out, g_ffn, w_router, b_router, w_up, b_up, w_down,
           b_down, g_final):
    n_batch, seq, _ = x_prompt.shape
    n_streams, dec_seq, _ = x_sample.shape
    depth = w_in.shape[0]
    assert depth == 1 and n_streams * dec_seq == BLK
    t_p = N_META + seq
    t_pad = -(-t_p // BLK) * BLK
    n_p = n_batch * t_pad
    n_s = n_streams * dec_seq
    n_all = n_p + n_s
    assert n_all % ROW_A == 0 and n_all % ROW_E == 0 and n_all % ROW_R == 0 and n_all % TB == 0

    meta = meta_tokens.astype(F32)
    pad = jnp.zeros((t_pad - t_p, D_MODEL), F32)
    pieces = []
    for b in range(n_batch):
        pieces += [meta, x_prompt[b], pad]
    x_all = jnp.concatenate(pieces + [x_sample.reshape(n_s, D_MODEL)], axis=0)

    l = 0
    u_all, q_all, kvb, k_p, v_p, k_s, v_s = _inproj(
        x_all, g_mix[l], w_in[l].astype(BF16), n_batch, t_p, t_pad)
    c_all = _conv(u_all, state_conv[l], conv_w[l], conv_b[l], conv_ln_g[l], conv_ln_b[l],
                  t_pad // BLK, n_p // BLK)
    g_heads = g_attn_out[l].reshape(N_HEADS, HEAD_DIM)
    o_all = _attn_prompt(q_all, kvb, g_heads, n_batch, t_pad)
    o_all = _attn_sample(q_all, kvb, cache_k[l:l + 1], cache_v[l:l + 1], g_heads, o_all, n_p, dec_seq, BLK)
    h_all, xn_packed, logits_t = _outproj(x_all, c_all, o_all, w_out[l].astype(BF16), g_ffn[l],
                                          w_router[l].T.astype(BF16), b_router[l])

    n_valid = n_batch * t_p + n_s
    cap = -(-n_valid // CH) * CH
    n_slots = N_EXPERTS * cap + TB * TOP_K
    max_items = -(-(n_valid * TOP_K) // TM) + N_EXPERTS
    dest, gates, counts = _route(logits_t, t_p, t_pad, n_p, cap)
    xs = _dispatch(_slot_table(dest, n_all // TB), xn_packed, n_slots)
    item_expert, item_row0, item_rows, n_live = _work_items(counts[:, 0], cap, max_items)
    ys = _moe(xs, item_expert, item_row0, item_rows, n_live, w_up[l], b_up[l], w_down[l], b_down[l])

    def out_tokens(a):
        a_p = a[:, :n_p].reshape(-1, n_batch, t_pad)[:, :, N_META:t_p].reshape(-1, n_batch * seq)
        return jnp.concatenate([a_p, a[:, n_p:]], axis=1)

    assert seq % TB == 0 and n_s == TB
    n_out_blocks = n_batch * seq // TB
    dest_out = _slot_table(out_tokens(dest), n_out_blocks + 1)
    y_p, y_s = _combine(dest_out, ys, h_all, out_tokens(gates).T, g_final, seq // TB, n_out_blocks, t_pad, n_p)

    def prompt_rows(a, width):
        return a[:n_p].reshape(n_batch, t_pad, width)[:, :t_p]

    y_prompt = y_p.reshape(n_batch, seq, D_MODEL)
    y_sample = y_s.reshape(n_streams, dec_seq, D_MODEL)
    k_prompt = k_p[None]
    v_prompt = v_p[None]
    conv_prompt = prompt_rows(u_all, D_CONV)[:, t_p - (CONV_WIDTH - 1):][None]
    k_sample = k_s.reshape(1, n_streams, dec_seq, N_HEADS, HEAD_DIM)
    v_sample = v_s.reshape(1, n_streams, dec_seq, N_HEADS, HEAD_DIM)
    u_s = u_all[n_p:].reshape(n_streams, dec_seq, D_CONV)
    conv_sample = jnp.concatenate([state_conv[l], u_s], axis=1)[:, -(CONV_WIDTH - 1):][None]
    return (y_prompt, y_sample, k_prompt, v_prompt, conv_prompt, k_sample, v_sample, conv_sample)
```

```python
import functools
import math

import jax
import jax.numpy as jnp
from jax import lax
from jax.experimental import pallas as pl
from jax.experimental.pallas import tpu as pltpu

F32 = jnp.float32
BF16 = jnp.bfloat16

D_MODEL = 2048
N_META = 16
D_CONV = 1024
N_HEADS = 8
HEAD_DIM = 128
D_ATTN = N_HEADS * HEAD_DIM
D_IN = 2 * D_CONV + 3 * D_ATTN
CONV_WIDTH = 31
N_EXPERTS = 32
TOP_K = 4
D_FF = D_MODEL
SWIGLU_ALPHA = 1.702
SWIGLU_LIMIT = 7.0
EPS = 1e-5

BLK = 256
HPS = 8
HALO = 32
COL = 512
ROW_A = 640
ROW_E = 640
ROW_R = 640
TM = 2048
CH = 128
SUB = 6 * CH
TF = 256
TB = 256
LOG_ZERO = -104.0
VMEM_LIMIT = 56 * 1024 * 1024


def _cparams(sem, vmem=VMEM_LIMIT):
    return pltpu.CompilerParams(dimension_semantics=sem, vmem_limit_bytes=vmem)


def _inproj_body(t_p, t_pad, n_p, x_ref, g_ref, wa_ref, wg_ref, wq_ref, w1_ref, w2_ref,
                 u_ref, q_ref, kvb_ref, kp_hbm, vp_hbm, ks_hbm, vs_hbm, xn_s, kv4_s, ksem, vsem):
    i = pl.program_id(0)
    j = pl.program_id(1)
    heads_per_tile = COL // HEAD_DIM
    k4_s, v4_s = kv4_s.at[0], kv4_s.at[1]

    def chunk_copies(tile, src, dst_p, dst_s, sem):
        out = []
        per_batch, full, rem = t_pad // CH, t_p // CH, t_p % CH
        for c in range(ROW_A // CH):
            g = tile * (ROW_A // CH) + c
            is_p = g < n_p // CH
            b, wi = g // per_batch, g % per_batch
            out.append((is_p & (wi < full), pltpu.make_async_copy(
                src.at[pl.ds(c * CH, CH)], dst_p.at[b, pl.ds(pl.multiple_of(wi * CH, CH), CH)], sem)))
            if rem:
                out.append((is_p & (wi == full), pltpu.make_async_copy(
                    src.at[pl.ds(c * CH, rem)], dst_p.at[b, pl.ds(full * CH, rem)], sem)))
            out.append((jnp.logical_not(is_p), pltpu.make_async_copy(
                src.at[pl.ds(c * CH, CH)],
                dst_s.at[pl.ds(pl.multiple_of((g - n_p // CH) * CH, CH), CH)], sem)))
        return out

    def start_all(copies):
        for cond, cp in copies:
            @pl.when(cond)
            def _():
                cp.start()

    def wait_all(copies):
        for cond, cp in copies:
            @pl.when(cond)
            def _():
                cp.wait()

    def store_heads(dst, p, first_head):
        for hh in range(heads_per_tile):
            dst[:, first_head + hh, :] = p[:, hh * HEAD_DIM:(hh + 1) * HEAD_DIM]

    @pl.when(j == 0)
    def _():
        x = x_ref[...]
        ms = jnp.mean(x * x, axis=-1, keepdims=True)
        xn_s[...] = (x * lax.rsqrt(ms + EPS) * g_ref[...]).astype(BF16)

        @pl.when(i > 0)
        def _():
            wait_all(chunk_copies(i - 1, k4_s, kp_hbm, ks_hbm, ksem))

    @pl.when((j == 1) & (i > 0))
    def _():
        wait_all(chunk_copies(i - 1, v4_s, vp_hbm, vs_hbm, vsem))

    xn = xn_s[...]
    proj = lambda w_ref: jnp.dot(xn, w_ref[...], preferred_element_type=F32)
    u_ref[...] = proj(wa_ref) * jax.nn.sigmoid(proj(wg_ref))
    q_ref[...] = (proj(wq_ref) * (1.0 / math.sqrt(HEAD_DIM))).astype(BF16)
    for t, w_ref in enumerate((w1_ref, w2_ref)):
        p = proj(w_ref)
        kvb_ref[0, :, t * COL:(t + 1) * COL] = p.astype(BF16)
        store_heads(kv4_s.at[j], p, t * heads_per_tile)

    @pl.when(j == 0)
    def _():
        start_all(chunk_copies(i, k4_s, kp_hbm, ks_hbm, ksem))

    @pl.when(j == 1)
    def _():
        start_all(chunk_copies(i, v4_s, vp_hbm, vs_hbm, vsem))

        @pl.when(i == pl.num_programs(0) - 1)
        def _():
            wait_all(chunk_copies(i, k4_s, kp_hbm, ks_hbm, ksem))
            wait_all(chunk_copies(i, v4_s, vp_hbm, vs_hbm, vsem))


def _inproj(x_all, g_mix, w_in_bf, n_batch, t_p, t_pad):
    n = x_all.shape[0]
    n_p = n_batch * t_pad
    assert D_IN == 10 * COL and COL == 4 * HEAD_DIM and ROW_A % CH == 0 and t_pad % CH == 0

    def w_spec(first, stride):
        return pl.BlockSpec((D_MODEL, COL), lambda i, j: (0, first + stride * j))

    wide = lambda dt: jax.ShapeDtypeStruct((n, D_CONV), dt)
    cache_p = jax.ShapeDtypeStruct((n_batch, t_p, N_HEADS, HEAD_DIM), F32)
    cache_s = jax.ShapeDtypeStruct((n - n_p, N_HEADS, HEAD_DIM), F32)
    any_spec = pl.BlockSpec(memory_space=pl.ANY)
    half_spec = pl.BlockSpec((ROW_A, COL), lambda i, j: (i, j))
    return pl.pallas_call(
        functools.partial(_inproj_body, t_p, t_pad, n_p),
        grid=(n // ROW_A, 2),
        in_specs=[
            pl.BlockSpec((ROW_A, D_MODEL), lambda i, j: (i, 0)),
            pl.BlockSpec((1, D_MODEL), lambda i, j: (0, 0)),
            w_spec(0, 1), w_spec(2, 1), w_spec(4, 1), w_spec(6, 2), w_spec(7, 2),
        ],
        out_specs=[half_spec, half_spec, pl.BlockSpec((1, ROW_A, D_ATTN), lambda i, j: (j, i, 0)),
                   any_spec, any_spec, any_spec, any_spec],
        out_shape=[wide(F32), wide(BF16), jax.ShapeDtypeStruct((2, n, D_ATTN), BF16),
                   cache_p, cache_p, cache_s, cache_s],
        scratch_shapes=[pltpu.VMEM((ROW_A, D_MODEL), BF16),
                        pltpu.VMEM((2, ROW_A, N_HEADS, HEAD_DIM), F32),
                        pltpu.SemaphoreType.DMA(()), pltpu.SemaphoreType.DMA(())],
        compiler_params=_cparams(("arbitrary", "arbitrary")),
        name="inproj",
    )(x_all, g_mix.reshape(1, D_MODEL), w_in_bf, w_in_bf, w_in_bf, w_in_bf, w_in_bf)


def _conv_taps(ext_ref, w_ref, b_ref, dst_ref, src_row0, dst_row0, nrows):
    def chunk(cc, carry):
        lanes = pl.ds(pl.multiple_of(cc * 128, 128), 128)
        acc = jnp.broadcast_to(b_ref[:, lanes], (nrows, 128))
        for j in range(CONV_WIDTH):
            acc = acc + ext_ref[pl.ds(src_row0 + j, nrows), lanes] * w_ref[pl.ds(j, 1), lanes]
        dst_ref[pl.ds(dst_row0, nrows), lanes] = acc
        return carry

    lax.fori_loop(0, D_CONV // 128, chunk, 0)


def _conv_block(ext_ref, sh_ref, w_ref, b_ref, dst_ref, off):
    span = HALO + BLK

    def chunk(cc, carry):
        lanes = pl.ds(pl.multiple_of(cc * 128, 128), 128)
        for r in range(8):
            n = span - (8 if r else 0)
            sh_ref[r, 0:n, :] = ext_ref[pl.ds(r, n), lanes]
        for rc in range(BLK // 64):
            acc = jnp.broadcast_to(b_ref[:, lanes], (64, 128))
            for j in range(CONV_WIDTH):
                r = (off + j) % 8
                acc = acc + sh_ref[r, pl.ds(off + j - r + rc * 64, 64), :] * w_ref[pl.ds(j, 1), lanes]
            dst_ref[pl.ds(rc * 64, 64), lanes] = acc
        return carry

    lax.fori_loop(0, D_CONV // 128, chunk, 0)


def _conv_body(blocks_per_batch, n_prompt_blocks, u_ref, prev_ref, state_ref, w_ref, b_ref,
               lg_ref, lb_ref, c_ref, ext_s, conv_s, sh_s):
    i = pl.program_id(0)
    off = HALO - (CONV_WIDTH - 1)

    @pl.when(i < n_prompt_blocks)
    def _():
        first = (i % blocks_per_batch) == 0
        ext_s[0:HALO, :] = jnp.where(first, 0.0, prev_ref[...])
        ext_s[HALO:HALO + BLK, :] = u_ref[...]
        _conv_block(ext_s, sh_s, w_ref, b_ref, conv_s, off)

    @pl.when(i >= n_prompt_blocks)
    def _():
        def stream(s, carry):
            r0 = pl.multiple_of(s * 16, 16)
            ext_s[off:HALO, :] = state_ref[s]
            ext_s[HALO:HALO + 16, :] = u_ref[pl.ds(r0, 16), :]
            _conv_taps(ext_s, w_ref, b_ref, conv_s, off, r0, 16)
            return carry

        lax.fori_loop(0, BLK // 16, stream, 0)

    x = conv_s[...]
    mu = jnp.mean(x, axis=-1, keepdims=True)
    xc = x - mu
    var = jnp.mean(xc * xc, axis=-1, keepdims=True)
    y = xc * lax.rsqrt(var + EPS) * lg_ref[...] + lb_ref[...]
    c_ref[...] = (y * jax.nn.sigmoid(y)).astype(BF16)


def _conv(u_all, state_conv, conv_w, conv_b, ln_g, ln_b, blocks_per_batch, n_prompt_blocks):
    n = u_all.shape[0]
    row = lambda a: a.reshape(1, D_CONV)
    const = lambda shape: pl.BlockSpec(shape, lambda i: (0,) * len(shape))
    return pl.pallas_call(
        functools.partial(_conv_body, blocks_per_batch, n_prompt_blocks),
        grid=(n // BLK,),
        in_specs=[
            pl.BlockSpec((BLK, D_CONV), lambda i: (i, 0)),
            pl.BlockSpec((HALO, D_CONV), lambda i: (jnp.maximum(i * (BLK // HALO) - 1, 0), 0)),
            const(state_conv.shape),
            const((CONV_WIDTH, D_CONV)),
            const((1, D_CONV)), const((1, D_CONV)), const((1, D_CONV)),
        ],
        out_specs=pl.BlockSpec((BLK, D_CONV), lambda i: (i, 0)),
        out_shape=jax.ShapeDtypeStruct((n, D_CONV), BF16),
        scratch_shapes=[pltpu.VMEM((HALO + BLK, D_CONV), F32), pltpu.VMEM((BLK, D_CONV), F32),
                        pltpu.VMEM((8, HALO + BLK, 128), F32)],
        compiler_params=_cparams(("arbitrary",)),
        name="conv",
    )(u_all, u_all, state_conv, conv_w, row(conv_b), row(ln_g), row(ln_b))


def _suffix_matrix(n, extra):
    r = lax.broadcasted_iota(jnp.int32, (n, n + extra), 0)
    c = lax.broadcasted_iota(jnp.int32, (n, n + extra), 1)
    return jnp.where((r > c) | (c >= n), 1.0, 0.0).astype(BF16)


def _split_dot(x, m):
    hi = x.astype(BF16)
    lo = (x - hi.astype(F32)).astype(BF16)
    return (jnp.dot(hi, m, preferred_element_type=F32) + jnp.dot(lo, m, preferred_element_type=F32))


def _log_keep(z):
    return -(jnp.maximum(z, 0.0) + jnp.log(1.0 + jnp.exp(-jnp.abs(z))))


def _attn_prompt_body(q_ref, k_ref, v_ref, g_ref, o_ref, m_s):
    hg = pl.program_id(1)
    qi = pl.program_id(2)

    @pl.when((pl.program_id(0) == 0) & (hg == 0) & (qi == 0))
    def _():
        m_s[...] = _suffix_matrix(BLK, 128)

    m = m_s[...]

    def tile(hh, kb, carry, diagonal):
        lanes = slice(hh * HEAD_DIM, (hh + 1) * HEAD_DIM)
        rows = pl.ds(pl.multiple_of(kb * BLK, BLK), BLK)
        z = lax.dot_general(q_ref[:, lanes], k_ref[rows, lanes], (((1,), (1,)), ((), ())),
                            preferred_element_type=F32)
        lk = _log_keep(z)
        if diagonal:
            keep = (lax.broadcasted_iota(jnp.int32, (BLK, BLK), 1)
                    < lax.broadcasted_iota(jnp.int32, (BLK, BLK), 0))
            lk = jnp.where(keep, lk, 0.0)
        cs = _split_dot(lk, m)
        after = cs[:, :BLK]
        if carry is not None:
            after = after + jnp.concatenate([carry, carry], axis=1)
        w = jnp.exp(z + lk + after)
        if diagonal:
            w = jnp.where(keep, w, 0.0)
        pv = jnp.dot(w.astype(BF16), v_ref[rows, lanes], preferred_element_type=F32)
        return pv, cs[:, BLK:]

    first = [tile(hh, qi, None, True) for hh in range(HPS)]

    def cond(st):
        kb, _, carries = st
        live = jnp.max(carries[0])
        for c in carries[1:]:
            live = jnp.maximum(live, jnp.max(c))
        return (kb >= 0) & (live > LOG_ZERO)

    def step(st):
        kb, accs, carries = st
        out = [tile(hh, kb, carries[hh], False) for hh in range(HPS)]
        return (kb - 1, tuple(a + o[0] for a, o in zip(accs, out)),
                tuple(c + o[1] for c, o in zip(carries, out)))

    _, accs, _ = lax.while_loop(cond, step, (qi - 1, tuple(f[0] for f in first), tuple(f[1] for f in first)))
    for hh in range(HPS):
        acc = accs[hh]
        ms = jnp.mean(acc * acc, axis=-1, keepdims=True)
        o_ref[:, hh * HEAD_DIM:(hh + 1) * HEAD_DIM] = (
            acc * lax.rsqrt(ms + EPS) * g_ref[pl.ds(hg * HPS + hh, 1), :]).astype(BF16)


def _attn_prompt(q_all, kvb, g_heads, n_batch, t_pad):
    nq = t_pad // BLK
    wide = HPS * HEAD_DIM
    return pl.pallas_call(
        _attn_prompt_body,
        grid=(n_batch, N_HEADS // HPS, nq),
        in_specs=[
            pl.BlockSpec((BLK, wide), lambda b, h, i: (b * nq + i, h)),
            pl.BlockSpec((None, t_pad, wide), lambda b, h, i: (0, b, h)),
            pl.BlockSpec((None, t_pad, wide), lambda b, h, i: (1, b, h)),
            pl.BlockSpec((N_HEADS, HEAD_DIM), lambda b, h, i: (0, 0)),
        ],
        out_specs=pl.BlockSpec((BLK, wide), lambda b, h, i: (b * nq + i, h)),
        out_shape=jax.ShapeDtypeStruct((q_all.shape[0], D_ATTN), BF16),
        scratch_shapes=[pltpu.VMEM((BLK, BLK + 128), BF16)],
        compiler_params=_cparams(("arbitrary", "arbitrary", "arbitrary")),
        name="attn_prompt",
    )(q_all, kvb, kvb, g_heads)


def _attn_sample_body(n_kb, kblk, q_ref, kn_ref, vn_ref, kc_hbm, vc_hbm, g_ref, o_in_hbm, o_ref,
                      kbuf, vbuf, qbd_s, acc_s, carry_s, m_s, mn_s, ksem, vsem):
    s = pl.program_id(0)
    n_streams = pl.num_programs(0)
    dq = q_ref.shape[0]
    last = n_kb - 1

    def cache_copies(stream, jb, slot):
        rows = pl.ds(pl.multiple_of(jb * kblk, kblk), kblk)
        return (pltpu.make_async_copy(kc_hbm.at[0, stream, rows], kbuf.at[slot], ksem.at[slot]),
                pltpu.make_async_copy(vc_hbm.at[0, stream, rows], vbuf.at[slot], vsem.at[slot]))

    def start(stream, jb, slot):
        for cp in cache_copies(stream, jb, slot):
            cp.start()

    def wait(stream, jb, slot):
        for cp in cache_copies(stream, jb, slot):
            cp.wait()

    def cat_heads(buf, slot):
        return jnp.concatenate([buf[slot, :, hh, :] for hh in range(N_HEADS)], axis=1).astype(BF16)

    def scores(kcat):
        return jnp.dot(kcat, qbd_s[...], preferred_element_type=F32)

    def add_values(w, vcat):
        full = lax.dot_general(w.astype(BF16), vcat, (((0,), (0,)), ((), ())), preferred_element_type=F32)
        for hh in range(N_HEADS):
            acc_s[hh] += full[hh * dq:(hh + 1) * dq, hh * HEAD_DIM:(hh + 1) * HEAD_DIM]

    @pl.when(s == 0)
    def _():
        start(0, last, last % 2)
        r = lax.broadcasted_iota(jnp.int32, (kblk, kblk), 0)
        c = lax.broadcasted_iota(jnp.int32, (kblk, kblk), 1)
        m_s[...] = jnp.where(c > r, 1.0, 0.0).astype(BF16)
        mn_s[...] = m_s[0:dq, 0:dq]

    qrep = jnp.concatenate([q_ref[...].astype(F32)] * N_HEADS, axis=0)
    qt = qrep.T
    row_head = lax.broadcasted_iota(jnp.int32, (D_ATTN, 128), 0) // HEAD_DIM
    col_head = lax.broadcasted_iota(jnp.int32, (D_ATTN, 128), 1) // dq
    qbd_s[...] = jnp.where(row_head == col_head, qt, 0.0).astype(BF16)
    acc_s[...] = jnp.zeros_like(acc_s)
    z = scores(kn_ref[...])
    lk = _log_keep(z)
    key = lax.broadcasted_iota(jnp.int32, (dq, 128), 0)
    qry = lax.broadcasted_iota(jnp.int32, (dq, 128), 1) % dq
    keep = key < qry
    lk = jnp.where(keep, lk, 0.0)
    after = _split_dot_left(mn_s[...], lk)
    w = jnp.where(keep, jnp.exp(z + lk + after), 0.0)
    add_values(w, vn_ref[...])
    carry_s[...] = jnp.sum(lk, axis=0, keepdims=True)

    def cond(jb):
        return (jb >= 0) & (jnp.max(carry_s[...]) > LOG_ZERO)

    def step(jb):
        slot = jb % 2
        wait(s, jb, slot)

        @pl.when(jb > 0)
        def _():
            start(s, jb - 1, 1 - slot)

        z = scores(cat_heads(kbuf, slot))
        lk = _log_keep(z)
        after = _split_dot_left(m_s[...], lk) + carry_s[...]
        w = jnp.exp(z + lk + after)
        add_values(w, cat_heads(vbuf, slot))
        carry_s[...] += jnp.sum(lk, axis=0, keepdims=True)
        return jb - 1

    jb_end = lax.while_loop(cond, step, last)

    @pl.when(jb_end >= 0)
    def _():
        wait(s, jb_end, jb_end % 2)

    @pl.when(s + 1 < n_streams)
    def _():
        start(s + 1, last, last % 2)

    for hh in range(N_HEADS):
        a = acc_s[hh]
        ms = jnp.mean(a * a, axis=-1, keepdims=True)
        o_ref[:, hh * HEAD_DIM:(hh + 1) * HEAD_DIM] = (
            a * lax.rsqrt(ms + EPS) * g_ref[pl.ds(hh, 1), :]).astype(BF16)


def _split_dot_left(m, x):
    hi = x.astype(BF16)
    lo = (x - hi.astype(F32)).astype(BF16)
    return (jnp.dot(m, hi, preferred_element_type=F32) + jnp.dot(m, lo, preferred_element_type=F32))


def _attn_sample(q_all, kvb, cache_k, cache_v, g_heads, o_all, row0, dec_seq, kblk):
    n_streams = cache_k.shape[1]
    past = cache_k.shape[2]
    n_kb = past // kblk
    assert n_kb * kblk == past and row0 % dec_seq == 0
    row_spec = pl.BlockSpec((dec_seq, D_ATTN), lambda s: (row0 // dec_seq + s, 0))
    kv_spec = lambda which: pl.BlockSpec((None, dec_seq, D_ATTN), lambda s: (which, row0 // dec_seq + s, 0))
    cache_buf = pltpu.VMEM((2, kblk, N_HEADS, HEAD_DIM), cache_k.dtype)
    any_spec = pl.BlockSpec(memory_space=pl.ANY)
    return pl.pallas_call(
        functools.partial(_attn_sample_body, n_kb, kblk),
        grid=(n_streams,),
        in_specs=[row_spec, kv_spec(0), kv_spec(1), any_spec, any_spec,
                  pl.BlockSpec((N_HEADS, HEAD_DIM), lambda s: (0, 0)), any_spec],
        out_specs=row_spec,
        out_shape=jax.ShapeDtypeStruct(o_all.shape, o_all.dtype),
        input_output_aliases={6: 0},
        scratch_shapes=[
            cache_buf, cache_buf,
            pltpu.VMEM((D_ATTN, 128), BF16),
            pltpu.VMEM((N_HEADS, dec_seq, HEAD_DIM), F32),
            pltpu.VMEM((1, 128), F32),
            pltpu.VMEM((kblk, kblk), BF16),
            pltpu.VMEM((dec_seq, dec_seq), BF16),
            pltpu.SemaphoreType.DMA((2,)), pltpu.SemaphoreType.DMA((2,)),
        ],
        compiler_params=_cparams(("arbitrary",)),
        name="attn_sample",
    )(q_all, kvb, kvb, cache_k, cache_v, g_heads, o_all)


def _outproj_body(x_ref, c_ref, o_ref, w_ref, g_ref, wr_ref, br_ref, h_ref, xn_ref, lg_ref):
    h = (x_ref[...]
         + jnp.dot(c_ref[...], w_ref[0:D_CONV, :], preferred_element_type=F32)
         + jnp.dot(o_ref[...], w_ref[D_CONV:, :], preferred_element_type=F32))
    h_ref[...] = h
    ms = jnp.mean(h * h, axis=-1, keepdims=True)
    xn = (h * lax.rsqrt(ms + EPS) * g_ref[...]).astype(BF16)
    bits = lax.bitcast_convert_type(xn.astype(F32), jnp.uint32)
    xn_ref[...] = (bits[:, :D_MODEL // 2] >> 16) | bits[:, D_MODEL // 2:]
    lg_ref[...] = lax.dot_general(wr_ref[...], xn, (((1,), (1,)), ((), ())),
                                  preferred_element_type=F32) + br_ref[...]


def _outproj(x_all, c_all, o_all, w_out_bf, g_ffn, w_router_t_bf, b_router):
    n = x_all.shape[0]
    const = lambda shape: pl.BlockSpec(shape, lambda i: (0,) * len(shape))
    return pl.pallas_call(
        _outproj_body,
        grid=(n // ROW_E,),
        in_specs=[
            pl.BlockSpec((ROW_E, D_MODEL), lambda i: (i, 0)),
            pl.BlockSpec((ROW_E, D_CONV), lambda i: (i, 0)),
            pl.BlockSpec((ROW_E, D_ATTN), lambda i: (i, 0)),
            const((D_MODEL, D_MODEL)),
            const((1, D_MODEL)),
            const((N_EXPERTS, D_MODEL)),
            const((N_EXPERTS, 1)),
        ],
        out_specs=[
            pl.BlockSpec((ROW_E, D_MODEL), lambda i: (i, 0)),
            pl.BlockSpec((ROW_E, D_MODEL // 2), lambda i: (i, 0)),
            pl.BlockSpec((N_EXPERTS, ROW_E), lambda i: (0, i)),
        ],
        out_shape=[jax.ShapeDtypeStruct((n, D_MODEL), F32), jax.ShapeDtypeStruct((n, D_MODEL // 2), jnp.uint32),
                   jax.ShapeDtypeStruct((N_EXPERTS, n), F32)],
        compiler_params=_cparams(("arbitrary",)),
        name="outproj",
    )(x_all, c_all, o_all, w_out_bf, g_ffn.reshape(1, D_MODEL), w_router_t_bf,
      b_router.reshape(N_EXPERTS, 1))


def _route_body(t_p, t_pad, n_p, cap, lg_ref, d_ref, gt_ref, cnt_ref, tri_s, run_s):
    i = pl.program_id(0)
    tb = lg_ref.shape[1]

    @pl.when(i == 0)
    def _():
        r = lax.broadcasted_iota(jnp.int32, (tb, tb + 128), 0)
        c = lax.broadcasted_iota(jnp.int32, (tb, tb + 128), 1)
        tri_s[...] = jnp.where((r < c) | (c >= tb), 1.0, 0.0).astype(BF16)
        run_s[...] = jnp.zeros_like(run_s)

    tok = i * tb + lax.broadcasted_iota(jnp.int32, (1, tb), 1)
    valid = ((tok % t_pad) < t_p) | (tok >= n_p)
    eid = lax.broadcasted_iota(jnp.int32, (N_EXPERTS, tb), 0)
    lg = lg_ref[...]
    sel = jnp.zeros((N_EXPERTS, tb), F32)
    hot, top = [], []
    for _ in range(TOP_K):
        mx = jnp.max(lg, axis=0, keepdims=True)
        idx = jnp.min(jnp.where(lg == mx, eid, N_EXPERTS), axis=0, keepdims=True)
        one = eid == idx
        lg = jnp.where(one, -jnp.inf, lg)
        hot.append(one)
        top.append(mx)
        sel = sel + jnp.where(one & valid, 1.0, 0.0)
    ex = [jnp.exp(t - top[0]) for t in top]
    den = ex[0] + ex[1] + ex[2] + ex[3]
    cs = jnp.dot(sel.astype(BF16), tri_s[...], preferred_element_type=F32)
    run = run_s[...]
    slot = (cs[:, :tb] + jnp.concatenate([run] * (tb // 128), axis=1)
            + (eid * cap).astype(F32))
    for k in range(TOP_K):
        mine = jnp.sum(jnp.where(hot[k], slot, 0.0), axis=0, keepdims=True).astype(jnp.int32)
        trash = N_EXPERTS * cap + (tok % TB) * TOP_K + k
        d_ref[pl.ds(k, 1), :] = jnp.where(valid, mine, trash)
        gt_ref[pl.ds(k, 1), :] = ex[k] / den
    run = run + cs[:, tb:]
    run_s[...] = run
    cnt_ref[...] = run.astype(jnp.int32)


def _route(logits_t, t_p, t_pad, n_p, cap):
    n = logits_t.shape[1]
    blk = lambda: pl.BlockSpec((TOP_K, ROW_R), lambda i: (0, i))
    return pl.pallas_call(
        functools.partial(_route_body, t_p, t_pad, n_p, cap),
        grid=(n // ROW_R,),
        in_specs=[pl.BlockSpec((N_EXPERTS, ROW_R), lambda i: (0, i))],
        out_specs=[blk(), blk(), pl.BlockSpec((N_EXPERTS, 128), lambda i: (0, 0))],
        out_shape=[jax.ShapeDtypeStruct((TOP_K, n), jnp.int32),
                   jax.ShapeDtypeStruct((TOP_K, n), F32), jax.ShapeDtypeStruct((N_EXPERTS, 128), jnp.int32)],
        scratch_shapes=[pltpu.VMEM((ROW_R, ROW_R + 128), BF16), pltpu.VMEM((N_EXPERTS, 128), F32)],
        compiler_params=_cparams(("arbitrary",)),
        name="route",
    )(logits_t)


def _slot_table(dest, n_blocks):
    return dest.T.reshape(n_blocks, TB // 8, 8 * TOP_K)


def _slot_of(dest_ref, i, u, k):
    return dest_ref[0, i, u * TOP_K + k]


SLOT_BLOCK = (1, TB // 8, 8 * TOP_K)


def _dispatch_body(dest_ref, xp_ref, xs_hbm, sem):
    def row_copy(i, u, slot):
        return pltpu.make_async_copy(xp_ref.at[i, pl.ds(u, 1), :], xs_hbm.at[pl.ds(slot, 1), :], sem)

    def issue(i, c):
        for u in range(8):
            for k in range(TOP_K):
                row_copy(i, u, _slot_of(dest_ref, i, u, k)).start()
        return c

    def drain(i, c):
        for _ in range(8 * TOP_K):
            row_copy(0, 0, 0).wait()
        return c

    lax.fori_loop(0, TB // 8, issue, 0)
    lax.fori_loop(0, TB // 8, drain, 0)


def _dispatch(dest_tab, xn_packed, n_slots):
    n, width = xn_packed.shape
    return pl.pallas_call(
        _dispatch_body,
        grid=(n // TB,),
        in_specs=[
            pl.BlockSpec(SLOT_BLOCK, lambda i: (i, 0, 0), memory_space=pltpu.SMEM),
            pl.BlockSpec((TB // 8, 8, width), lambda i: (i, 0, 0)),
        ],
        out_specs=pl.BlockSpec(memory_space=pl.ANY),
        out_shape=jax.ShapeDtypeStruct((n_slots, width), jnp.uint32),
        scratch_shapes=[pltpu.SemaphoreType.DMA(())],
        compiler_params=_cparams(("arbitrary",)),
        name="dispatch",
    )(dest_tab, xn_packed.reshape(n // 8, 8, width))


def _moe_body(ie_ref, r0_ref, nr_ref, nlive_ref, xs_hbm, wg_ref, wl_ref, bg_ref, bl_ref, wd_ref, bd_ref,
              ys_hbm, xraw, xb16, yacc, pend_s, xsem, ysem):
    it = pl.program_id(0)
    j = pl.program_id(1)
    n_live = nlive_ref[0]
    n_f = pl.num_programs(1)

    def chunks(item):
        return (nr_ref[item] + CH - 1) // CH

    def x_copy(item, c):
        src = pl.multiple_of(r0_ref[item] + c * CH, CH)
        return pltpu.make_async_copy(xs_hbm.at[pl.ds(src, CH), :],
                                     xraw.at[pl.ds(pl.multiple_of(c * CH, CH), CH), :], xsem)

    def y_copy(item, c):
        dst = pl.multiple_of(r0_ref[item] + c * CH, CH)
        return pltpu.make_async_copy(yacc.at[pl.ds(pl.multiple_of(c * CH, CH), CH), :],
                                     ys_hbm.at[pl.ds(dst, CH), :], ysem)

    def for_chunks(n, fn):
        def body(c, carry):
            fn(c)
            return carry
        lax.fori_loop(0, n, body, 0)

    @pl.when((it == 0) & (j == 0))
    def _():
        pend_s[0] = 0
        for_chunks(chunks(0), lambda c: x_copy(0, c).start())

    @pl.when((it < n_live) & (j == 0))
    def _():
        nrows = nr_ref[it]
        for_chunks(chunks(it), lambda c: x_copy(it, c).wait())

        def unpack(c):
            rows = pl.ds(pl.multiple_of(c * CH, CH), CH)
            p = xraw[rows, :]
            rid = c * CH + lax.broadcasted_iota(jnp.int32, (CH, 1), 0)
            live = rid < nrows
            lo = lax.bitcast_convert_type(p << 16, F32)
            hi = lax.bitcast_convert_type(p & jnp.uint32(0xFFFF0000), F32)
            xb16[rows, 0:D_MODEL // 2] = jnp.where(live, lo, 0.0).astype(BF16)
            xb16[rows, D_MODEL // 2:] = jnp.where(live, hi, 0.0).astype(BF16)

        for_chunks(chunks(it), unpack)

        @pl.when(it + 1 < n_live)
        def _():
            for_chunks(chunks(it + 1), lambda c: x_copy(it + 1, c).start())

    @pl.when(it < n_live)
    def _():
        @pl.when(j == 0)
        def _():
            for_chunks(pend_s[0], lambda c: y_copy(it, 0).wait())
            pend_s[0] = 0

        def ffn_rows(r0, n):
            rows = pl.ds(r0, n)
            x = xb16[rows, :]
            hg = jnp.dot(x, wg_ref[0].astype(BF16), preferred_element_type=F32) + bg_ref[0]
            hl = jnp.dot(x, wl_ref[0].astype(BF16), preferred_element_type=F32) + bl_ref[0]
            hg = jnp.minimum(hg, SWIGLU_LIMIT)
            hl = jnp.clip(hl, -SWIGLU_LIMIT, SWIGLU_LIMIT)
            a = hg * jax.nn.sigmoid(SWIGLU_ALPHA * hg) * (hl + 1.0)
            part = jnp.dot(a.astype(BF16), wd_ref[0].astype(BF16), preferred_element_type=F32)
            yacc[rows, :] = jnp.where(j == 0, bd_ref[0], yacc[rows, :]) + part

        n_ch = chunks(it)
        n_full = n_ch // (SUB // CH)
        tail = n_ch - n_full * (SUB // CH)

        def full(s, carry):
            ffn_rows(pl.multiple_of(s * SUB, CH), SUB)
            return carry

        lax.fori_loop(0, n_full, full, 0)
        for t in range(1, SUB // CH):
            @pl.when(tail == t)
            def _():
                ffn_rows(pl.multiple_of(n_full * SUB, CH), t * CH)

        @pl.when(j == n_f - 1)
        def _():
            for_chunks(n_ch, lambda c: y_copy(it, c).start())
            pend_s[0] = n_ch

            @pl.when(it == n_live - 1)
            def _():
                for_chunks(n_ch, lambda c: y_copy(it, c).wait())
                pend_s[0] = 0


def _moe(xs, item_expert, item_row0, item_rows, n_live, w_up, b_up, w_down, b_down):
    max_items = item_expert.shape[0]
    n_f = D_FF // TF

    def tile(it, j, nl):
        return jnp.where(it < nl[0], j, n_f - 1)

    return pl.pallas_call(
        _moe_body,
        grid_spec=pltpu.PrefetchScalarGridSpec(
            num_scalar_prefetch=4,
            grid=(n_live[0], n_f),
            in_specs=[
                pl.BlockSpec(memory_space=pl.ANY),
                pl.BlockSpec((1, D_MODEL, TF), lambda it, j, ie, r0, nr, nl: (ie[it], 0, tile(it, j, nl))),
                pl.BlockSpec((1, D_MODEL, TF), lambda it, j, ie, r0, nr, nl: (ie[it], 0, n_f + tile(it, j, nl))),
                pl.BlockSpec((1, 1, TF), lambda it, j, ie, r0, nr, nl: (ie[it], 0, tile(it, j, nl))),
                pl.BlockSpec((1, 1, TF), lambda it, j, ie, r0, nr, nl: (ie[it], 0, n_f + tile(it, j, nl))),
                pl.BlockSpec((1, TF, D_MODEL), lambda it, j, ie, r0, nr, nl: (ie[it], tile(it, j, nl), 0)),
                pl.BlockSpec((1, 1, D_MODEL), lambda it, j, ie, r0, nr, nl: (ie[it], 0, 0)),
            ],
            out_specs=pl.BlockSpec(memory_space=pl.ANY),
            scratch_shapes=[
                pltpu.VMEM((TM, D_MODEL // 2), jnp.uint32),
                pltpu.VMEM((TM, D_MODEL), BF16),
                pltpu.VMEM((TM, D_MODEL), F32),
                pltpu.SMEM((1,), jnp.int32),
                pltpu.SemaphoreType.DMA(()),
                pltpu.SemaphoreType.DMA(()),
            ],
        ),
        out_shape=jax.ShapeDtypeStruct((xs.shape[0], D_MODEL), F32),
        compiler_params=_cparams(("arbitrary", "arbitrary")),
        name="moe",
    )(item_expert, item_row0, item_rows, n_live, xs, w_up, w_up,
      b_up.reshape(N_EXPERTS, 1, 2 * D_FF), b_up.reshape(N_EXPERTS, 1, 2 * D_FF), w_down,
      b_down.reshape(N_EXPERTS, 1, D_MODEL))


def _combine_body(blocks_per_batch, n_prompt_blocks, t_pad, n_p, dest_ref, dest_next_ref, ys_hbm, h_hbm,
                  gt_ref, gf_ref, yp_ref, ysm_ref, gbuf, hbuf, gsem, hsem):
    i = pl.program_id(0)
    n_blocks = pl.num_programs(0)
    cur = i % 2

    def h_copy(blk, slot):
        row = jnp.where(blk < n_prompt_blocks,
                        (blk // blocks_per_batch) * t_pad + N_META + (blk % blocks_per_batch) * TB, n_p)
        return pltpu.make_async_copy(h_hbm.at[pl.ds(pl.multiple_of(row, 8), TB), :], hbuf.at[slot],
                                     hsem.at[slot])

    def row_copy(slot, buf, k, c, u):
        return pltpu.make_async_copy(ys_hbm.at[pl.ds(slot, 1), :], gbuf.at[buf, k, c, pl.ds(u, 1), :],
                                     gsem.at[buf])

    def request(table_ref, blk, buf):
        h_copy(blk, buf).start()

        def issue(c, carry):
            for u in range(8):
                for k in range(TOP_K):
                    row_copy(_slot_of(table_ref, c, u, k), buf, k, c, u).start()
            return carry

        lax.fori_loop(0, TB // 8, issue, 0)

    def drain(c, carry):
        for _ in range(8 * TOP_K):
            row_copy(0, cur, 0, 0, 0).wait()
        return carry

    @pl.when(i == 0)
    def _():
        request(dest_ref, 0, 0)

    @pl.when(i + 1 < n_blocks)
    def _():
        request(dest_next_ref, i + 1, 1 - cur)

    h_copy(i, cur).wait()
    lax.fori_loop(0, TB // 8, drain, 0)
    y = hbuf[cur]
    for k in range(TOP_K):
        y = y + gbuf[cur, k].reshape(TB, D_MODEL) * gt_ref[:, k:k + 1]
    ms = jnp.mean(y * y, axis=-1, keepdims=True)
    y = y * lax.rsqrt(ms + EPS) * gf_ref[...]

    @pl.when(i < n_prompt_blocks)
    def _():
        yp_ref[...] = y

    @pl.when(i >= n_prompt_blocks)
    def _():
        ysm_ref[...] = y


def _combine(dest_blk, ys, h_all, gates_t, g_final, blocks_per_batch, n_prompt_blocks, t_pad, n_p):
    return pl.pallas_call(
        functools.partial(_combine_body, blocks_per_batch, n_prompt_blocks, t_pad, n_p),
        grid=(n_prompt_blocks + 1,),
        in_specs=[
            pl.BlockSpec(SLOT_BLOCK, lambda i: (i, 0, 0), memory_space=pltpu.SMEM),
            pl.BlockSpec(SLOT_BLOCK, lambda i: (jnp.minimum(i + 1, n_prompt_blocks), 0, 0),
                         memory_space=pltpu.SMEM),
            pl.BlockSpec(memory_space=pl.ANY),
            pl.BlockSpec(memory_space=pl.ANY),
            pl.BlockSpec((TB, TOP_K), lambda i: (i, 0)),
            pl.BlockSpec((1, D_MODEL), lambda i: (0, 0)),
        ],
        out_specs=[
            pl.BlockSpec((TB, D_MODEL), lambda i: (jnp.minimum(i, n_prompt_blocks - 1), 0)),
            pl.BlockSpec((TB, D_MODEL), lambda i: (0, 0)),
        ],
        out_shape=[jax.ShapeDtypeStruct((n_prompt_blocks * TB, D_MODEL), F32),
                   jax.ShapeDtypeStruct((TB, D_MODEL), F32)],
        scratch_shapes=[pltpu.VMEM((2, TOP_K, TB // 8, 8, D_MODEL), F32), pltpu.VMEM((2, TB, D_MODEL), F32),
                        pltpu.SemaphoreType.DMA((2,)), pltpu.SemaphoreType.DMA((2,))],
        compiler_params=_cparams(("arbitrary",)),
        name="combine",
    )(dest_blk, dest_blk, ys, h_all, gates_t, g_final.reshape(1, D_MODEL))


def _work_items(counts, cap, max_items):
    items_per = (counts + TM - 1) // TM
    item_end = jnp.cumsum(items_per)
    item_start = item_end - items_per
    n_live = item_end[-1]
    it = jnp.arange(max_items, dtype=jnp.int32)
    it_live = jnp.minimum(it, n_live - 1)
    expert = jnp.minimum(jnp.searchsorted(item_end, it_live, side="right"), N_EXPERTS - 1).astype(jnp.int32)
    part = it_live - item_start[expert]
    rows = jnp.clip(counts[expert] - part * TM, 0, TM)
    rows = jnp.where(it < n_live, rows, 0).astype(jnp.int32)
    row0 = (expert * cap + part * TM).astype(jnp.int32)
    return expert, row0, rows, n_live.reshape(1).astype(jnp.int32)


def kernel(x_prompt, x_sample, cache_k, cache_v, state_conv, meta_tokens, g_mix, w_in, conv_w, conv_b,
           conv_ln_g, conv_ln_b, g_attn_out, w_out, g_ffn, w_router, b_router, w_up, b_up, w_down,
           b_down, g_final):
    n_batch, seq, _ = x_prompt.shape
    n_streams, dec_seq, _ = x_sample.shape
    depth = w_in.shape[0]
    assert depth == 1 and n_streams * dec_seq == BLK
    t_p = N_META + seq
    t_pad = -(-t_p // BLK) * BLK
    n_p = n_batch * t_pad
    n_s = n_streams * dec_seq
    n_all = n_p + n_s
    assert n_all % ROW_A == 0 and n_all % ROW_E == 0 and n_all % ROW_R == 0 and n_all % TB == 0

    meta = meta_tokens.astype(F32)
    pad = jnp.zeros((t_pad - t_p, D_MODEL), F32)
    pieces = []
    for b in range(n_batch):
        pieces += [meta, x_prompt[b], pad]
    x_all = jnp.concatenate(pieces + [x_sample.reshape(n_s, D_MODEL)], axis=0)

    l = 0
    u_all, q_all, kvb, k_p, v_p, k_s, v_s = _inproj(
        x_all, g_mix[l], w_in[l].astype(BF16), n_batch, t_p, t_pad)
    c_all = _conv(u_all, state_conv[l], conv_w[l], conv_b[l], conv_ln_g[l], conv_ln_b[l],
                  t_pad // BLK, n_p // BLK)
    g_heads = g_attn_out[l].reshape(N_HEADS, HEAD_DIM)
    o_all = _attn_prompt(q_all, kvb, g_heads, n_batch, t_pad)
    o_all = _attn_sample(q_all, kvb, cache_k[l:l + 1], cache_v[l:l + 1], g_heads, o_all, n_p, dec_seq, BLK)
    h_all, xn_packed, logits_t = _outproj(x_all, c_all, o_all, w_out[l].astype(BF16), g_ffn[l],
                                          w_router[l].T.astype(BF16), b_router[l])

    n_valid = n_batch * t_p + n_s
    cap = -(-n_valid // CH) * CH
    n_slots = N_EXPERTS * cap + TB * TOP_K
    max_items = -(-(n_valid * TOP_K) // TM) + N_EXPERTS
    dest, gates, counts = _route(logits_t, t_p, t_pad, n_p, cap)
    xs = _dispatch(_slot_table(dest, n_all // TB), xn_packed, n_slots)
    item_expert, item_row0, item_rows, n_live = _work_items(counts[:, 0], cap, max_items)
    ys = _moe(xs, item_expert, item_row0, item_rows, n_live, w_up[l], b_up[l], w_down[l], b_down[l])

    def out_tokens(a):
        a_p = a[:, :n_p].reshape(-1, n_batch, t_pad)[:, :, N_META:t_p].reshape(-1, n_batch * seq)
        return jnp.concatenate([a_p, a[:, n_p:]], axis=1)

    assert seq % TB == 0 and n_s == TB
    n_out_blocks = n_batch * seq // TB
    dest_out = _slot_table(out_tokens(dest), n_out_blocks + 1)
    y_p, y_s = _combine(dest_out, ys, h_all, out_tokens(gates).T, g_final, seq // TB, n_out_blocks, t_pad, n_p)

    y_prompt = y_p.reshape(n_batch, seq, D_MODEL)
    y_sample = y_s.reshape(n_streams, dec_seq, D_MODEL)
    k_prompt = k_p[None]
    v_prompt = v_p[None]
    conv_prompt = jnp.stack([u_all[b * t_pad + t_p - (CONV_WIDTH - 1):b * t_pad + t_p]
                             for b in range(n_batch)])[None]
    k_sample = k_s.reshape(1, n_streams, dec_seq, N_HEADS, HEAD_DIM)
    v_sample = v_s.reshape(1, n_streams, dec_seq, N_HEADS, HEAD_DIM)
    u_s = u_all[n_p:].reshape(n_streams, dec_seq, D_CONV)
    conv_sample = jnp.concatenate([state_conv[l], u_s], axis=1)[:, -(CONV_WIDTH - 1):][None]
    return (y_prompt, y_sample, k_prompt, v_prompt, conv_prompt, k_sample, v_sample, conv_sample)
```

```python
import functools
import math

import jax
import jax.numpy as jnp
from jax import lax
from jax.experimental import pallas as pl
from jax.experimental.pallas import tpu as pltpu

F32 = jnp.float32
BF16 = jnp.bfloat16

D_MODEL = 2048
N_META = 16
D_CONV = 1024
N_HEADS = 8
HEAD_DIM = 128
D_ATTN = N_HEADS * HEAD_DIM
D_IN = 2 * D_CONV + 3 * D_ATTN
CONV_WIDTH = 31
N_EXPERTS = 32
TOP_K = 4
D_FF = D_MODEL
SWIGLU_ALPHA = 1.702
SWIGLU_LIMIT = 7.0
EPS = 1e-5

BLK = 256
HPS = 8
HALO = 32
COL = 512
ROW_A = 640
ROW_E = 640
ROW_R = 640
TM = 2048
CH = 128
SUB = 6 * CH
TF = 256
TB = 256
LOG_ZERO = -104.0
VMEM_LIMIT = 56 * 1024 * 1024


def _cparams(sem, vmem=VMEM_LIMIT):
    return pltpu.CompilerParams(dimension_semantics=sem, vmem_limit_bytes=vmem)


def _inproj_body(t_p, t_pad, n_p, x_ref, g_ref, wa_ref, wg_ref, wq_ref, w1_ref, w2_ref,
                 u_ref, q_ref, kvb_ref, kp_hbm, vp_hbm, ks_hbm, vs_hbm, xn_s, kv4_s, ksem, vsem):
    i = pl.program_id(0)
    j = pl.program_id(1)
    heads_per_tile = COL // HEAD_DIM
    k4_s, v4_s = kv4_s.at[0], kv4_s.at[1]

    def chunk_copies(tile, src, dst_p, dst_s, sem):
        out = []
        per_batch, full, rem = t_pad // CH, t_p // CH, t_p % CH
        for c in range(ROW_A // CH):
            g = tile * (ROW_A // CH) + c
            is_p = g < n_p // CH
            b, wi = g // per_batch, g % per_batch
            out.append((is_p & (wi < full), pltpu.make_async_copy(
                src.at[pl.ds(c * CH, CH)], dst_p.at[b, pl.ds(pl.multiple_of(wi * CH, CH), CH)], sem)))
            if rem:
                out.append((is_p & (wi == full), pltpu.make_async_copy(
                    src.at[pl.ds(c * CH, rem)], dst_p.at[b, pl.ds(full * CH, rem)], sem)))
            out.append((jnp.logical_not(is_p), pltpu.make_async_copy(
                src.at[pl.ds(c * CH, CH)],
                dst_s.at[pl.ds(pl.multiple_of((g - n_p // CH) * CH, CH), CH)], sem)))
        return out

    def start_all(copies):
        for cond, cp in copies:
            @pl.when(cond)
            def _():
                cp.start()

    def wait_all(copies):
        for cond, cp in copies:
            @pl.when(cond)
            def _():
                cp.wait()

    def store_heads(dst, p, first_head):
        for hh in range(heads_per_tile):
            dst[:, first_head + hh, :] = p[:, hh * HEAD_DIM:(hh + 1) * HEAD_DIM]

    @pl.when(j == 0)
    def _():
        x = x_ref[...]
        ms = jnp.mean(x * x, axis=-1, keepdims=True)
        xn_s[...] = (x * lax.rsqrt(ms + EPS) * g_ref[...]).astype(BF16)

        @pl.when(i > 0)
        def _():
            wait_all(chunk_copies(i - 1, k4_s, kp_hbm, ks_hbm, ksem))

    @pl.when((j == 1) & (i > 0))
    def _():
        wait_all(chunk_copies(i - 1, v4_s, vp_hbm, vs_hbm, vsem))

    xn = xn_s[...]
    proj = lambda w_ref: jnp.dot(xn, w_ref[...], preferred_element_type=F32)
    u_ref[...] = proj(wa_ref) * jax.nn.sigmoid(proj(wg_ref))
    q_ref[...] = (proj(wq_ref) * (1.0 / math.sqrt(HEAD_DIM))).astype(BF16)
    for t, w_ref in enumerate((w1_ref, w2_ref)):
        p = proj(w_ref)
        kvb_ref[0, :, t * COL:(t + 1) * COL] = p.astype(BF16)
        store_heads(kv4_s.at[j], p, t * heads_per_tile)

    @pl.when(j == 0)
    def _():
        start_all(chunk_copies(i, k4_s, kp_hbm, ks_hbm, ksem))

    @pl.when(j == 1)
    def _():
        start_all(chunk_copies(i, v4_s, vp_hbm, vs_hbm, vsem))

        @pl.when(i == pl.num_programs(0) - 1)
        def _():
            wait_all(chunk_copies(i, k4_s, kp_hbm, ks_hbm, ksem))
            wait_all(chunk_copies(i, v4_s, vp_hbm, vs_hbm, vsem))


def _inproj(x_all, g_mix, w_in_bf, n_batch, t_p, t_pad):
    n = x_all.shape[0]
    n_p = n_batch * t_pad
    assert D_IN == 10 * COL and COL == 4 * HEAD_DIM and ROW_A % CH == 0 and t_pad % CH == 0

    def w_spec(first, stride):
        return pl.BlockSpec((D_MODEL, COL), lambda i, j: (0, first + stride * j))

    wide = lambda dt: jax.ShapeDtypeStruct((n, D_CONV), dt)
    cache_p = jax.ShapeDtypeStruct((n_batch, t_p, N_HEADS, HEAD_DIM), F32)
    cache_s = jax.ShapeDtypeStruct((n - n_p, N_HEADS, HEAD_DIM), F32)
    any_spec = pl.BlockSpec(memory_space=pl.ANY)
    half_spec = pl.BlockSpec((ROW_A, COL), lambda i, j: (i, j))
    return pl.pallas_call(
        functools.partial(_inproj_body, t_p, t_pad, n_p),
        grid=(n // ROW_A, 2),
        in_specs=[
            pl.BlockSpec((ROW_A, D_MODEL), lambda i, j: (i, 0)),
            pl.BlockSpec((1, D_MODEL), lambda i, j: (0, 0)),
            w_spec(0, 1), w_spec(2, 1), w_spec(4, 1), w_spec(6, 2), w_spec(7, 2),
        ],
        out_specs=[half_spec, half_spec, pl.BlockSpec((1, ROW_A, D_ATTN), lambda i, j: (j, i, 0)),
                   any_spec, any_spec, any_spec, any_spec],
        out_shape=[wide(F32), wide(BF16), jax.ShapeDtypeStruct((2, n, D_ATTN), BF16),
                   cache_p, cache_p, cache_s, cache_s],
        scratch_shapes=[pltpu.VMEM((ROW_A, D_MODEL), BF16),
                        pltpu.VMEM((2, ROW_A, N_HEADS, HEAD_DIM), F32),
                        pltpu.SemaphoreType.DMA(()), pltpu.SemaphoreType.DMA(())],
        compiler_params=_cparams(("arbitrary", "arbitrary")),
        name="inproj",
    )(x_all, g_mix.reshape(1, D_MODEL), w_in_bf, w_in_bf, w_in_bf, w_in_bf, w_in_bf)


def _conv_taps(ext_ref, w_ref, b_ref, dst_ref, src_row0, dst_row0, nrows):
    def chunk(cc, carry):
        lanes = pl.ds(pl.multiple_of(cc * 128, 128), 128)
        acc = jnp.broadcast_to(b_ref[:, lanes], (nrows, 128))
        for j in range(CONV_WIDTH):
            acc = acc + ext_ref[pl.ds(src_row0 + j, nrows), lanes] * w_ref[pl.ds(j, 1), lanes]
        dst_ref[pl.ds(dst_row0, nrows), lanes] = acc
        return carry

    lax.fori_loop(0, D_CONV // 128, chunk, 0)


def _conv_block(ext_ref, sh_ref, w_ref, b_ref, dst_ref, off):
    span = HALO + BLK

    def chunk(cc, carry):
        lanes = pl.ds(pl.multiple_of(cc * 128, 128), 128)
        for r in range(8):
            n = span - (8 if r else 0)
            sh_ref[r, 0:n, :] = ext_ref[pl.ds(r, n), lanes]
        for rc in range(BLK // 64):
            acc = jnp.broadcast_to(b_ref[:, lanes], (64, 128))
            for j in range(CONV_WIDTH):
                r = (off + j) % 8
                acc = acc + sh_ref[r, pl.ds(off + j - r + rc * 64, 64), :] * w_ref[pl.ds(j, 1), lanes]
            dst_ref[pl.ds(rc * 64, 64), lanes] = acc
        return carry

    lax.fori_loop(0, D_CONV // 128, chunk, 0)


def _conv_body(blocks_per_batch, n_prompt_blocks, u_ref, prev_ref, state_ref, w_ref, b_ref,
               lg_ref, lb_ref, c_ref, ext_s, conv_s, sh_s):
    i = pl.program_id(0)
    off = HALO - (CONV_WIDTH - 1)

    @pl.when(i < n_prompt_blocks)
    def _():
        first = (i % blocks_per_batch) == 0
        ext_s[0:HALO, :] = jnp.where(first, 0.0, prev_ref[...])
        ext_s[HALO:HALO + BLK, :] = u_ref[...]
        _conv_block(ext_s, sh_s, w_ref, b_ref, conv_s, off)

    @pl.when(i >= n_prompt_blocks)
    def _():
        def stream(s, carry):
            r0 = pl.multiple_of(s * 16, 16)
            ext_s[off:HALO, :] = state_ref[s]
            ext_s[HALO:HALO + 16, :] = u_ref[pl.ds(r0, 16), :]
            _conv_taps(ext_s, w_ref, b_ref, conv_s, off, r0, 16)
            return carry

        lax.fori_loop(0, BLK // 16, stream, 0)

    x = conv_s[...]
    mu = jnp.mean(x, axis=-1, keepdims=True)
    xc = x - mu
    var = jnp.mean(xc * xc, axis=-1, keepdims=True)
    y = xc * lax.rsqrt(var + EPS) * lg_ref[...] + lb_ref[...]
    c_ref[...] = (y * jax.nn.sigmoid(y)).astype(BF16)


def _conv(u_all, state_conv, conv_w, conv_b, ln_g, ln_b, blocks_per_batch, n_prompt_blocks):
    n = u_all.shape[0]
    row = lambda a: a.reshape(1, D_CONV)
    const = lambda shape: pl.BlockSpec(shape, lambda i: (0,) * len(shape))
    return pl.pallas_call(
        functools.partial(_conv_body, blocks_per_batch, n_prompt_blocks),
        grid=(n // BLK,),
        in_specs=[
            pl.BlockSpec((BLK, D_CONV), lambda i: (i, 0)),
            pl.BlockSpec((HALO, D_CONV), lambda i: (jnp.maximum(i * (BLK // HALO) - 1, 0), 0)),
            const(state_conv.shape),
            const((CONV_WIDTH, D_CONV)),
            const((1, D_CONV)), const((1, D_CONV)), const((1, D_CONV)),
        ],
        out_specs=pl.BlockSpec((BLK, D_CONV), lambda i: (i, 0)),
        out_shape=jax.ShapeDtypeStruct((n, D_CONV), BF16),
        scratch_shapes=[pltpu.VMEM((HALO + BLK, D_CONV), F32), pltpu.VMEM((BLK, D_CONV), F32),
                        pltpu.VMEM((8, HALO + BLK, 128), F32)],
        compiler_params=_cparams(("arbitrary",)),
        name="conv",
    )(u_all, u_all, state_conv, conv_w, row(conv_b), row(ln_g), row(ln_b))


def _suffix_matrix(n, extra):
    r = lax.broadcasted_iota(jnp.int32, (n, n + extra), 0)
    c = lax.broadcasted_iota(jnp.int32, (n, n + extra), 1)
    return jnp.where((r > c) | (c >= n), 1.0, 0.0).astype(BF16)


def _split_dot(x, m):
    hi = x.astype(BF16)
    lo = (x - hi.astype(F32)).astype(BF16)
    return (jnp.dot(hi, m, preferred_element_type=F32) + jnp.dot(lo, m, preferred_element_type=F32))


def _log_keep(z):
    return -(jnp.maximum(z, 0.0) + jnp.log(1.0 + jnp.exp(-jnp.abs(z))))


def _attn_prompt_body(q_ref, k_ref, v_ref, g_ref, o_ref, m_s):
    hg = pl.program_id(1)
    qi = pl.program_id(2)

    @pl.when((pl.program_id(0) == 0) & (hg == 0) & (qi == 0))
    def _():
        m_s[...] = _suffix_matrix(BLK, 128)

    m = m_s[...]

    def tile(hh, kb, carry, diagonal):
        lanes = slice(hh * HEAD_DIM, (hh + 1) * HEAD_DIM)
        rows = pl.ds(pl.multiple_of(kb * BLK, BLK), BLK)
        z = lax.dot_general(q_ref[:, lanes], k_ref[rows, lanes], (((1,), (1,)), ((), ())),
                            preferred_element_type=F32)
        lk = _log_keep(z)
        if diagonal:
            keep = (lax.broadcasted_iota(jnp.int32, (BLK, BLK), 1)
                    < lax.broadcasted_iota(jnp.int32, (BLK, BLK), 0))
            lk = jnp.where(keep, lk, 0.0)
        cs = _split_dot(lk, m)
        after = cs[:, :BLK]
        if carry is not None:
            after = after + jnp.concatenate([carry, carry], axis=1)
        w = jnp.exp(z + lk + after)
        if diagonal:
            w = jnp.where(keep, w, 0.0)
        pv = jnp.dot(w.astype(BF16), v_ref[rows, lanes], preferred_element_type=F32)
        return pv, cs[:, BLK:]

    first = [tile(hh, qi, None, True) for hh in range(HPS)]

    def cond(st):
        kb, _, carries = st
        live = jnp.max(carries[0])
        for c in carries[1:]:
            live = jnp.maximum(live, jnp.max(c))
        return (kb >= 0) & (live > LOG_ZERO)

    def step(st):
        kb, accs, carries = st
        out = [tile(hh, kb, carries[hh], False) for hh in range(HPS)]
        return (kb - 1, tuple(a + o[0] for a, o in zip(accs, out)),
                tuple(c + o[1] for c, o in zip(carries, out)))

    _, accs, _ = lax.while_loop(cond, step, (qi - 1, tuple(f[0] for f in first), tuple(f[1] for f in first)))
    for hh in range(HPS):
        acc = accs[hh]
        ms = jnp.mean(acc * acc, axis=-1, keepdims=True)
        o_ref[:, hh * HEAD_DIM:(hh + 1) * HEAD_DIM] = (
            acc * lax.rsqrt(ms + EPS) * g_ref[pl.ds(hg * HPS + hh, 1), :]).astype(BF16)


def _attn_prompt(q_all, kvb, g_heads, n_batch, t_pad):
    nq = t_pad // BLK
    wide = HPS * HEAD_DIM
    return pl.pallas_call(
        _attn_prompt_body,
        grid=(n_batch, N_HEADS // HPS, nq),
        in_specs=[
            pl.BlockSpec((BLK, wide), lambda b, h, i: (b * nq + i, h)),
            pl.BlockSpec((None, t_pad, wide), lambda b, h, i: (0, b, h)),
            pl.BlockSpec((None, t_pad, wide), lambda b, h, i: (1, b, h)),
            pl.BlockSpec((N_HEADS, HEAD_DIM), lambda b, h, i: (0, 0)),
        ],
        out_specs=pl.BlockSpec((BLK, wide), lambda b, h, i: (b * nq + i, h)),
        out_shape=jax.ShapeDtypeStruct((q_all.shape[0], D_ATTN), BF16),
        scratch_shapes=[pltpu.VMEM((BLK, BLK + 128), BF16)],
        compiler_params=_cparams(("arbitrary", "arbitrary", "arbitrary")),
        name="attn_prompt",
    )(q_all, kvb, kvb, g_heads)


def _attn_sample_body(n_kb, kblk, q_ref, kn_ref, vn_ref, kc_hbm, vc_hbm, g_ref, o_in_hbm, o_ref,
                      kbuf, vbuf, qbd_s, acc_s, carry_s, m_s, mn_s, ksem, vsem):
    s = pl.program_id(0)
    n_streams = pl.num_programs(0)
    dq = q_ref.shape[0]
    last = n_kb - 1

    def cache_copies(stream, jb, slot):
        rows = pl.ds(pl.multiple_of(jb * kblk, kblk), kblk)
        return (pltpu.make_async_copy(kc_hbm.at[0, stream, rows], kbuf.at[slot], ksem.at[slot]),
                pltpu.make_async_copy(vc_hbm.at[0, stream, rows], vbuf.at[slot], vsem.at[slot]))

    def start(stream, jb, slot):
        for cp in cache_copies(stream, jb, slot):
            cp.start()

    def wait(stream, jb, slot):
        for cp in cache_copies(stream, jb, slot):
            cp.wait()

    def cat_heads(buf, slot):
        return jnp.concatenate([buf[slot, :, hh, :] for hh in range(N_HEADS)], axis=1).astype(BF16)

    def scores(kcat):
        return jnp.dot(kcat, qbd_s[...], preferred_element_type=F32)

    def add_values(w, vcat):
        full = lax.dot_general(w.astype(BF16), vcat, (((0,), (0,)), ((), ())), preferred_element_type=F32)
        for hh in range(N_HEADS):
            acc_s[hh] += full[hh * dq:(hh + 1) * dq, hh * HEAD_DIM:(hh + 1) * HEAD_DIM]

    @pl.when(s == 0)
    def _():
        start(0, last, last % 2)
        r = lax.broadcasted_iota(jnp.int32, (kblk, kblk), 0)
        c = lax.broadcasted_iota(jnp.int32, (kblk, kblk), 1)
        m_s[...] = jnp.where(c > r, 1.0, 0.0).astype(BF16)
        mn_s[...] = m_s[0:dq, 0:dq]

    qrep = jnp.concatenate([q_ref[...].astype(F32)] * N_HEADS, axis=0)
    qt = qrep.T
    row_head = lax.broadcasted_iota(jnp.int32, (D_ATTN, 128), 0) // HEAD_DIM
    col_head = lax.broadcasted_iota(jnp.int32, (D_ATTN, 128), 1) // dq
    qbd_s[...] = jnp.where(row_head == col_head, qt, 0.0).astype(BF16)
    acc_s[...] = jnp.zeros_like(acc_s)
    z = scores(kn_ref[...])
    lk = _log_keep(z)
    key = lax.broadcasted_iota(jnp.int32, (dq, 128), 0)
    qry = lax.broadcasted_iota(jnp.int32, (dq, 128), 1) % dq
    keep = key < qry
    lk = jnp.where(keep, lk, 0.0)
    after = _split_dot_left(mn_s[...], lk)
    w = jnp.where(keep, jnp.exp(z + lk + after), 0.0)
    add_values(w, vn_ref[...])
    carry_s[...] = jnp.sum(lk, axis=0, keepdims=True)

    def cond(jb):
        return (jb >= 0) & (jnp.max(carry_s[...]) > LOG_ZERO)

    def step(jb):
        slot = jb % 2
        wait(s, jb, slot)

        @pl.when(jb > 0)
        def _():
            start(s, jb - 1, 1 - slot)

        z = scores(cat_heads(kbuf, slot))
        lk = _log_keep(z)
        after = _split_dot_left(m_s[...], lk) + carry_s[...]
        w = jnp.exp(z + lk + after)
        add_values(w, cat_heads(vbuf, slot))
        carry_s[...] += jnp.sum(lk, axis=0, keepdims=True)
        return jb - 1

    jb_end = lax.while_loop(cond, step, last)

    @pl.when(jb_end >= 0)
    def _():
        wait(s, jb_end, jb_end % 2)

    @pl.when(s + 1 < n_streams)
    def _():
        start(s + 1, last, last % 2)

    for hh in range(N_HEADS):
        a = acc_s[hh]
        ms = jnp.mean(a * a, axis=-1, keepdims=True)
        o_ref[:, hh * HEAD_DIM:(hh + 1) * HEAD_DIM] = (
            a * lax.rsqrt(ms + EPS) * g_ref[pl.ds(hh, 1), :]).astype(BF16)


def _split_dot_left(m, x):
    hi = x.astype(BF16)
    lo = (x - hi.astype(F32)).astype(BF16)
    return (jnp.dot(m, hi, preferred_element_type=F32) + jnp.dot(m, lo, preferred_element_type=F32))


def _attn_sample(q_all, kvb, cache_k, cache_v, g_heads, o_all, row0, dec_seq, kblk):
    n_streams = cache_k.shape[1]
    past = cache_k.shape[2]
    n_kb = past // kblk
    assert n_kb * kblk == past and row0 % dec_seq == 0
    row_spec = pl.BlockSpec((dec_seq, D_ATTN), lambda s: (row0 // dec_seq + s, 0))
    kv_spec = lambda which: pl.BlockSpec((None, dec_seq, D_ATTN), lambda s: (which, row0 // dec_seq + s, 0))
    cache_buf = pltpu.VMEM((2, kblk, N_HEADS, HEAD_DIM), cache_k.dtype)
    any_spec = pl.BlockSpec(memory_space=pl.ANY)
    return pl.pallas_call(
        functools.partial(_attn_sample_body, n_kb, kblk),
        grid=(n_streams,),
        in_specs=[row_spec, kv_spec(0), kv_spec(1), any_spec, any_spec,
                  pl.BlockSpec((N_HEADS, HEAD_DIM), lambda s: (0, 0)), any_spec],
        out_specs=row_spec,
        out_shape=jax.ShapeDtypeStruct(o_all.shape, o_all.dtype),
        input_output_aliases={6: 0},
        scratch_shapes=[
            cache_buf, cache_buf,
            pltpu.VMEM((D_ATTN, 128), BF16),
            pltpu.VMEM((N_HEADS, dec_seq, HEAD_DIM), F32),
            pltpu.VMEM((1, 128), F32),
            pltpu.VMEM((kblk, kblk), BF16),
            pltpu.VMEM((dec_seq, dec_seq), BF16),
            pltpu.SemaphoreType.DMA((2,)), pltpu.SemaphoreType.DMA((2,)),
        ],
        compiler_params=_cparams(("arbitrary",)),
        name="attn_sample",
    )(q_all, kvb, kvb, cache_k, cache_v, g_heads, o_all)


def _outproj_body(x_ref, c_ref, o_ref, w_ref, g_ref, wr_ref, br_ref, h_ref, xn_ref, lg_ref):
    h = (x_ref[...]
         + jnp.dot(c_ref[...], w_ref[0:D_CONV, :], preferred_element_type=F32)
         + jnp.dot(o_ref[...], w_ref[D_CONV:, :], preferred_element_type=F32))
    h_ref[...] = h
    ms = jnp.mean(h * h, axis=-1, keepdims=True)
    xn = (h * lax.rsqrt(ms + EPS) * g_ref[...]).astype(BF16)
    bits = lax.bitcast_convert_type(xn.astype(F32), jnp.uint32)
    xn_ref[...] = (bits[:, :D_MODEL // 2] >> 16) | bits[:, D_MODEL // 2:]
    lg_ref[...] = lax.dot_general(wr_ref[...], xn, (((1,), (1,)), ((), ())),
                                  preferred_element_type=F32) + br_ref[...]


def _outproj(x_all, c_all, o_all, w_out_bf, g_ffn, w_router_t_bf, b_router):
    n = x_all.shape[0]
    const = lambda shape: pl.BlockSpec(shape, lambda i: (0,) * len(shape))
    return pl.pallas_call(
        _outproj_body,
        grid=(n // ROW_E,),
        in_specs=[
            pl.BlockSpec((ROW_E, D_MODEL), lambda i: (i, 0)),
            pl.BlockSpec((ROW_E, D_CONV), lambda i: (i, 0)),
            pl.BlockSpec((ROW_E, D_ATTN), lambda i: (i, 0)),
            const((D_MODEL, D_MODEL)),
            const((1, D_MODEL)),
            const((N_EXPERTS, D_MODEL)),
            const((N_EXPERTS, 1)),
        ],
        out_specs=[
            pl.BlockSpec((ROW_E, D_MODEL), lambda i: (i, 0)),
            pl.BlockSpec((ROW_E, D_MODEL // 2), lambda i: (i, 0)),
            pl.BlockSpec((N_EXPERTS, ROW_E), lambda i: (0, i)),
        ],
        out_shape=[jax.ShapeDtypeStruct((n, D_MODEL), F32), jax.ShapeDtypeStruct((n, D_MODEL // 2), jnp.uint32),
                   jax.ShapeDtypeStruct((N_EXPERTS, n), F32)],
        compiler_params=_cparams(("arbitrary",)),
        name="outproj",
    )(x_all, c_all, o_all, w_out_bf, g_ffn.reshape(1, D_MODEL), w_router_t_bf,
      b_router.reshape(N_EXPERTS, 1))


def _route_body(t_p, t_pad, n_p, cap, lg_ref, d_ref, gt_ref, cnt_ref, tri_s, run_s):
    i = pl.program_id(0)
    tb = lg_ref.shape[1]

    @pl.when(i == 0)
    def _():
        r = lax.broadcasted_iota(jnp.int32, (tb, tb + 128), 0)
        c = lax.broadcasted_iota(jnp.int32, (tb, tb + 128), 1)
        tri_s[...] = jnp.where((r < c) | (c >= tb), 1.0, 0.0).astype(BF16)
        run_s[...] = jnp.zeros_like(run_s)

    tok = i * tb + lax.broadcasted_iota(jnp.int32, (1, tb), 1)
    valid = ((tok % t_pad) < t_p) | (tok >= n_p)
    eid = lax.broadcasted_iota(jnp.int32, (N_EXPERTS, tb), 0)
    lg = lg_ref[...]
    sel = jnp.zeros((N_EXPERTS, tb), F32)
    hot, top = [], []
    for _ in range(TOP_K):
        mx = jnp.max(lg, axis=0, keepdims=True)
        idx = jnp.min(jnp.where(lg == mx, eid, N_EXPERTS), axis=0, keepdims=True)
        one = eid == idx
        lg = jnp.where(one, -jnp.inf, lg)
        hot.append(one)
        top.append(mx)
        sel = sel + jnp.where(one & valid, 1.0, 0.0)
    ex = [jnp.exp(t - top[0]) for t in top]
    den = ex[0] + ex[1] + ex[2] + ex[3]
    cs = jnp.dot(sel.astype(BF16), tri_s[...], preferred_element_type=F32)
    run = run_s[...]
    slot = (cs[:, :tb] + jnp.concatenate([run] * (tb // 128), axis=1)
            + (eid * cap).astype(F32))
    for k in range(TOP_K):
        mine = jnp.sum(jnp.where(hot[k], slot, 0.0), axis=0, keepdims=True).astype(jnp.int32)
        trash = N_EXPERTS * cap + (tok % TB) * TOP_K + k
        d_ref[pl.ds(k, 1), :] = jnp.where(valid, mine, trash)
        gt_ref[pl.ds(k, 1), :] = ex[k] / den
    run = run + cs[:, tb:]
    run_s[...] = run
    cnt_ref[...] = run.astype(jnp.int32)


def _route(logits_t, t_p, t_pad, n_p, cap):
    n = logits_t.shape[1]
    blk = lambda: pl.BlockSpec((TOP_K, ROW_R), lambda i: (0, i))
    return pl.pallas_call(
        functools.partial(_route_body, t_p, t_pad, n_p, cap),
        grid=(n // ROW_R,),
        in_specs=[pl.BlockSpec((N_EXPERTS, ROW_R), lambda i: (0, i))],
        out_specs=[blk(), blk(), pl.BlockSpec((N_EXPERTS, 128), lambda i: (0, 0))],
        out_shape=[jax.ShapeDtypeStruct((TOP_K, n), jnp.int32),
                   jax.ShapeDtypeStruct((TOP_K, n), F32), jax.ShapeDtypeStruct((N_EXPERTS, 128), jnp.int32)],
        scratch_shapes=[pltpu.VMEM((ROW_R, ROW_R + 128), BF16), pltpu.VMEM((N_EXPERTS, 128), F32)],
        compiler_params=_cparams(("arbitrary",)),
        name="route",
    )(logits_t)


def _slot_table(dest, n_blocks):
    return dest.T.reshape(n_blocks, TB // 8, 8 * TOP_K)


def _slot_of(dest_ref, i, u, k):
    return dest_ref[0, i, u * TOP_K + k]


SLOT_BLOCK = (1, TB // 8, 8 * TOP_K)


def _dispatch_body(dest_ref, xp_ref, xs_hbm, sem):
    def row_copy(i, u, slot):
        return pltpu.make_async_copy(xp_ref.at[i, pl.ds(u, 1), :], xs_hbm.at[pl.ds(slot, 1), :], sem)

    def issue(i, c):
        for u in range(8):
            for k in range(TOP_K):
                row_copy(i, u, _slot_of(dest_ref, i, u, k)).start(priority=k % 2)
        return c

    def drain(i, c):
        for _ in range(8 * TOP_K):
            row_copy(0, 0, 0).wait()
        return c

    for i in range(TB // 8):
        issue(i, 0)
    lax.fori_loop(0, TB // 8, drain, 0)


def _dispatch(dest_tab, xn_packed, n_slots):
    n, width = xn_packed.shape
    return pl.pallas_call(
        _dispatch_body,
        grid=(n // TB,),
        in_specs=[
            pl.BlockSpec(SLOT_BLOCK, lambda i: (i, 0, 0), memory_space=pltpu.SMEM),
            pl.BlockSpec((TB // 8, 8, width), lambda i: (i, 0, 0)),
        ],
        out_specs=pl.BlockSpec(memory_space=pl.ANY),
        out_shape=jax.ShapeDtypeStruct((n_slots, width), jnp.uint32),
        scratch_shapes=[pltpu.SemaphoreType.DMA(())],
        compiler_params=_cparams(("arbitrary",)),
        name="dispatch",
    )(dest_tab, xn_packed.reshape(n // 8, 8, width))


def _moe_body(ie_ref, r0_ref, nr_ref, nlive_ref, xs_hbm, wg_ref, wl_ref, bg_ref, bl_ref, wd_ref, bd_ref,
              ys_hbm, xraw, xb16, yacc, pend_s, xsem, ysem):
    it = pl.program_id(0)
    j = pl.program_id(1)
    n_live = nlive_ref[0]
    n_f = pl.num_programs(1)

    def chunks(item):
        return (nr_ref[item] + CH - 1) // CH

    def x_copy(item, c):
        src = pl.multiple_of(r0_ref[item] + c * CH, CH)
        return pltpu.make_async_copy(xs_hbm.at[pl.ds(src, CH), :],
                                     xraw.at[pl.ds(pl.multiple_of(c * CH, CH), CH), :], xsem)

    def y_copy(item, c):
        dst = pl.multiple_of(r0_ref[item] + c * CH, CH)
        return pltpu.make_async_copy(yacc.at[pl.ds(pl.multiple_of(c * CH, CH), CH), :],
                                     ys_hbm.at[pl.ds(dst, CH), :], ysem)

    def for_chunks(n, fn):
        def body(c, carry):
            fn(c)
            return carry
        lax.fori_loop(0, n, body, 0)

    @pl.when((it == 0) & (j == 0))
    def _():
        pend_s[0] = 0
        for_chunks(chunks(0), lambda c: x_copy(0, c).start())

    @pl.when((it < n_live) & (j == 0))
    def _():
        nrows = nr_ref[it]
        for_chunks(chunks(it), lambda c: x_copy(it, c).wait())

        def unpack(c):
            rows = pl.ds(pl.multiple_of(c * CH, CH), CH)
            p = xraw[rows, :]
            rid = c * CH + lax.broadcasted_iota(jnp.int32, (CH, 1), 0)
            live = rid < nrows
            lo = lax.bitcast_convert_type(p << 16, F32)
            hi = lax.bitcast_convert_type(p & jnp.uint32(0xFFFF0000), F32)
            xb16[rows, 0:D_MODEL // 2] = jnp.where(live, lo, 0.0).astype(BF16)
            xb16[rows, D_MODEL // 2:] = jnp.where(live, hi, 0.0).astype(BF16)

        for_chunks(chunks(it), unpack)

        @pl.when(it + 1 < n_live)
        def _():
            for_chunks(chunks(it + 1), lambda c: x_copy(it + 1, c).start())

    @pl.when(it < n_live)
    def _():
        @pl.when(j == 0)
        def _():
            for_chunks(pend_s[0], lambda c: y_copy(it, 0).wait())
            pend_s[0] = 0

        def ffn_rows(r0, n):
            rows = pl.ds(r0, n)
            x = xb16[rows, :]
            hg = jnp.dot(x, wg_ref[0].astype(BF16), preferred_element_type=F32) + bg_ref[0]
            hl = jnp.dot(x, wl_ref[0].astype(BF16), preferred_element_type=F32) + bl_ref[0]
            hg = jnp.minimum(hg, SWIGLU_LIMIT)
            hl = jnp.clip(hl, -SWIGLU_LIMIT, SWIGLU_LIMIT)
            a = hg * jax.nn.sigmoid(SWIGLU_ALPHA * hg) * (hl + 1.0)
            part = jnp.dot(a.astype(BF16), wd_ref[0].astype(BF16), preferred_element_type=F32)
            yacc[rows, :] = jnp.where(j == 0, bd_ref[0], yacc[rows, :]) + part

            @pl.when(j == n_f - 1)
            def _():
                for cc in range(n // CH):
                    y_copy(it, r0 // CH + cc).start()

        n_ch = chunks(it)
        n_full = n_ch // (SUB // CH)
        tail = n_ch - n_full * (SUB // CH)

        def full(s, carry):
            ffn_rows(pl.multiple_of(s * SUB, CH), SUB)
            return carry

        lax.fori_loop(0, n_full, full, 0)
        for t in range(1, SUB // CH):
            @pl.when(tail == t)
            def _():
                ffn_rows(pl.multiple_of(n_full * SUB, CH), t * CH)

        @pl.when(j == n_f - 1)
        def _():
            pend_s[0] = n_ch

            @pl.when(it == n_live - 1)
            def _():
                for_chunks(n_ch, lambda c: y_copy(it, c).wait())
                pend_s[0] = 0


def _moe(xs, item_expert, item_row0, item_rows, n_live, w_up, b_up, w_down, b_down):
    max_items = item_expert.shape[0]
    n_f = D_FF // TF

    def tile(it, j, nl):
        return jnp.where(it < nl[0], j, n_f - 1)

    return pl.pallas_call(
        _moe_body,
        grid_spec=pltpu.PrefetchScalarGridSpec(
            num_scalar_prefetch=4,
            grid=(n_live[0], n_f),
            in_specs=[
                pl.BlockSpec(memory_space=pl.ANY),
                pl.BlockSpec((1, D_MODEL, TF), lambda it, j, ie, r0, nr, nl: (ie[it], 0, tile(it, j, nl))),
                pl.BlockSpec((1, D_MODEL, TF), lambda it, j, ie, r0, nr, nl: (ie[it], 0, n_f + tile(it, j, nl))),
                pl.BlockSpec((1, 1, TF), lambda it, j, ie, r0, nr, nl: (ie[it], 0, tile(it, j, nl))),
                pl.BlockSpec((1, 1, TF), lambda it, j, ie, r0, nr, nl: (ie[it], 0, n_f + tile(it, j, nl))),
                pl.BlockSpec((1, TF, D_MODEL), lambda it, j, ie, r0, nr, nl: (ie[it], tile(it, j, nl), 0)),
                pl.BlockSpec((1, 1, D_MODEL), lambda it, j, ie, r0, nr, nl: (ie[it], 0, 0)),
            ],
            out_specs=pl.BlockSpec(memory_space=pl.ANY),
            scratch_shapes=[
                pltpu.VMEM((TM, D_MODEL // 2), jnp.uint32),
                pltpu.VMEM((TM, D_MODEL), BF16),
                pltpu.VMEM((TM, D_MODEL), F32),
                pltpu.SMEM((1,), jnp.int32),
                pltpu.SemaphoreType.DMA(()),
                pltpu.SemaphoreType.DMA(()),
            ],
        ),
        out_shape=jax.ShapeDtypeStruct((xs.shape[0], D_MODEL), F32),
        compiler_params=_cparams(("arbitrary", "arbitrary")),
        name="moe",
    )(item_expert, item_row0, item_rows, n_live, xs, w_up, w_up,
      b_up.reshape(N_EXPERTS, 1, 2 * D_FF), b_up.reshape(N_EXPERTS, 1, 2 * D_FF), w_down,
      b_down.reshape(N_EXPERTS, 1, D_MODEL))


def _combine_body(blocks_per_batch, n_prompt_blocks, t_pad, n_p, dest_ref, dest_next_ref, ys_hbm, h_hbm,
                  gt_ref, gf_ref, yp_ref, ysm_ref, gbuf, hbuf, gsem, hsem):
    i = pl.program_id(0)
    n_blocks = pl.num_programs(0)
    cur = i % 2

    def h_copy(blk, slot):
        row = jnp.where(blk < n_prompt_blocks,
                        (blk // blocks_per_batch) * t_pad + N_META + (blk % blocks_per_batch) * TB, n_p)
        return pltpu.make_async_copy(h_hbm.at[pl.ds(pl.multiple_of(row, 8), TB), :], hbuf.at[slot],
                                     hsem.at[slot])

    def row_copy(slot, buf, k, c, u):
        return pltpu.make_async_copy(ys_hbm.at[pl.ds(slot, 1), :], gbuf.at[buf, k, c, pl.ds(u, 1), :],
                                     gsem.at[buf])

    def request(table_ref, blk, buf):
        h_copy(blk, buf).start()

        def issue(c, carry):
            for u in range(8):
                for k in range(TOP_K):
                    row_copy(_slot_of(table_ref, c, u, k), buf, k, c, u).start(priority=k % 2)
            return carry

        for c in range(TB // 8):
            issue(c, 0)

    def drain(c, carry):
        for _ in range(8 * TOP_K):
            row_copy(0, cur, 0, 0, 0).wait()
        return carry

    @pl.when(i == 0)
    def _():
        request(dest_ref, 0, 0)

    for nxt in range(2):
        @pl.when((i + 1 < n_blocks) & (cur == 1 - nxt))
        def _():
            request(dest_next_ref, i + 1, nxt)

    h_copy(i, cur).wait()
    lax.fori_loop(0, TB // 8, drain, 0)
    y = hbuf[cur]
    for k in range(TOP_K):
        y = y + gbuf[cur, k].reshape(TB, D_MODEL) * gt_ref[:, k:k + 1]
    ms = jnp.mean(y * y, axis=-1, keepdims=True)
    y = y * lax.rsqrt(ms + EPS) * gf_ref[...]

    @pl.when(i < n_prompt_blocks)
    def _():
        yp_ref[...] = y

    @pl.when(i >= n_prompt_blocks)
    def _():
        ysm_ref[...] = y


def _combine(dest_blk, ys, h_all, gates_t, g_final, blocks_per_batch, n_prompt_blocks, t_pad, n_p):
    return pl.pallas_call(
        functools.partial(_combine_body, blocks_per_batch, n_prompt_blocks, t_pad, n_p),
        grid=(n_prompt_blocks + 1,),
        in_specs=[
            pl.BlockSpec(SLOT_BLOCK, lambda i: (i, 0, 0), memory_space=pltpu.SMEM),
            pl.BlockSpec(SLOT_BLOCK, lambda i: (jnp.minimum(i + 1, n_prompt_blocks), 0, 0),
                         memory_space=pltpu.SMEM),
            pl.BlockSpec(memory_space=pl.ANY),
            pl.BlockSpec(memory_space=pl.ANY),
            pl.BlockSpec((TB, TOP_K), lambda i: (i, 0)),
            pl.BlockSpec((1, D_MODEL), lambda i: (0, 0)),
        ],
        out_specs=[
            pl.BlockSpec((TB, D_MODEL), lambda i: (jnp.minimum(i, n_prompt_blocks - 1), 0)),
            pl.BlockSpec((TB, D_MODEL), lambda i: (0, 0)),
        ],
        out_shape=[jax.ShapeDtypeStruct((n_prompt_blocks * TB, D_MODEL), F32),
                   jax.ShapeDtypeStruct((TB, D_MODEL), F32)],
        scratch_shapes=[pltpu.VMEM((2, TOP_K, TB // 8, 8, D_MODEL), F32), pltpu.VMEM((2, TB, D_MODEL), F32),
                        pltpu.SemaphoreType.DMA((2,)), pltpu.SemaphoreType.DMA((2,))],
        compiler_params=_cparams(("arbitrary",)),
        name="combine",
    )(dest_blk, dest_blk, ys, h_all, gates_t, g_final.reshape(1, D_MODEL))


def _work_items(counts, cap, max_items):
    items_per = (counts + TM - 1) // TM
    item_end = jnp.cumsum(items_per)
    item_start = item_end - items_per
    n_live = item_end[-1]
    it = jnp.arange(max_items, dtype=jnp.int32)
    it_live = jnp.minimum(it, n_live - 1)
    expert = jnp.minimum(jnp.searchsorted(item_end, it_live, side="right"), N_EXPERTS - 1).astype(jnp.int32)
    part = it_live - item_start[expert]
    rows = jnp.clip(counts[expert] - part * TM, 0, TM)
    rows = jnp.where(it < n_live, rows, 0).astype(jnp.int32)
    row0 = (expert * cap + part * TM).astype(jnp.int32)
    return expert, row0, rows, n_live.reshape(1).astype(jnp.int32)


def kernel(x_prompt, x_sample, cache_k, cache_v, state_conv, meta_tokens, g_mix, w_in, conv_w, conv_b,
           conv_ln_g, conv_ln_b, g_attn_out, w_out, g_ffn, w_router, b_router, w_up, b_up, w_down,
           b_down, g_final):
    n_batch, seq, _ = x_prompt.shape
    n_streams, dec_seq, _ = x_sample.shape
    depth = w_in.shape[0]
    assert depth == 1 and n_streams * dec_seq == BLK
    t_p = N_META + seq
    t_pad = -(-t_p // BLK) * BLK
    n_p = n_batch * t_pad
    n_s = n_streams * dec_seq
    n_all = n_p + n_s
    assert n_all % ROW_A == 0 and n_all % ROW_E == 0 and n_all % ROW_R == 0 and n_all % TB == 0

    meta = meta_tokens.astype(F32)
    pad = jnp.zeros((t_pad - t_p, D_MODEL), F32)
    pieces = []
    for b in range(n_batch):
        pieces += [meta, x_prompt[b], pad]
    x_all = jnp.concatenate(pieces + [x_sample.reshape(n_s, D_MODEL)], axis=0)

    l = 0
    u_all, q_all, kvb, k_p, v_p, k_s, v_s = _inproj(
        x_all, g_mix[l], w_in[l].astype(BF16), n_batch, t_p, t_pad)
    c_all = _conv(u_all, state_conv[l], conv_w[l], conv_b[l], conv_ln_g[l], conv_ln_b[l],
                  t_pad // BLK, n_p // BLK)
    g_heads = g_attn_out[l].reshape(N_HEADS, HEAD_DIM)
    o_all = _attn_prompt(q_all, kvb, g_heads, n_batch, t_pad)
    o_all = _attn_sample(q_all, kvb, cache_k[l:l + 1], cache_v[l:l + 1], g_heads, o_all, n_p, dec_seq, BLK)
    h_all, xn_packed, logits_t = _outproj(x_all, c_all, o_all, w_out[l].astype(BF16), g_ffn[l],
                                          w_router[l].T.astype(BF16), b_router[l])

    n_valid = n_batch * t_p + n_s
    cap = -(-n_valid // CH) * CH
    n_slots = N_EXPERTS * cap + TB * TOP_K
    max_items = -(-(n_valid * TOP_K) // TM) + N_EXPERTS
    dest, gates, counts = _route(logits_t, t_p, t_pad, n_p, cap)
    xs = _dispatch(_slot_table(dest, n_all // TB), xn_packed, n_slots)
    item_expert, item_row0, item_rows, n_live = _work_items(counts[:, 0], cap, max_items)
    ys = _moe(xs, item_expert, item_row0, item_rows, n_live, w_up[l], b_up[l], w_down[l], b_down[l])

    def out_tokens(a):
        a_p = a[:, :n_p].reshape(-1, n_batch, t_pad)[:, :, N_META:t_p].reshape(-1, n_batch * seq)
        return jnp.concatenate([a_p, a[:, n_p:]], axis=1)

    assert seq % TB == 0 and n_s == TB
    n_out_blocks = n_batch * seq // TB
    dest_out = _slot_table(out_tokens(dest), n_out_blocks + 1)
    y_p, y_s = _combine(dest_out, ys, h_all, out_tokens(gates).T, g_final, seq // TB, n_out_blocks, t_pad, n_p)

    y_prompt = y_p.reshape(n_batch, seq, D_MODEL)
    y_sample = y_s.reshape(n_streams, dec_seq, D_MODEL)
    k_prompt = k_p[None]
    v_prompt = v_p[None]
    conv_prompt = jnp.stack([u_all[b * t_pad + t_p - (CONV_WIDTH - 1):b * t_pad + t_p]
                             for b in range(n_batch)])[None]
    k_sample = k_s.reshape(1, n_streams, dec_seq, N_HEADS, HEAD_DIM)
    v_sample = v_s.reshape(1, n_streams, dec_seq, N_HEADS, HEAD_DIM)
    u_s = u_all[n_p:].reshape(n_streams, dec_seq, D_CONV)
    conv_sample = jnp.concatenate([state_conv[l], u_s], axis=1)[:, -(CONV_WIDTH - 1):][None]
    return (y_prompt, y_sample, k_prompt, v_prompt, conv_prompt, k_sample, v_sample, conv_sample)
```

```python
import functools
import math

import jax
import jax.numpy as jnp
from jax import lax
from jax.experimental import pallas as pl
from jax.experimental.pallas import tpu as pltpu

F32 = jnp.float32
BF16 = jnp.bfloat16

D_MODEL = 2048
N_META = 16
D_CONV = 1024
N_HEADS = 8
HEAD_DIM = 128
D_ATTN = N_HEADS * HEAD_DIM
D_IN = 2 * D_CONV + 3 * D_ATTN
CONV_WIDTH = 31
N_EXPERTS = 32
TOP_K = 4
D_FF = D_MODEL
SWIGLU_ALPHA = 1.702
SWIGLU_LIMIT = 7.0
EPS = 1e-5

BLK = 256
HPS = 8
HALO = 32
COL = 512
ROW_A = 640
ROW_E = 640
ROW_R = 640
TM = 2048
CH = 128
SUB = 6 * CH
TF = 256
TB = 256
LOG_ZERO = -104.0
VMEM_LIMIT = 56 * 1024 * 1024


def _cparams(sem, vmem=VMEM_LIMIT):
    return pltpu.CompilerParams(dimension_semantics=sem, vmem_limit_bytes=vmem)


def _inproj_body(t_p, t_pad, n_p, x_ref, g_ref, wa_ref, wg_ref, wq_ref, w1_ref, w2_ref,
                 u_ref, q_ref, kvb_ref, kp_hbm, vp_hbm, ks_hbm, vs_hbm, xn_s, kv4_s, ksem, vsem):
    i = pl.program_id(0)
    j = pl.program_id(1)
    heads_per_tile = COL // HEAD_DIM
    k4_s, v4_s = kv4_s.at[0], kv4_s.at[1]

    def chunk_copies(tile, src, dst_p, dst_s, sem):
        out = []
        per_batch, full, rem = t_pad // CH, t_p // CH, t_p % CH
        for c in range(ROW_A // CH):
            g = tile * (ROW_A // CH) + c
            is_p = g < n_p // CH
            b, wi = g // per_batch, g % per_batch
            out.append((is_p & (wi < full), pltpu.make_async_copy(
                src.at[pl.ds(c * CH, CH)], dst_p.at[b, pl.ds(pl.multiple_of(wi * CH, CH), CH)], sem)))
            if rem:
                out.append((is_p & (wi == full), pltpu.make_async_copy(
                    src.at[pl.ds(c * CH, rem)], dst_p.at[b, pl.ds(full * CH, rem)], sem)))
            out.append((jnp.logical_not(is_p), pltpu.make_async_copy(
                src.at[pl.ds(c * CH, CH)],
                dst_s.at[pl.ds(pl.multiple_of((g - n_p // CH) * CH, CH), CH)], sem)))
        return out

    def start_all(copies):
        for cond, cp in copies:
            @pl.when(cond)
            def _():
                cp.start()

    def wait_all(copies):
        for cond, cp in copies:
            @pl.when(cond)
            def _():
                cp.wait()

    def store_heads(dst, p, first_head):
        for hh in range(heads_per_tile):
            dst[:, first_head + hh, :] = p[:, hh * HEAD_DIM:(hh + 1) * HEAD_DIM]

    @pl.when(j == 0)
    def _():
        x = x_ref[...]
        ms = jnp.mean(x * x, axis=-1, keepdims=True)
        xn_s[...] = (x * lax.rsqrt(ms + EPS) * g_ref[...]).astype(BF16)

        @pl.when(i > 0)
        def _():
            wait_all(chunk_copies(i - 1, k4_s, kp_hbm, ks_hbm, ksem))

    @pl.when((j == 1) & (i > 0))
    def _():
        wait_all(chunk_copies(i - 1, v4_s, vp_hbm, vs_hbm, vsem))

    xn = xn_s[...]
    proj = lambda w_ref: jnp.dot(xn, w_ref[...], preferred_element_type=F32)
    u_ref[...] = proj(wa_ref) * jax.nn.sigmoid(proj(wg_ref))
    q_ref[...] = (proj(wq_ref) * (1.0 / math.sqrt(HEAD_DIM))).astype(BF16)
    for t, w_ref in enumerate((w1_ref, w2_ref)):
        p = proj(w_ref)
        kvb_ref[0, :, t * COL:(t + 1) * COL] = p.astype(BF16)
        store_heads(kv4_s.at[j], p, t * heads_per_tile)

    @pl.when(j == 0)
    def _():
        start_all(chunk_copies(i, k4_s, kp_hbm, ks_hbm, ksem))

    @pl.when(j == 1)
    def _():
        start_all(chunk_copies(i, v4_s, vp_hbm, vs_hbm, vsem))

        @pl.when(i == pl.num_programs(0) - 1)
        def _():
            wait_all(chunk_copies(i, k4_s, kp_hbm, ks_hbm, ksem))
            wait_all(chunk_copies(i, v4_s, vp_hbm, vs_hbm, vsem))


def _inproj(x_all, g_mix, w_in_bf, n_batch, t_p, t_pad):
    n = x_all.shape[0]
    n_p = n_batch * t_pad
    assert D_IN == 10 * COL and COL == 4 * HEAD_DIM and ROW_A % CH == 0 and t_pad % CH == 0

    def w_spec(first, stride):
        return pl.BlockSpec((D_MODEL, COL), lambda i, j: (0, first + stride * j))

    wide = lambda dt: jax.ShapeDtypeStruct((n, D_CONV), dt)
    cache_p = jax.ShapeDtypeStruct((n_batch, t_p, N_HEADS, HEAD_DIM), F32)
    cache_s = jax.ShapeDtypeStruct((n - n_p, N_HEADS, HEAD_DIM), F32)
    any_spec = pl.BlockSpec(memory_space=pl.ANY)
    half_spec = pl.BlockSpec((ROW_A, COL), lambda i, j: (i, j))
    return pl.pallas_call(
        functools.partial(_inproj_body, t_p, t_pad, n_p),
        grid=(n // ROW_A, 2),
        in_specs=[
            pl.BlockSpec((ROW_A, D_MODEL), lambda i, j: (i, 0)),
            pl.BlockSpec((1, D_MODEL), lambda i, j: (0, 0)),
            w_spec(0, 1), w_spec(2, 1), w_spec(4, 1), w_spec(6, 2), w_spec(7, 2),
        ],
        out_specs=[half_spec, half_spec, pl.BlockSpec((1, ROW_A, D_ATTN), lambda i, j: (j, i, 0)),
                   any_spec, any_spec, any_spec, any_spec],
        out_shape=[wide(F32), wide(BF16), jax.ShapeDtypeStruct((2, n, D_ATTN), BF16),
                   cache_p, cache_p, cache_s, cache_s],
        scratch_shapes=[pltpu.VMEM((ROW_A, D_MODEL), BF16),
                        pltpu.VMEM((2, ROW_A, N_HEADS, HEAD_DIM), F32),
                        pltpu.SemaphoreType.DMA(()), pltpu.SemaphoreType.DMA(())],
        compiler_params=_cparams(("arbitrary", "arbitrary")),
        name="inproj",
    )(x_all, g_mix.reshape(1, D_MODEL), w_in_bf, w_in_bf, w_in_bf, w_in_bf, w_in_bf)


def _conv_taps(ext_ref, w_ref, b_ref, dst_ref, src_row0, dst_row0, nrows):
    def chunk(cc, carry):
        lanes = pl.ds(pl.multiple_of(cc * 128, 128), 128)
        acc = jnp.broadcast_to(b_ref[:, lanes], (nrows, 128))
        for j in range(CONV_WIDTH):
            acc = acc + ext_ref[pl.ds(src_row0 + j, nrows), lanes] * w_ref[pl.ds(j, 1), lanes]
        dst_ref[pl.ds(dst_row0, nrows), lanes] = acc
        return carry

    lax.fori_loop(0, D_CONV // 128, chunk, 0)


def _conv_block(ext_ref, sh_ref, w_ref, b_ref, dst_ref, off):
    span = HALO + BLK

    def chunk(cc, carry):
        lanes = pl.ds(pl.multiple_of(cc * 128, 128), 128)
        for r in range(8):
            n = span - (8 if r else 0)
            sh_ref[r, 0:n, :] = ext_ref[pl.ds(r, n), lanes]
        for rc in range(BLK // 64):
            acc = jnp.broadcast_to(b_ref[:, lanes], (64, 128))
            for j in range(CONV_WIDTH):
                r = (off + j) % 8
                acc = acc + sh_ref[r, pl.ds(off + j - r + rc * 64, 64), :] * w_ref[pl.ds(j, 1), lanes]
            dst_ref[pl.ds(rc * 64, 64), lanes] = acc
        return carry

    lax.fori_loop(0, D_CONV // 128, chunk, 0)


def _conv_body(blocks_per_batch, n_prompt_blocks, u_ref, prev_ref, state_ref, w_ref, b_ref,
               lg_ref, lb_ref, c_ref, ext_s, conv_s, sh_s):
    i = pl.program_id(0)
    off = HALO - (CONV_WIDTH - 1)

    @pl.when(i < n_prompt_blocks)
    def _():
        first = (i % blocks_per_batch) == 0
        ext_s[0:HALO, :] = jnp.where(first, 0.0, prev_ref[...])
        ext_s[HALO:HALO + BLK, :] = u_ref[...]
        _conv_block(ext_s, sh_s, w_ref, b_ref, conv_s, off)

    @pl.when(i >= n_prompt_blocks)
    def _():
        def stream(s, carry):
            r0 = pl.multiple_of(s * 16, 16)
            ext_s[off:HALO, :] = state_ref[s]
            ext_s[HALO:HALO + 16, :] = u_ref[pl.ds(r0, 16), :]
            _conv_taps(ext_s, w_ref, b_ref, conv_s, off, r0, 16)
            return carry

        lax.fori_loop(0, BLK // 16, stream, 0)

    x = conv_s[...]
    mu = jnp.mean(x, axis=-1, keepdims=True)
    xc = x - mu
    var = jnp.mean(xc * xc, axis=-1, keepdims=True)
    y = xc * lax.rsqrt(var + EPS) * lg_ref[...] + lb_ref[...]
    c_ref[...] = (y * jax.nn.sigmoid(y)).astype(BF16)


def _conv(u_all, state_conv, conv_w, conv_b, ln_g, ln_b, blocks_per_batch, n_prompt_blocks):
    n = u_all.shape[0]
    row = lambda a: a.reshape(1, D_CONV)
    const = lambda shape: pl.BlockSpec(shape, lambda i: (0,) * len(shape))
    return pl.pallas_call(
        functools.partial(_conv_body, blocks_per_batch, n_prompt_blocks),
        grid=(n // BLK,),
        in_specs=[
            pl.BlockSpec((BLK, D_CONV), lambda i: (i, 0)),
            pl.BlockSpec((HALO, D_CONV), lambda i: (jnp.maximum(i * (BLK // HALO) - 1, 0), 0)),
            const(state_conv.shape),
            const((CONV_WIDTH, D_CONV)),
            const((1, D_CONV)), const((1, D_CONV)), const((1, D_CONV)),
        ],
        out_specs=pl.BlockSpec((BLK, D_CONV), lambda i: (i, 0)),
        out_shape=jax.ShapeDtypeStruct((n, D_CONV), BF16),
        scratch_shapes=[pltpu.VMEM((HALO + BLK, D_CONV), F32), pltpu.VMEM((BLK, D_CONV), F32),
                        pltpu.VMEM((8, HALO + BLK, 128), F32)],
        compiler_params=_cparams(("arbitrary",)),
        name="conv",
    )(u_all, u_all, state_conv, conv_w, row(conv_b), row(ln_g), row(ln_b))


def _suffix_matrix(n, extra):
    r = lax.broadcasted_iota(jnp.int32, (n, n + extra), 0)
    c = lax.broadcasted_iota(jnp.int32, (n, n + extra), 1)
    return jnp.where((r > c) | (c >= n), 1.0, 0.0).astype(BF16)


def _split_dot(x, m):
    hi = x.astype(BF16)
    lo = (x - hi.astype(F32)).astype(BF16)
    return (jnp.dot(hi, m, preferred_element_type=F32) + jnp.dot(lo, m, preferred_element_type=F32))


def _log_keep(z):
    return -(jnp.maximum(z, 0.0) + jnp.log(1.0 + jnp.exp(-jnp.abs(z))))


def _attn_prompt_body(q_ref, k_ref, v_ref, g_ref, o_ref, m_s):
    hg = pl.program_id(1)
    qi = pl.program_id(2)

    @pl.when((pl.program_id(0) == 0) & (hg == 0) & (qi == 0))
    def _():
        m_s[...] = _suffix_matrix(BLK, 128)

    m = m_s[...]

    def tile(hh, kb, carry, diagonal):
        lanes = slice(hh * HEAD_DIM, (hh + 1) * HEAD_DIM)
        rows = pl.ds(pl.multiple_of(kb * BLK, BLK), BLK)
        z = lax.dot_general(q_ref[:, lanes], k_ref[rows, lanes], (((1,), (1,)), ((), ())),
                            preferred_element_type=F32)
        lk = _log_keep(z)
        if diagonal:
            keep = (lax.broadcasted_iota(jnp.int32, (BLK, BLK), 1)
                    < lax.broadcasted_iota(jnp.int32, (BLK, BLK), 0))
            lk = jnp.where(keep, lk, 0.0)
        cs = _split_dot(lk, m)
        after = cs[:, :BLK]
        if carry is not None:
            after = after + jnp.concatenate([carry, carry], axis=1)
        w = jnp.exp(z + lk + after)
        if diagonal:
            w = jnp.where(keep, w, 0.0)
        pv = jnp.dot(w.astype(BF16), v_ref[rows, lanes], preferred_element_type=F32)
        return pv, cs[:, BLK:]

    first = [tile(hh, qi, None, True) for hh in range(HPS)]

    def cond(st):
        kb, _, carries = st
        live = jnp.max(carries[0])
        for c in carries[1:]:
            live = jnp.maximum(live, jnp.max(c))
        return (kb >= 0) & (live > LOG_ZERO)

    def step(st):
        kb, accs, carries = st
        out = [tile(hh, kb, carries[hh], False) for hh in range(HPS)]
        return (kb - 1, tuple(a + o[0] for a, o in zip(accs, out)),
                tuple(c + o[1] for c, o in zip(carries, out)))

    _, accs, _ = lax.while_loop(cond, step, (qi - 1, tuple(f[0] for f in first), tuple(f[1] for f in first)))
    for hh in range(HPS):
        acc = accs[hh]
        ms = jnp.mean(acc * acc, axis=-1, keepdims=True)
        o_ref[:, hh * HEAD_DIM:(hh + 1) * HEAD_DIM] = (
            acc * lax.rsqrt(ms + EPS) * g_ref[pl.ds(hg * HPS + hh, 1), :]).astype(BF16)


def _attn_prompt(q_all, kvb, g_heads, n_batch, t_pad):
    nq = t_pad // BLK
    wide = HPS * HEAD_DIM
    return pl.pallas_call(
        _attn_prompt_body,
        grid=(n_batch, N_HEADS // HPS, nq),
        in_specs=[
            pl.BlockSpec((BLK, wide), lambda b, h, i: (b * nq + i, h)),
            pl.BlockSpec((None, t_pad, wide), lambda b, h, i: (0, b, h)),
            pl.BlockSpec((None, t_pad, wide), lambda b, h, i: (1, b, h)),
            pl.BlockSpec((N_HEADS, HEAD_DIM), lambda b, h, i: (0, 0)),
        ],
        out_specs=pl.BlockSpec((BLK, wide), lambda b, h, i: (b * nq + i, h)),
        out_shape=jax.ShapeDtypeStruct((q_all.shape[0], D_ATTN), BF16),
        scratch_shapes=[pltpu.VMEM((BLK, BLK + 128), BF16)],
        compiler_params=_cparams(("arbitrary", "arbitrary", "arbitrary")),
        name="attn_prompt",
    )(q_all, kvb, kvb, g_heads)


def _attn_sample_body(n_kb, kblk, q_ref, kn_ref, vn_ref, kc_hbm, vc_hbm, g_ref, o_in_hbm, o_ref,
                      kbuf, vbuf, qbd_s, acc_s, carry_s, m_s, mn_s, ksem, vsem):
    s = pl.program_id(0)
    n_streams = pl.num_programs(0)
    dq = q_ref.shape[0]
    last = n_kb - 1

    def cache_copies(stream, jb, slot):
        rows = pl.ds(pl.multiple_of(jb * kblk, kblk), kblk)
        return (pltpu.make_async_copy(kc_hbm.at[0, stream, rows], kbuf.at[slot], ksem.at[slot]),
                pltpu.make_async_copy(vc_hbm.at[0, stream, rows], vbuf.at[slot], vsem.at[slot]))

    def start(stream, jb, slot):
        for cp in cache_copies(stream, jb, slot):
            cp.start()

    def wait(stream, jb, slot):
        for cp in cache_copies(stream, jb, slot):
            cp.wait()

    def cat_heads(buf, slot):
        return jnp.concatenate([buf[slot, :, hh, :] for hh in range(N_HEADS)], axis=1).astype(BF16)

    def scores(kcat):
        return jnp.dot(kcat, qbd_s[...], preferred_element_type=F32)

    def add_values(w, vcat):
        full = lax.dot_general(w.astype(BF16), vcat, (((0,), (0,)), ((), ())), preferred_element_type=F32)
        for hh in range(N_HEADS):
            acc_s[hh] += full[hh * dq:(hh + 1) * dq, hh * HEAD_DIM:(hh + 1) * HEAD_DIM]

    @pl.when(s == 0)
    def _():
        start(0, last, last % 2)
        r = lax.broadcasted_iota(jnp.int32, (kblk, kblk), 0)
        c = lax.broadcasted_iota(jnp.int32, (kblk, kblk), 1)
        m_s[...] = jnp.where(c > r, 1.0, 0.0).astype(BF16)
        mn_s[...] = m_s[0:dq, 0:dq]

    qrep = jnp.concatenate([q_ref[...].astype(F32)] * N_HEADS, axis=0)
    qt = qrep.T
    row_head = lax.broadcasted_iota(jnp.int32, (D_ATTN, 128), 0) // HEAD_DIM
    col_head = lax.broadcasted_iota(jnp.int32, (D_ATTN, 128), 1) // dq
    qbd_s[...] = jnp.where(row_head == col_head, qt, 0.0).astype(BF16)
    acc_s[...] = jnp.zeros_like(acc_s)
    z = scores(kn_ref[...])
    lk = _log_keep(z)
    key = lax.broadcasted_iota(jnp.int32, (dq, 128), 0)
    qry = lax.broadcasted_iota(jnp.int32, (dq, 128), 1) % dq
    keep = key < qry
    lk = jnp.where(keep, lk, 0.0)
    after = _split_dot_left(mn_s[...], lk)
    w = jnp.where(keep, jnp.exp(z + lk + after), 0.0)
    add_values(w, vn_ref[...])
    carry_s[...] = jnp.sum(lk, axis=0, keepdims=True)

    def cond(jb):
        return (jb >= 0) & (jnp.max(carry_s[...]) > LOG_ZERO)

    def step(jb):
        slot = jb % 2
        wait(s, jb, slot)

        @pl.when(jb > 0)
        def _():
            start(s, jb - 1, 1 - slot)

        z = scores(cat_heads(kbuf, slot))
        lk = _log_keep(z)
        after = _split_dot_left(m_s[...], lk) + carry_s[...]
        w = jnp.exp(z + lk + after)
        add_values(w, cat_heads(vbuf, slot))
        carry_s[...] += jnp.sum(lk, axis=0, keepdims=True)
        return jb - 1

    jb_end = lax.while_loop(cond, step, last)

    @pl.when(jb_end >= 0)
    def _():
        wait(s, jb_end, jb_end % 2)

    @pl.when(s + 1 < n_streams)
    def _():
        start(s + 1, last, last % 2)

    for hh in range(N_HEADS):
        a = acc_s[hh]
        ms = jnp.mean(a * a, axis=-1, keepdims=True)
        o_ref[:, hh * HEAD_DIM:(hh + 1) * HEAD_DIM] = (
            a * lax.rsqrt(ms + EPS) * g_ref[pl.ds(hh, 1), :]).astype(BF16)


def _split_dot_left(m, x):
    hi = x.astype(BF16)
    lo = (x - hi.astype(F32)).astype(BF16)
    return (jnp.dot(m, hi, preferred_element_type=F32) + jnp.dot(m, lo, preferred_element_type=F32))


def _attn_sample(q_all, kvb, cache_k, cache_v, g_heads, o_all, row0, dec_seq, kblk):
    n_streams = cache_k.shape[1]
    past = cache_k.shape[2]
    n_kb = past // kblk
    assert n_kb * kblk == past and row0 % dec_seq == 0
    row_spec = pl.BlockSpec((dec_seq, D_ATTN), lambda s: (row0 // dec_seq + s, 0))
    kv_spec = lambda which: pl.BlockSpec((None, dec_seq, D_ATTN), lambda s: (which, row0 // dec_seq + s, 0))
    cache_buf = pltpu.VMEM((2, kblk, N_HEADS, HEAD_DIM), cache_k.dtype)
    any_spec = pl.BlockSpec(memory_space=pl.ANY)
    return pl.pallas_call(
        functools.partial(_attn_sample_body, n_kb, kblk),
        grid=(n_streams,),
        in_specs=[row_spec, kv_spec(0), kv_spec(1), any_spec, any_spec,
                  pl.BlockSpec((N_HEADS, HEAD_DIM), lambda s: (0, 0)), any_spec],
        out_specs=row_spec,
        out_shape=jax.ShapeDtypeStruct(o_all.shape, o_all.dtype),
        input_output_aliases={6: 0},
        scratch_shapes=[
            cache_buf, cache_buf,
            pltpu.VMEM((D_ATTN, 128), BF16),
            pltpu.VMEM((N_HEADS, dec_seq, HEAD_DIM), F32),
            pltpu.VMEM((1, 128), F32),
            pltpu.VMEM((kblk, kblk), BF16),
            pltpu.VMEM((dec_seq, dec_seq), BF16),
            pltpu.SemaphoreType.DMA((2,)), pltpu.SemaphoreType.DMA((2,)),
        ],
        compiler_params=_cparams(("arbitrary",)),
        name="attn_sample",
    )(q_all, kvb, kvb, cache_k, cache_v, g_heads, o_all)


def _outproj_body(x_ref, c_ref, o_ref, w_ref, g_ref, wr_ref, br_ref, h_ref, xn_ref, lg_ref):
    h = (x_ref[...]
         + jnp.dot(c_ref[...], w_ref[0:D_CONV, :], preferred_element_type=F32)
         + jnp.dot(o_ref[...], w_ref[D_CONV:, :], preferred_element_type=F32))
    h_ref[...] = h
    ms = jnp.mean(h * h, axis=-1, keepdims=True)
    xn = (h * lax.rsqrt(ms + EPS) * g_ref[...]).astype(BF16)
    bits = lax.bitcast_convert_type(xn.astype(F32), jnp.uint32)
    xn_ref[...] = (bits[:, :D_MODEL // 2] >> 16) | bits[:, D_MODEL // 2:]
    lg_ref[...] = lax.dot_general(wr_ref[...], xn, (((1,), (1,)), ((), ())),
                                  preferred_element_type=F32) + br_ref[...]


def _outproj(x_all, c_all, o_all, w_out_bf, g_ffn, w_router_t_bf, b_router):
    n = x_all.shape[0]
    const = lambda shape: pl.BlockSpec(shape, lambda i: (0,) * len(shape))
    return pl.pallas_call(
        _outproj_body,
        grid=(n // ROW_E,),
        in_specs=[
            pl.BlockSpec((ROW_E, D_MODEL), lambda i: (i, 0)),
            pl.BlockSpec((ROW_E, D_CONV), lambda i: (i, 0)),
            pl.BlockSpec((ROW_E, D_ATTN), lambda i: (i, 0)),
            const((D_MODEL, D_MODEL)),
            const((1, D_MODEL)),
            const((N_EXPERTS, D_MODEL)),
            const((N_EXPERTS, 1)),
        ],
        out_specs=[
            pl.BlockSpec((ROW_E, D_MODEL), lambda i: (i, 0)),
            pl.BlockSpec((ROW_E, D_MODEL // 2), lambda i: (i, 0)),
            pl.BlockSpec((N_EXPERTS, ROW_E), lambda i: (0, i)),
        ],
        out_shape=[jax.ShapeDtypeStruct((n, D_MODEL), F32), jax.ShapeDtypeStruct((n, D_MODEL // 2), jnp.uint32),
                   jax.ShapeDtypeStruct((N_EXPERTS, n), F32)],
        compiler_params=_cparams(("arbitrary",)),
        name="outproj",
    )(x_all, c_all, o_all, w_out_bf, g_ffn.reshape(1, D_MODEL), w_router_t_bf,
      b_router.reshape(N_EXPERTS, 1))


def _route_body(t_p, t_pad, n_p, cap, lg_ref, d_ref, gt_ref, cnt_ref, tri_s, run_s):
    i = pl.program_id(0)
    tb = lg_ref.shape[1]

    @pl.when(i == 0)
    def _():
        r = lax.broadcasted_iota(jnp.int32, (tb, tb + 128), 0)
        c = lax.broadcasted_iota(jnp.int32, (tb, tb + 128), 1)
        tri_s[...] = jnp.where((r < c) | (c >= tb), 1.0, 0.0).astype(BF16)
        run_s[...] = jnp.zeros_like(run_s)

    tok = i * tb + lax.broadcasted_iota(jnp.int32, (1, tb), 1)
    valid = ((tok % t_pad) < t_p) | (tok >= n_p)
    eid = lax.broadcasted_iota(jnp.int32, (N_EXPERTS, tb), 0)
    lg = lg_ref[...]
    sel = jnp.zeros((N_EXPERTS, tb), F32)
    hot, top = [], []
    for _ in range(TOP_K):
        mx = jnp.max(lg, axis=0, keepdims=True)
        idx = jnp.min(jnp.where(lg == mx, eid, N_EXPERTS), axis=0, keepdims=True)
        one = eid == idx
        lg = jnp.where(one, -jnp.inf, lg)
        hot.append(one)
        top.append(mx)
        sel = sel + jnp.where(one & valid, 1.0, 0.0)
    ex = [jnp.exp(t - top[0]) for t in top]
    den = ex[0] + ex[1] + ex[2] + ex[3]
    cs = jnp.dot(sel.astype(BF16), tri_s[...], preferred_element_type=F32)
    run = run_s[...]
    slot = (cs[:, :tb] + jnp.concatenate([run] * (tb // 128), axis=1)
            + (eid * cap).astype(F32))
    for k in range(TOP_K):
        mine = jnp.sum(jnp.where(hot[k], slot, 0.0), axis=0, keepdims=True).astype(jnp.int32)
        trash = N_EXPERTS * cap + (tok % TB) * TOP_K + k
        d_ref[pl.ds(k, 1), :] = jnp.where(valid, mine, trash)
        gt_ref[pl.ds(k, 1), :] = ex[k] / den
    run = run + cs[:, tb:]
    run_s[...] = run
    cnt_ref[...] = run.astype(jnp.int32)


def _route(logits_t, t_p, t_pad, n_p, cap):
    n = logits_t.shape[1]
    blk = lambda: pl.BlockSpec((TOP_K, ROW_R), lambda i: (0, i))
    return pl.pallas_call(
        functools.partial(_route_body, t_p, t_pad, n_p, cap),
        grid=(n // ROW_R,),
        in_specs=[pl.BlockSpec((N_EXPERTS, ROW_R), lambda i: (0, i))],
        out_specs=[blk(), blk(), pl.BlockSpec((N_EXPERTS, 128), lambda i: (0, 0))],
        out_shape=[jax.ShapeDtypeStruct((TOP_K, n), jnp.int32),
                   jax.ShapeDtypeStruct((TOP_K, n), F32), jax.ShapeDtypeStruct((N_EXPERTS, 128), jnp.int32)],
        scratch_shapes=[pltpu.VMEM((ROW_R, ROW_R + 128), BF16), pltpu.VMEM((N_EXPERTS, 128), F32)],
        compiler_params=_cparams(("arbitrary",)),
        name="route",
    )(logits_t)


def _slot_table(dest, n_blocks):
    return dest.T.reshape(n_blocks, TB // 8, 8 * TOP_K)


def _slot_of(dest_ref, i, u, k):
    return dest_ref[0, i, u * TOP_K + k]


SLOT_BLOCK = (1, TB // 8, 8 * TOP_K)


def _dispatch_body(dest_ref, xp_ref, xs_hbm, sem):
    def row_copy(i, u, slot):
        return pltpu.make_async_copy(xp_ref.at[i, pl.ds(u, 1), :], xs_hbm.at[pl.ds(slot, 1), :], sem)

    def issue(i, c):
        for u in range(8):
            for k in range(TOP_K):
                row_copy(i, u, _slot_of(dest_ref, i, u, k)).start(priority=k % 2)
        return c

    def drain(i, c):
        for _ in range(8 * TOP_K):
            row_copy(0, 0, 0).wait()
        return c

    for i in range(TB // 8):
        issue(i, 0)
    lax.fori_loop(0, TB // 8, drain, 0)


def _dispatch(dest_tab, xn_packed, n_slots):
    n, width = xn_packed.shape
    return pl.pallas_call(
        _dispatch_body,
        grid=(n // TB,),
        in_specs=[
            pl.BlockSpec(SLOT_BLOCK, lambda i: (i, 0, 0), memory_space=pltpu.SMEM),
            pl.BlockSpec((TB // 8, 8, width), lambda i: (i, 0, 0)),
        ],
        out_specs=pl.BlockSpec(memory_space=pl.ANY),
        out_shape=jax.ShapeDtypeStruct((n_slots, width), jnp.uint32),
        scratch_shapes=[pltpu.SemaphoreType.DMA(())],
        compiler_params=_cparams(("arbitrary",)),
        name="dispatch",
    )(dest_tab, xn_packed.reshape(n // 8, 8, width))


def _moe_body(ie_ref, r0_ref, nr_ref, nlive_ref, xs_hbm, wup_hbm, wdn_hbm, bup_ref, bd_ref,
              ys_hbm, xraw, xb16, yacc, wg_buf, wl_buf, wd_buf, pend_s, xsem, ysem, wsem):
    it = pl.program_id(0)
    n_live = nlive_ref[0]
    n_f = D_FF // TF
    nrows = nr_ref[it]

    def chunks(item):
        return (nr_ref[item] + CH - 1) // CH

    def x_copy(item, c):
        src = pl.multiple_of(r0_ref[item] + c * CH, CH)
        return pltpu.make_async_copy(xs_hbm.at[pl.ds(src, CH), :],
                                     xraw.at[pl.ds(pl.multiple_of(c * CH, CH), CH), :], xsem)

    def y_copy(item, c):
        dst = pl.multiple_of(r0_ref[item] + c * CH, CH)
        return pltpu.make_async_copy(yacc.at[pl.ds(pl.multiple_of(c * CH, CH), CH), :],
                                     ys_hbm.at[pl.ds(dst, CH), :], ysem)

    def w_copies(item, jj, slot):
        e = ie_ref[item]
        glu = pl.ds(pl.multiple_of(jj * TF, TF), TF)
        lin = pl.ds(pl.multiple_of(D_FF + jj * TF, TF), TF)
        return (pltpu.make_async_copy(wup_hbm.at[e, :, glu], wg_buf.at[slot], wsem.at[slot]),
                pltpu.make_async_copy(wup_hbm.at[e, :, lin], wl_buf.at[slot], wsem.at[slot]),
                pltpu.make_async_copy(wdn_hbm.at[e, glu, :], wd_buf.at[slot], wsem.at[slot]))

    def start_w(item, jj, slot):
        for cp in w_copies(item, jj, slot):
            cp.start()

    def for_chunks(n, fn):
        def body(c, carry):
            fn(c)
            return carry
        lax.fori_loop(0, n, body, 0)

    @pl.when(it == 0)
    def _():
        pend_s[0] = 0
        for_chunks(chunks(0), lambda c: x_copy(0, c).start())
        start_w(0, 0, 0)

    n_ch = chunks(it)
    for_chunks(n_ch, lambda c: x_copy(it, c).wait())

    def unpack(c):
        rows = pl.ds(pl.multiple_of(c * CH, CH), CH)
        p = xraw[rows, :]
        rid = c * CH + lax.broadcasted_iota(jnp.int32, (CH, 1), 0)
        live = rid < nrows
        lo = lax.bitcast_convert_type(p << 16, F32)
        hi = lax.bitcast_convert_type(p & jnp.uint32(0xFFFF0000), F32)
        xb16[rows, 0:D_MODEL // 2] = jnp.where(live, lo, 0.0).astype(BF16)
        xb16[rows, D_MODEL // 2:] = jnp.where(live, hi, 0.0).astype(BF16)

    for_chunks(n_ch, unpack)

    @pl.when(it + 1 < n_live)
    def _():
        for_chunks(chunks(it + 1), lambda c: x_copy(it + 1, c).start())

    for_chunks(pend_s[0], lambda c: y_copy(it, 0).wait())
    n_full = n_ch // (SUB // CH)
    tail = n_ch - n_full * (SUB // CH)

    def hidden_tile(j, carry):
        slot = j % 2
        for cp in w_copies(it, j, slot):
            cp.wait()

        @pl.when(j + 1 < n_f)
        def _():
            start_w(it, j + 1, 1 - slot)

        @pl.when((j + 1 == n_f) & (it + 1 < n_live))
        def _():
            start_w(it + 1, 0, 1 - slot)

        bg = bup_ref[0, :, pl.ds(pl.multiple_of(j * TF, TF), TF)]
        bl = bup_ref[0, :, pl.ds(pl.multiple_of(D_FF + j * TF, TF), TF)]

        def ffn_rows(r0, n):
            rows = pl.ds(r0, n)
            x = xb16[rows, :]
            hg = jnp.dot(x, wg_buf[slot].astype(BF16), preferred_element_type=F32) + bg
            hl = jnp.dot(x, wl_buf[slot].astype(BF16), preferred_element_type=F32) + bl
            hg = jnp.minimum(hg, SWIGLU_LIMIT)
            hl = jnp.clip(hl, -SWIGLU_LIMIT, SWIGLU_LIMIT)
            a = hg * jax.nn.sigmoid(SWIGLU_ALPHA * hg) * (hl + 1.0)
            part = jnp.dot(a.astype(BF16), wd_buf[slot].astype(BF16), preferred_element_type=F32)
            yacc[rows, :] = jnp.where(j == 0, bd_ref[0], yacc[rows, :]) + part

            @pl.when(j == n_f - 1)
            def _():
                for cc in range(n // CH):
                    y_copy(it, r0 // CH + cc).start()

        def full(s, c):
            ffn_rows(pl.multiple_of(s * SUB, CH), SUB)
            return c

        lax.fori_loop(0, n_full, full, 0)
        for t in range(1, SUB // CH):
            @pl.when(tail == t)
            def _():
                ffn_rows(pl.multiple_of(n_full * SUB, CH), t * CH)
        return carry

    lax.fori_loop(0, n_f, hidden_tile, 0)
    pend_s[0] = n_ch

    @pl.when(it == n_live - 1)
    def _():
        for_chunks(n_ch, lambda c: y_copy(it, c).wait())
        pend_s[0] = 0


def _moe(xs, item_expert, item_row0, item_rows, n_live, w_up, b_up, w_down, b_down):
    assert (D_FF // TF) % 2 == 0
    any_spec = pl.BlockSpec(memory_space=pl.ANY)
    return pl.pallas_call(
        _moe_body,
        grid_spec=pltpu.PrefetchScalarGridSpec(
            num_scalar_prefetch=4,
            grid=(n_live[0],),
            in_specs=[
                any_spec, any_spec, any_spec,
                pl.BlockSpec((1, 1, 2 * D_FF), lambda it, ie, r0, nr, nl: (ie[it], 0, 0)),
                pl.BlockSpec((1, 1, D_MODEL), lambda it, ie, r0, nr, nl: (ie[it], 0, 0)),
            ],
            out_specs=any_spec,
            scratch_shapes=[
                pltpu.VMEM((TM, D_MODEL // 2), jnp.uint32),
                pltpu.VMEM((TM, D_MODEL), BF16),
                pltpu.VMEM((TM, D_MODEL), F32),
                pltpu.VMEM((2, D_MODEL, TF), w_up.dtype),
                pltpu.VMEM((2, D_MODEL, TF), w_up.dtype),
                pltpu.VMEM((2, TF, D_MODEL), w_down.dtype),
                pltpu.SMEM((1,), jnp.int32),
                pltpu.SemaphoreType.DMA(()),
                pltpu.SemaphoreType.DMA(()),
                pltpu.SemaphoreType.DMA((2,)),
            ],
        ),
        out_shape=jax.ShapeDtypeStruct((xs.shape[0], D_MODEL), F32),
        compiler_params=_cparams(("arbitrary",)),
        name="moe",
    )(item_expert, item_row0, item_rows, n_live, xs, w_up, w_down,
      b_up.reshape(N_EXPERTS, 1, 2 * D_FF), b_down.reshape(N_EXPERTS, 1, D_MODEL))


def _combine_body(blocks_per_batch, n_prompt_blocks, t_pad, n_p, dest_ref, dest_next_ref, ys_hbm, h_hbm,
                  gt_ref, gf_ref, yp_ref, ysm_ref, gbuf, hbuf, gsem, hsem):
    i = pl.program_id(0)
    n_blocks = pl.num_programs(0)
    cur = i % 2

    def h_copy(blk, slot):
        row = jnp.where(blk < n_prompt_blocks,
                        (blk // blocks_per_batch) * t_pad + N_META + (blk % blocks_per_batch) * TB, n_p)
        return pltpu.make_async_copy(h_hbm.at[pl.ds(pl.multiple_of(row, 8), TB), :], hbuf.at[slot],
                                     hsem.at[slot])

    def row_copy(slot, buf, k, c, u):
        return pltpu.make_async_copy(ys_hbm.at[pl.ds(slot, 1), :], gbuf.at[buf, k, c, pl.ds(u, 1), :],
                                     gsem.at[buf])

    def request(table_ref, blk, buf):
        h_copy(blk, buf).start()

        def issue(c, carry):
            for u in range(8):
                for k in range(TOP_K):
                    row_copy(_slot_of(table_ref, c, u, k), buf, k, c, u).start(priority=k % 2)
            return carry

        for c in range(TB // 8):
            issue(c, 0)

    def drain(c, carry):
        for _ in range(8 * TOP_K):
            row_copy(0, cur, 0, 0, 0).wait()
        return carry

    @pl.when(i == 0)
    def _():
        request(dest_ref, 0, 0)

    for nxt in range(2):
        @pl.when((i + 1 < n_blocks) & (cur == 1 - nxt))
        def _():
            request(dest_next_ref, i + 1, nxt)

    h_copy(i, cur).wait()
    lax.fori_loop(0, TB // 8, drain, 0)
    y = hbuf[cur]
    for k in range(TOP_K):
        y = y + gbuf[cur, k].reshape(TB, D_MODEL) * gt_ref[:, k:k + 1]
    ms = jnp.mean(y * y, axis=-1, keepdims=True)
    y = y * lax.rsqrt(ms + EPS) * gf_ref[...]

    @pl.when(i < n_prompt_blocks)
    def _():
        yp_ref[...] = y

    @pl.when(i >= n_prompt_blocks)
    def _():
        ysm_ref[...] = y


def _combine(dest_blk, ys, h_all, gates_t, g_final, blocks_per_batch, n_prompt_blocks, t_pad, n_p):
    return pl.pallas_call(
        functools.partial(_combine_body, blocks_per_batch, n_prompt_blocks, t_pad, n_p),
        grid=(n_prompt_blocks + 1,),
        in_specs=[
            pl.BlockSpec(SLOT_BLOCK, lambda i: (i, 0, 0), memory_space=pltpu.SMEM),
            pl.BlockSpec(SLOT_BLOCK, lambda i: (jnp.minimum(i + 1, n_prompt_blocks), 0, 0),
                         memory_space=pltpu.SMEM),
            pl.BlockSpec(memory_space=pl.ANY),
            pl.BlockSpec(memory_space=pl.ANY),
            pl.BlockSpec((TB, TOP_K), lambda i: (i, 0)),
            pl.BlockSpec((1, D_MODEL), lambda i: (0, 0)),
        ],
        out_specs=[
            pl.BlockSpec((TB, D_MODEL), lambda i: (jnp.minimum(i, n_prompt_blocks - 1), 0)),
            pl.BlockSpec((TB, D_MODEL), lambda i: (0, 0)),
        ],
        out_shape=[jax.ShapeDtypeStruct((n_prompt_blocks * TB, D_MODEL), F32),
                   jax.ShapeDtypeStruct((TB, D_MODEL), F32)],
        scratch_shapes=[pltpu.VMEM((2, TOP_K, TB // 8, 8, D_MODEL), F32), pltpu.VMEM((2, TB, D_MODEL), F32),
                        pltpu.SemaphoreType.DMA((2,)), pltpu.SemaphoreType.DMA((2,))],
        compiler_params=_cparams(("arbitrary",)),
        name="combine",
    )(dest_blk, dest_blk, ys, h_all, gates_t, g_final.reshape(1, D_MODEL))


def _work_items(counts, cap, max_items):
    items_per = (counts + TM - 1) // TM
    item_end = jnp.cumsum(items_per)
    item_start = item_end - items_per
    n_live = item_end[-1]
    it = jnp.arange(max_items, dtype=jnp.int32)
    it_live = jnp.minimum(it, n_live - 1)
    expert = jnp.minimum(jnp.searchsorted(item_end, it_live, side="right"), N_EXPERTS - 1).astype(jnp.int32)
    part = it_live - item_start[expert]
    rows = jnp.clip(counts[expert] - part * TM, 0, TM)
    rows = jnp.where(it < n_live, rows, 0).astype(jnp.int32)
    row0 = (expert * cap + part * TM).astype(jnp.int32)
    return expert, row0, rows, n_live.reshape(1).astype(jnp.int32)


def kernel(x_prompt, x_sample, cache_k, cache_v, state_conv, meta_tokens, g_mix, w_in, conv_w, conv_b,
           conv_ln_g, conv_ln_b, g_attn_out, w_out, g_ffn, w_router, b_router, w_up, b_up, w_down,
           b_down, g_final):
    n_batch, seq, _ = x_prompt.shape
    n_streams, dec_seq, _ = x_sample.shape
    depth = w_in.shape[0]
    assert depth == 1 and n_streams * dec_seq == BLK
    t_p = N_META + seq
    t_pad = -(-t_p // BLK) * BLK
    n_p = n_batch * t_pad
    n_s = n_streams * dec_seq
    n_all = n_p + n_s
    assert n_all % ROW_A == 0 and n_all % ROW_E == 0 and n_all % ROW_R == 0 and n_all % TB == 0

    meta = meta_tokens.astype(F32)
    pad = jnp.zeros((t_pad - t_p, D_MODEL), F32)
    pieces = []
    for b in range(n_batch):
        pieces += [meta, x_prompt[b], pad]
    x_all = jnp.concatenate(pieces + [x_sample.reshape(n_s, D_MODEL)], axis=0)

    l = 0
    u_all, q_all, kvb, k_p, v_p, k_s, v_s = _inproj(
        x_all, g_mix[l], w_in[l].astype(BF16), n_batch, t_p, t_pad)
    c_all = _conv(u_all, state_conv[l], conv_w[l], conv_b[l], conv_ln_g[l], conv_ln_b[l],
                  t_pad // BLK, n_p // BLK)
    g_heads = g_attn_out[l].reshape(N_HEADS, HEAD_DIM)
    o_all = _attn_prompt(q_all, kvb, g_heads, n_batch, t_pad)
    o_all = _attn_sample(q_all, kvb, cache_k[l:l + 1], cache_v[l:l + 1], g_heads, o_all, n_p, dec_seq, BLK)
    h_all, xn_packed, logits_t = _outproj(x_all, c_all, o_all, w_out[l].astype(BF16), g_ffn[l],
                                          w_router[l].T.astype(BF16), b_router[l])

    n_valid = n_batch * t_p + n_s
    cap = -(-n_valid // CH) * CH
    n_slots = N_EXPERTS * cap + TB * TOP_K
    max_items = -(-(n_valid * TOP_K) // TM) + N_EXPERTS
    dest, gates, counts = _route(logits_t, t_p, t_pad, n_p, cap)
    xs = _dispatch(_slot_table(dest, n_all // TB), xn_packed, n_slots)
    item_expert, item_row0, item_rows, n_live = _work_items(counts[:, 0], cap, max_items)
    ys = _moe(xs, item_expert, item_row0, item_rows, n_live, w_up[l], b_up[l], w_down[l], b_down[l])

    def out_tokens(a):
        a_p = a[:, :n_p].reshape(-1, n_batch, t_pad)[:, :, N_META:t_p].reshape(-1, n_batch * seq)
        return jnp.concatenate([a_p, a[:, n_p:]], axis=1)

    assert seq % TB == 0 and n_s == TB
    n_out_blocks = n_batch * seq // TB
    dest_out = _slot_table(out_tokens(dest), n_out_blocks + 1)
    y_p, y_s = _combine(dest_out, ys, h_all, out_tokens(gates).T, g_final, seq // TB, n_out_blocks, t_pad, n_p)

    y_prompt = y_p.reshape(n_batch, seq, D_MODEL)
    y_sample = y_s.reshape(n_streams, dec_seq, D_MODEL)
    k_prompt = k_p[None]
    v_prompt = v_p[None]
    conv_prompt = jnp.stack([u_all[b * t_pad + t_p - (CONV_WIDTH - 1):b * t_pad + t_p]
                             for b in range(n_batch)])[None]
    k_sample = k_s.reshape(1, n_streams, dec_seq, N_HEADS, HEAD_DIM)
    v_sample = v_s.reshape(1, n_streams, dec_seq, N_HEADS, HEAD_DIM)
    u_s = u_all[n_p:].reshape(n_streams, dec_seq, D_CONV)
    conv_sample = jnp.concatenate([state_conv[l], u_s], axis=1)[:, -(CONV_WIDTH - 1):][None]
    return (y_prompt, y_sample, k_prompt, v_prompt, conv_prompt, k_sample, v_sample, conv_sample)
```

```python
import functools
import math

import jax
import jax.numpy as jnp
from jax import lax
from jax.experimental import pallas as pl
from jax.experimental.pallas import tpu as pltpu

F32 = jnp.float32
BF16 = jnp.bfloat16

D_MODEL = 2048
N_META = 16
D_CONV = 1024
N_HEADS = 8
HEAD_DIM = 128
D_ATTN = N_HEADS * HEAD_DIM
D_IN = 2 * D_CONV + 3 * D_ATTN
CONV_WIDTH = 31
N_EXPERTS = 32
TOP_K = 4
D_FF = D_MODEL
SWIGLU_ALPHA = 1.702
SWIGLU_LIMIT = 7.0
EPS = 1e-5

BLK = 256
HPS = 8
HALO = 32
COL = 512
ROW_A = 640
ROW_E = 640
ROW_R = 640
TM = 2048
CH = 128
SUB = 6 * CH
TF = 256
TB = 256
LOG_ZERO = -104.0
VMEM_LIMIT = 56 * 1024 * 1024


def _cparams(sem, vmem=VMEM_LIMIT):
    return pltpu.CompilerParams(dimension_semantics=sem, vmem_limit_bytes=vmem)


def _row_tile_pieces(tile, n_batch, seq, t_p, t_pad, n_s):
    lo, hi = tile * ROW_A, (tile + 1) * ROW_A
    segments = []
    for b in range(n_batch):
        base = b * t_pad
        segments += [("meta", b, base, base + N_META), ("prompt", b, base + N_META, base + t_p),
                     ("zero", b, base + t_p, base + t_pad)]
    segments.append(("sample", 0, n_batch * t_pad, n_batch * t_pad + n_s))
    copies, zeros = [], []
    for kind, b, s0, s1 in segments:
        a0, a1 = max(lo, s0), min(hi, s1)
        if a0 >= a1:
            continue
        if kind == "zero":
            zeros.append((a0 - lo, a1 - a0))
        else:
            copies.append((kind, b, a0 - s0, a0 - lo, a1 - a0))
    return copies, zeros


def _inproj_body(n_batch, seq, t_p, t_pad, n_p, n_s, xp_hbm, xs_hbm, meta_hbm, g_ref, wa_ref, wg_ref,
                 wq_ref, w1_ref, w2_ref, u_ref, q_ref, kvb_ref, kp_hbm, vp_hbm, ks_hbm, vs_hbm, xo_hbm,
                 xbuf, xn_s, kv4_s, xsem, osem, ksem, vsem):
    i = pl.program_id(0)
    j = pl.program_id(1)
    n_tiles = (n_p + n_s) // ROW_A
    heads_per_tile = COL // HEAD_DIM
    k4_s, v4_s = kv4_s.at[0], kv4_s.at[1]

    def tile_copies(tile):
        slot = tile % 2
        out = []
        for kind, b, src_row, dst_row, rows in _row_tile_pieces(tile, n_batch, seq, t_p, t_pad, n_s)[0]:
            src = {"meta": meta_hbm, "prompt": xp_hbm.at[b], "sample": xs_hbm}[kind]
            out.append(pltpu.make_async_copy(src.at[pl.ds(src_row, rows)],
                                             xbuf.at[slot, pl.ds(dst_row, rows)], xsem.at[slot]))
        return out

    def request_tile(tile):
        for r0, rows in _row_tile_pieces(tile, n_batch, seq, t_p, t_pad, n_s)[1]:
            xbuf[tile % 2, r0:r0 + rows, :] = jnp.zeros((rows, D_MODEL), F32)
        for cp in tile_copies(tile):
            cp.start()

    def x_out_copy(tile):
        return pltpu.make_async_copy(xbuf.at[tile % 2], xo_hbm.at[pl.ds(tile * ROW_A, ROW_A)], osem.at[tile % 2])

    def chunk_copies(tile, src, dst_p, dst_s, sem):
        out = []
        per_batch, full, rem = t_pad // CH, t_p // CH, t_p % CH
        for c in range(ROW_A // CH):
            g = tile * (ROW_A // CH) + c
            is_p = g < n_p // CH
            b, wi = g // per_batch, g % per_batch
            out.append((is_p & (wi < full), pltpu.make_async_copy(
                src.at[pl.ds(c * CH, CH)], dst_p.at[b, pl.ds(pl.multiple_of(wi * CH, CH), CH)], sem)))
            if rem:
                out.append((is_p & (wi == full), pltpu.make_async_copy(
                    src.at[pl.ds(c * CH, rem)], dst_p.at[b, pl.ds(full * CH, rem)], sem)))
            out.append((jnp.logical_not(is_p), pltpu.make_async_copy(
                src.at[pl.ds(c * CH, CH)],
                dst_s.at[pl.ds(pl.multiple_of((g - n_p // CH) * CH, CH), CH)], sem)))
        return out

    def start_all(copies):
        for cond, cp in copies:
            @pl.when(cond)
            def _():
                cp.start()

    def wait_all(copies):
        for cond, cp in copies:
            @pl.when(cond)
            def _():
                cp.wait()

    def store_heads(dst, p, first_head):
        for hh in range(heads_per_tile):
            dst[:, first_head + hh, :] = p[:, hh * HEAD_DIM:(hh + 1) * HEAD_DIM]

    @pl.when(j == 0)
    def _():
        @pl.when(i == 0)
        def _():
            request_tile(0)

        for t in range(n_tiles):
            @pl.when(i == t)
            def _():
                for cp in tile_copies(t):
                    cp.wait()
                if t >= 1:
                    x_out_copy(t - 1).wait()
                if t + 1 < n_tiles:
                    request_tile(t + 1)
                x_out_copy(t).start()

        x = xbuf[i % 2]
        ms = jnp.mean(x * x, axis=-1, keepdims=True)
        xn_s[...] = (x * lax.rsqrt(ms + EPS) * g_ref[...]).astype(BF16)

        @pl.when(i > 0)
        def _():
            wait_all(chunk_copies(i - 1, k4_s, kp_hbm, ks_hbm, ksem))

    @pl.when((j == 1) & (i > 0))
    def _():
        wait_all(chunk_copies(i - 1, v4_s, vp_hbm, vs_hbm, vsem))

    xn = xn_s[...]
    proj = lambda w_ref: jnp.dot(xn, w_ref[...], preferred_element_type=F32)
    u_ref[...] = proj(wa_ref) * jax.nn.sigmoid(proj(wg_ref))
    q_ref[...] = (proj(wq_ref) * (1.0 / math.sqrt(HEAD_DIM))).astype(BF16)
    for t, w_ref in enumerate((w1_ref, w2_ref)):
        p = proj(w_ref)
        kvb_ref[0, :, t * COL:(t + 1) * COL] = p.astype(BF16)
        store_heads(kv4_s.at[j], p, t * heads_per_tile)

    @pl.when(j == 0)
    def _():
        start_all(chunk_copies(i, k4_s, kp_hbm, ks_hbm, ksem))

    @pl.when(j == 1)
    def _():
        start_all(chunk_copies(i, v4_s, vp_hbm, vs_hbm, vsem))

        @pl.when(i == pl.num_programs(0) - 1)
        def _():
            wait_all(chunk_copies(i, k4_s, kp_hbm, ks_hbm, ksem))
            wait_all(chunk_copies(i, v4_s, vp_hbm, vs_hbm, vsem))
            x_out_copy(n_tiles - 1).wait()


def _inproj(x_prompt, x_sample2d, meta, g_mix, w_in_bf, t_pad):
    n_batch, seq, _ = x_prompt.shape
    n_s = x_sample2d.shape[0]
    t_p = meta.shape[0] + seq
    n_p = n_batch * t_pad
    n = n_p + n_s
    assert D_IN == 10 * COL and COL == 4 * HEAD_DIM and ROW_A % CH == 0 and t_pad % CH == 0
    assert n % ROW_A == 0 and meta.shape[0] == N_META and N_META % 8 == 0 and seq % 8 == 0

    def w_spec(first, stride):
        return pl.BlockSpec((D_MODEL, COL), lambda i, j: (0, first + stride * j))

    wide = lambda dt: jax.ShapeDtypeStruct((n, D_CONV), dt)
    cache_p = jax.ShapeDtypeStruct((n_batch, t_p, N_HEADS, HEAD_DIM), F32)
    cache_s = jax.ShapeDtypeStruct((n - n_p, N_HEADS, HEAD_DIM), F32)
    any_spec = pl.BlockSpec(memory_space=pl.ANY)
    half_spec = pl.BlockSpec((ROW_A, COL), lambda i, j: (i, j))
    return pl.pallas_call(
        functools.partial(_inproj_body, n_batch, seq, t_p, t_pad, n_p, n_s),
        grid=(n // ROW_A, 2),
        in_specs=[
            any_spec, any_spec, any_spec,
            pl.BlockSpec((1, D_MODEL), lambda i, j: (0, 0)),
            w_spec(0, 1), w_spec(2, 1), w_spec(4, 1), w_spec(6, 2), w_spec(7, 2),
        ],
        out_specs=[half_spec, half_spec, pl.BlockSpec((1, ROW_A, D_ATTN), lambda i, j: (j, i, 0)),
                   any_spec, any_spec, any_spec, any_spec, any_spec],
        out_shape=[wide(F32), wide(BF16), jax.ShapeDtypeStruct((2, n, D_ATTN), BF16),
                   cache_p, cache_p, cache_s, cache_s, jax.ShapeDtypeStruct((n, D_MODEL), F32)],
        scratch_shapes=[pltpu.VMEM((2, ROW_A, D_MODEL), F32),
                        pltpu.VMEM((ROW_A, D_MODEL), BF16),
                        pltpu.VMEM((2, ROW_A, N_HEADS, HEAD_DIM), F32),
                        pltpu.SemaphoreType.DMA((2,)), pltpu.SemaphoreType.DMA((2,)),
                        pltpu.SemaphoreType.DMA(()), pltpu.SemaphoreType.DMA(())],
        compiler_params=_cparams(("arbitrary", "arbitrary")),
        name="inproj",
    )(x_prompt, x_sample2d, meta, g_mix.reshape(1, D_MODEL), w_in_bf, w_in_bf, w_in_bf, w_in_bf, w_in_bf)


def _conv_taps(ext_ref, w_ref, b_ref, dst_ref, src_row0, dst_row0, nrows):
    def chunk(cc, carry):
        lanes = pl.ds(pl.multiple_of(cc * 128, 128), 128)
        acc = jnp.broadcast_to(b_ref[:, lanes], (nrows, 128))
        for j in range(CONV_WIDTH):
            acc = acc + ext_ref[pl.ds(src_row0 + j, nrows), lanes] * w_ref[pl.ds(j, 1), lanes]
        dst_ref[pl.ds(dst_row0, nrows), lanes] = acc
        return carry

    lax.fori_loop(0, D_CONV // 128, chunk, 0)


def _conv_block(ext_ref, sh_ref, w_ref, b_ref, dst_ref, off):
    span = HALO + BLK

    def chunk(cc, carry):
        lanes = pl.ds(pl.multiple_of(cc * 128, 128), 128)
        for r in range(8):
            n = span - (8 if r else 0)
            sh_ref[r, 0:n, :] = ext_ref[pl.ds(r, n), lanes]
        for rc in range(BLK // 64):
            acc = jnp.broadcast_to(b_ref[:, lanes], (64, 128))
            for j in range(CONV_WIDTH):
                r = (off + j) % 8
                acc = acc + sh_ref[r, pl.ds(off + j - r + rc * 64, 64), :] * w_ref[pl.ds(j, 1), lanes]
            dst_ref[pl.ds(rc * 64, 64), lanes] = acc
        return carry

    lax.fori_loop(0, D_CONV // 128, chunk, 0)


def _conv_body(blocks_per_batch, n_prompt_blocks, u_ref, prev_ref, state_ref, w_ref, b_ref,
               lg_ref, lb_ref, c_ref, ext_s, conv_s, sh_s):
    i = pl.program_id(0)
    off = HALO - (CONV_WIDTH - 1)

    @pl.when(i < n_prompt_blocks)
    def _():
        first = (i % blocks_per_batch) == 0
        ext_s[0:HALO, :] = jnp.where(first, 0.0, prev_ref[...])
        ext_s[HALO:HALO + BLK, :] = u_ref[...]
        _conv_block(ext_s, sh_s, w_ref, b_ref, conv_s, off)

    @pl.when(i >= n_prompt_blocks)
    def _():
        def stream(s, carry):
            r0 = pl.multiple_of(s * 16, 16)
            ext_s[off:HALO, :] = state_ref[s]
            ext_s[HALO:HALO + 16, :] = u_ref[pl.ds(r0, 16), :]
            _conv_taps(ext_s, w_ref, b_ref, conv_s, off, r0, 16)
            return carry

        lax.fori_loop(0, BLK // 16, stream, 0)

    x = conv_s[...]
    mu = jnp.mean(x, axis=-1, keepdims=True)
    xc = x - mu
    var = jnp.mean(xc * xc, axis=-1, keepdims=True)
    y = xc * lax.rsqrt(var + EPS) * lg_ref[...] + lb_ref[...]
    c_ref[...] = (y * jax.nn.sigmoid(y)).astype(BF16)


def _conv(u_all, state_conv, conv_w, conv_b, ln_g, ln_b, blocks_per_batch, n_prompt_blocks):
    n = u_all.shape[0]
    row = lambda a: a.reshape(1, D_CONV)
    const = lambda shape: pl.BlockSpec(shape, lambda i: (0,) * len(shape))
    return pl.pallas_call(
        functools.partial(_conv_body, blocks_per_batch, n_prompt_blocks),
        grid=(n // BLK,),
        in_specs=[
            pl.BlockSpec((BLK, D_CONV), lambda i: (i, 0)),
            pl.BlockSpec((HALO, D_CONV), lambda i: (jnp.maximum(i * (BLK // HALO) - 1, 0), 0)),
            const(state_conv.shape),
            const((CONV_WIDTH, D_CONV)),
            const((1, D_CONV)), const((1, D_CONV)), const((1, D_CONV)),
        ],
        out_specs=pl.BlockSpec((BLK, D_CONV), lambda i: (i, 0)),
        out_shape=jax.ShapeDtypeStruct((n, D_CONV), BF16),
        scratch_shapes=[pltpu.VMEM((HALO + BLK, D_CONV), F32), pltpu.VMEM((BLK, D_CONV), F32),
                        pltpu.VMEM((8, HALO + BLK, 128), F32)],
        compiler_params=_cparams(("arbitrary",)),
        name="conv",
    )(u_all, u_all, state_conv, conv_w, row(conv_b), row(ln_g), row(ln_b))


def _suffix_matrix(n, extra):
    r = lax.broadcasted_iota(jnp.int32, (n, n + extra), 0)
    c = lax.broadcasted_iota(jnp.int32, (n, n + extra), 1)
    return jnp.where((r > c) | (c >= n), 1.0, 0.0).astype(BF16)


def _split_dot(x, m):
    hi = x.astype(BF16)
    lo = (x - hi.astype(F32)).astype(BF16)
    return (jnp.dot(hi, m, preferred_element_type=F32) + jnp.dot(lo, m, preferred_element_type=F32))


def _log_keep(z):
    return -(jnp.maximum(z, 0.0) + jnp.log(1.0 + jnp.exp(-jnp.abs(z))))


def _attn_prompt_body(q_ref, k_ref, v_ref, g_ref, o_ref, m_s):
    hg = pl.program_id(1)
    qi = pl.program_id(2)

    @pl.when((pl.program_id(0) == 0) & (hg == 0) & (qi == 0))
    def _():
        m_s[...] = _suffix_matrix(BLK, 128)

    m = m_s[...]

    def tile(hh, kb, carry, diagonal):
        lanes = slice(hh * HEAD_DIM, (hh + 1) * HEAD_DIM)
        rows = pl.ds(pl.multiple_of(kb * BLK, BLK), BLK)
        z = lax.dot_general(q_ref[:, lanes], k_ref[rows, lanes], (((1,), (1,)), ((), ())),
                            preferred_element_type=F32)
        lk = _log_keep(z)
        if diagonal:
            keep = (lax.broadcasted_iota(jnp.int32, (BLK, BLK), 1)
                    < lax.broadcasted_iota(jnp.int32, (BLK, BLK), 0))
            lk = jnp.where(keep, lk, 0.0)
        cs = _split_dot(lk, m)
        after = cs[:, :BLK]
        if carry is not None:
            after = after + jnp.concatenate([carry, carry], axis=1)
        w = jnp.exp(z + lk + after)
        if diagonal:
            w = jnp.where(keep, w, 0.0)
        pv = jnp.dot(w.astype(BF16), v_ref[rows, lanes], preferred_element_type=F32)
        return pv, cs[:, BLK:]

    first = [tile(hh, qi, None, True) for hh in range(HPS)]

    def cond(st):
        kb, _, carries = st
        live = jnp.max(carries[0])
        for c in carries[1:]:
            live = jnp.maximum(live, jnp.max(c))
        return (kb >= 0) & (live > LOG_ZERO)

    def step(st):
        kb, accs, carries = st
        out = [tile(hh, kb, carries[hh], False) for hh in range(HPS)]
        return (kb - 1, tuple(a + o[0] for a, o in zip(accs, out)),
                tuple(c + o[1] for c, o in zip(carries, out)))

    _, accs, _ = lax.while_loop(cond, step, (qi - 1, tuple(f[0] for f in first), tuple(f[1] for f in first)))
    for hh in range(HPS):
        acc = accs[hh]
        ms = jnp.mean(acc * acc, axis=-1, keepdims=True)
        o_ref[:, hh * HEAD_DIM:(hh + 1) * HEAD_DIM] = (
            acc * lax.rsqrt(ms + EPS) * g_ref[pl.ds(hg * HPS + hh, 1), :]).astype(BF16)


def _attn_prompt(q_all, kvb, g_heads, n_batch, t_pad):
    nq = t_pad // BLK
    wide = HPS * HEAD_DIM
    return pl.pallas_call(
        _attn_prompt_body,
        grid=(n_batch, N_HEADS // HPS, nq),
        in_specs=[
            pl.BlockSpec((BLK, wide), lambda b, h, i: (b * nq + i, h)),
            pl.BlockSpec((None, t_pad, wide), lambda b, h, i: (0, b, h)),
            pl.BlockSpec((None, t_pad, wide), lambda b, h, i: (1, b, h)),
            pl.BlockSpec((N_HEADS, HEAD_DIM), lambda b, h, i: (0, 0)),
        ],
        out_specs=pl.BlockSpec((BLK, wide), lambda b, h, i: (b * nq + i, h)),
        out_shape=jax.ShapeDtypeStruct((q_all.shape[0], D_ATTN), BF16),
        scratch_shapes=[pltpu.VMEM((BLK, BLK + 128), BF16)],
        compiler_params=_cparams(("arbitrary", "arbitrary", "arbitrary")),
        name="attn_prompt",
    )(q_all, kvb, kvb, g_heads)


def _attn_sample_body(n_kb, kblk, q_ref, kn_ref, vn_ref, kc_hbm, vc_hbm, g_ref, o_in_hbm, o_ref,
                      kbuf, vbuf, qbd_s, acc_s, carry_s, m_s, mn_s, ksem, vsem):
    s = pl.program_id(0)
    n_streams = pl.num_programs(0)
    dq = q_ref.shape[0]
    last = n_kb - 1

    def cache_copies(stream, jb, slot):
        rows = pl.ds(pl.multiple_of(jb * kblk, kblk), kblk)
        return (pltpu.make_async_copy(kc_hbm.at[0, stream, rows], kbuf.at[slot], ksem.at[slot]),
                pltpu.make_async_copy(vc_hbm.at[0, stream, rows], vbuf.at[slot], vsem.at[slot]))

    def start(stream, jb, slot):
        for cp in cache_copies(stream, jb, slot):
            cp.start()

    def wait(stream, jb, slot):
        for cp in cache_copies(stream, jb, slot):
            cp.wait()

    def cat_heads(buf, slot):
        return jnp.concatenate([buf[slot, :, hh, :] for hh in range(N_HEADS)], axis=1).astype(BF16)

    def scores(kcat):
        return jnp.dot(kcat, qbd_s[...], preferred_element_type=F32)

    def add_values(w, vcat):
        full = lax.dot_general(w.astype(BF16), vcat, (((0,), (0,)), ((), ())), preferred_element_type=F32)
        for hh in range(N_HEADS):
            acc_s[hh] += full[hh * dq:(hh + 1) * dq, hh * HEAD_DIM:(hh + 1) * HEAD_DIM]

    @pl.when(s == 0)
    def _():
        start(0, last, last % 2)
        r = lax.broadcasted_iota(jnp.int32, (kblk, kblk), 0)
        c = lax.broadcasted_iota(jnp.int32, (kblk, kblk), 1)
        m_s[...] = jnp.where(c > r, 1.0, 0.0).astype(BF16)
        mn_s[...] = m_s[0:dq, 0:dq]

    qrep = jnp.concatenate([q_ref[...].astype(F32)] * N_HEADS, axis=0)
    qt = qrep.T
    row_head = lax.broadcasted_iota(jnp.int32, (D_ATTN, 128), 0) // HEAD_DIM
    col_head = lax.broadcasted_iota(jnp.int32, (D_ATTN, 128), 1) // dq
    qbd_s[...] = jnp.where(row_head == col_head, qt, 0.0).astype(BF16)
    acc_s[...] = jnp.zeros_like(acc_s)
    z = scores(kn_ref[...])
    lk = _log_keep(z)
    key = lax.broadcasted_iota(jnp.int32, (dq, 128), 0)
    qry = lax.broadcasted_iota(jnp.int32, (dq, 128), 1) % dq
    keep = key < qry
    lk = jnp.where(keep, lk, 0.0)
    after = _split_dot_left(mn_s[...], lk)
    w = jnp.where(keep, jnp.exp(z + lk + after), 0.0)
    add_values(w, vn_ref[...])
    carry_s[...] = jnp.sum(lk, axis=0, keepdims=True)

    def cond(jb):
        return (jb >= 0) & (jnp.max(carry_s[...]) > LOG_ZERO)

    def step(jb):
        slot = jb % 2
        wait(s, jb, slot)

        @pl.when(jb > 0)
        def _():
            start(s, jb - 1, 1 - slot)

        z = scores(cat_heads(kbuf, slot))
        lk = _log_keep(z)
        after = _split_dot_left(m_s[...], lk) + carry_s[...]
        w = jnp.exp(z + lk + after)
        add_values(w, cat_heads(vbuf, slot))
        carry_s[...] += jnp.sum(lk, axis=0, keepdims=True)
        return jb - 1

    jb_end = lax.while_loop(cond, step, last)

    @pl.when(jb_end >= 0)
    def _():
        wait(s, jb_end, jb_end % 2)

    @pl.when(s + 1 < n_streams)
    def _():
        start(s + 1, last, last % 2)

    for hh in range(N_HEADS):
        a = acc_s[hh]
        ms = jnp.mean(a * a, axis=-1, keepdims=True)
        o_ref[:, hh * HEAD_DIM:(hh + 1) * HEAD_DIM] = (
            a * lax.rsqrt(ms + EPS) * g_ref[pl.ds(hh, 1), :]).astype(BF16)


def _split_dot_left(m, x):
    hi = x.astype(BF16)
    lo = (x - hi.astype(F32)).astype(BF16)
    return (jnp.dot(m, hi, preferred_element_type=F32) + jnp.dot(m, lo, preferred_element_type=F32))


def _attn_sample(q_all, kvb, cache_k, cache_v, g_heads, o_all, row0, dec_seq, kblk):
    n_streams = cache_k.shape[1]
    past = cache_k.shape[2]
    n_kb = past // kblk
    assert n_kb * kblk == past and row0 % dec_seq == 0
    row_spec = pl.BlockSpec((dec_seq, D_ATTN), lambda s: (row0 // dec_seq + s, 0))
    kv_spec = lambda which: pl.BlockSpec((None, dec_seq, D_ATTN), lambda s: (which, row0 // dec_seq + s, 0))
    cache_buf = pltpu.VMEM((2, kblk, N_HEADS, HEAD_DIM), cache_k.dtype)
    any_spec = pl.BlockSpec(memory_space=pl.ANY)
    return pl.pallas_call(
        functools.partial(_attn_sample_body, n_kb, kblk),
        grid=(n_streams,),
        in_specs=[row_spec, kv_spec(0), kv_spec(1), any_spec, any_spec,
                  pl.BlockSpec((N_HEADS, HEAD_DIM), lambda s: (0, 0)), any_spec],
        out_specs=row_spec,
        out_shape=jax.ShapeDtypeStruct(o_all.shape, o_all.dtype),
        input_output_aliases={6: 0},
        scratch_shapes=[
            cache_buf, cache_buf,
            pltpu.VMEM((D_ATTN, 128), BF16),
            pltpu.VMEM((N_HEADS, dec_seq, HEAD_DIM), F32),
            pltpu.VMEM((1, 128), F32),
            pltpu.VMEM((kblk, kblk), BF16),
            pltpu.VMEM((dec_seq, dec_seq), BF16),
            pltpu.SemaphoreType.DMA((2,)), pltpu.SemaphoreType.DMA((2,)),
        ],
        compiler_params=_cparams(("arbitrary",)),
        name="attn_sample",
    )(q_all, kvb, kvb, cache_k, cache_v, g_heads, o_all)


def _outproj_body(x_ref, c_ref, o_ref, w_ref, g_ref, wr_ref, br_ref, h_ref, xn_ref, lg_ref):
    h = (x_ref[...]
         + jnp.dot(c_ref[...], w_ref[0:D_CONV, :], preferred_element_type=F32)
         + jnp.dot(o_ref[...], w_ref[D_CONV:, :], preferred_element_type=F32))
    h_ref[...] = h
    ms = jnp.mean(h * h, axis=-1, keepdims=True)
    xn = (h * lax.rsqrt(ms + EPS) * g_ref[...]).astype(BF16)
    bits = lax.bitcast_convert_type(xn.astype(F32), jnp.uint32)
    xn_ref[...] = (bits[:, :D_MODEL // 2] >> 16) | bits[:, D_MODEL // 2:]
    lg_ref[...] = lax.dot_general(wr_ref[...], xn, (((1,), (1,)), ((), ())),
                                  preferred_element_type=F32) + br_ref[...]


def _outproj(x_all, c_all, o_all, w_out_bf, g_ffn, w_router_t_bf, b_router):
    n = x_all.shape[0]
    const = lambda shape: pl.BlockSpec(shape, lambda i: (0,) * len(shape))
    return pl.pallas_call(
        _outproj_body,
        grid=(n // ROW_E,),
        in_specs=[
            pl.BlockSpec((ROW_E, D_MODEL), lambda i: (i, 0)),
            pl.BlockSpec((ROW_E, D_CONV), lambda i: (i, 0)),
            pl.BlockSpec((ROW_E, D_ATTN), lambda i: (i, 0)),
            const((D_MODEL, D_MODEL)),
            const((1, D_MODEL)),
            const((N_EXPERTS, D_MODEL)),
            const((N_EXPERTS, 1)),
        ],
        out_specs=[
            pl.BlockSpec((ROW_E, D_MODEL), lambda i: (i, 0)),
            pl.BlockSpec((ROW_E, D_MODEL // 2), lambda i: (i, 0)),
            pl.BlockSpec((N_EXPERTS, ROW_E), lambda i: (0, i)),
        ],
        out_shape=[jax.ShapeDtypeStruct((n, D_MODEL), F32), jax.ShapeDtypeStruct((n, D_MODEL // 2), jnp.uint32),
                   jax.ShapeDtypeStruct((N_EXPERTS, n), F32)],
        compiler_params=_cparams(("arbitrary",)),
        name="outproj",
    )(x_all, c_all, o_all, w_out_bf, g_ffn.reshape(1, D_MODEL), w_router_t_bf,
      b_router.reshape(N_EXPERTS, 1))


def _route_body(t_p, t_pad, n_p, cap, lg_ref, d_ref, gt_ref, cnt_ref, tri_s, run_s):
    i = pl.program_id(0)
    tb = lg_ref.shape[1]

    @pl.when(i == 0)
    def _():
        r = lax.broadcasted_iota(jnp.int32, (tb, tb + 128), 0)
        c = lax.broadcasted_iota(jnp.int32, (tb, tb + 128), 1)
        tri_s[...] = jnp.where((r < c) | (c >= tb), 1.0, 0.0).astype(BF16)
        run_s[...] = jnp.zeros_like(run_s)

    tok = i * tb + lax.broadcasted_iota(jnp.int32, (1, tb), 1)
    valid = ((tok % t_pad) < t_p) | (tok >= n_p)
    eid = lax.broadcasted_iota(jnp.int32, (N_EXPERTS, tb), 0)
    lg = lg_ref[...]
    sel = jnp.zeros((N_EXPERTS, tb), F32)
    hot, top = [], []
    for _ in range(TOP_K):
        mx = jnp.max(lg, axis=0, keepdims=True)
        idx = jnp.min(jnp.where(lg == mx, eid, N_EXPERTS), axis=0, keepdims=True)
        one = eid == idx
        lg = jnp.where(one, -jnp.inf, lg)
        hot.append(one)
        top.append(mx)
        sel = sel + jnp.where(one & valid, 1.0, 0.0)
    ex = [jnp.exp(t - top[0]) for t in top]
    den = ex[0] + ex[1] + ex[2] + ex[3]
    cs = jnp.dot(sel.astype(BF16), tri_s[...], preferred_element_type=F32)
    run = run_s[...]
    slot = (cs[:, :tb] + jnp.concatenate([run] * (tb // 128), axis=1)
            + (eid * cap).astype(F32))
    for k in range(TOP_K):
        mine = jnp.sum(jnp.where(hot[k], slot, 0.0), axis=0, keepdims=True).astype(jnp.int32)
        trash = N_EXPERTS * cap + (tok % TB) * TOP_K + k
        d_ref[pl.ds(k, 1), :] = jnp.where(valid, mine, trash)
        gt_ref[pl.ds(k, 1), :] = ex[k] / den
    run = run + cs[:, tb:]
    run_s[...] = run
    cnt_ref[...] = run.astype(jnp.int32)


def _route(logits_t, t_p, t_pad, n_p, cap):
    n = logits_t.shape[1]
    blk = lambda: pl.BlockSpec((TOP_K, ROW_R), lambda i: (0, i))
    return pl.pallas_call(
        functools.partial(_route_body, t_p, t_pad, n_p, cap),
        grid=(n // ROW_R,),
        in_specs=[pl.BlockSpec((N_EXPERTS, ROW_R), lambda i: (0, i))],
        out_specs=[blk(), blk(), pl.BlockSpec((N_EXPERTS, 128), lambda i: (0, 0))],
        out_shape=[jax.ShapeDtypeStruct((TOP_K, n), jnp.int32),
                   jax.ShapeDtypeStruct((TOP_K, n), F32), jax.ShapeDtypeStruct((N_EXPERTS, 128), jnp.int32)],
        scratch_shapes=[pltpu.VMEM((ROW_R, ROW_R + 128), BF16), pltpu.VMEM((N_EXPERTS, 128), F32)],
        compiler_params=_cparams(("arbitrary",)),
        name="route",
    )(logits_t)


def _slot_table(dest, n_blocks):
    return dest.T.reshape(n_blocks, TB // 8, 8 * TOP_K)


def _slot_of(dest_ref, i, u, k):
    return dest_ref[0, i, u * TOP_K + k]


SLOT_BLOCK = (1, TB // 8, 8 * TOP_K)


def _dispatch_body(dest_ref, xp_ref, xs_hbm, sem):
    def row_copy(i, u, slot):
        return pltpu.make_async_copy(xp_ref.at[i, pl.ds(u, 1), :], xs_hbm.at[pl.ds(slot, 1), :], sem)

    def issue(i, c):
        for u in range(8):
            for k in range(TOP_K):
                row_copy(i, u, _slot_of(dest_ref, i, u, k)).start(priority=k % 2)
        return c

    def drain(i, c):
        for _ in range(8 * TOP_K):
            row_copy(0, 0, 0).wait()
        return c

    for i in range(TB // 8):
        issue(i, 0)
    lax.fori_loop(0, TB // 8, drain, 0)


def _dispatch(dest_tab, xn_packed, n_slots):
    n, width = xn_packed.shape
    return pl.pallas_call(
        _dispatch_body,
        grid=(n // TB,),
        in_specs=[
            pl.BlockSpec(SLOT_BLOCK, lambda i: (i, 0, 0), memory_space=pltpu.SMEM),
            pl.BlockSpec((TB // 8, 8, width), lambda i: (i, 0, 0)),
        ],
        out_specs=pl.BlockSpec(memory_space=pl.ANY),
        out_shape=jax.ShapeDtypeStruct((n_slots, width), jnp.uint32),
        scratch_shapes=[pltpu.SemaphoreType.DMA(())],
        compiler_params=_cparams(("arbitrary",)),
        name="dispatch",
    )(dest_tab, xn_packed.reshape(n // 8, 8, width))


def _moe_body(ie_ref, r0_ref, nr_ref, nlive_ref, xs_hbm, wup_hbm, wdn_hbm, bup_ref, bd_ref,
              ys_hbm, xraw, xb16, yacc, wg_buf, wl_buf, wd_buf, pend_s, xsem, ysem, wsem):
    it = pl.program_id(0)
    n_live = nlive_ref[0]
    n_f = D_FF // TF
    nrows = nr_ref[it]

    def chunks(item):
        return (nr_ref[item] + CH - 1) // CH

    def x_copy(item, c):
        src = pl.multiple_of(r0_ref[item] + c * CH, CH)
        return pltpu.make_async_copy(xs_hbm.at[pl.ds(src, CH), :],
                                     xraw.at[pl.ds(pl.multiple_of(c * CH, CH), CH), :], xsem)

    def y_copy(item, c):
        dst = pl.multiple_of(r0_ref[item] + c * CH, CH)
        return pltpu.make_async_copy(yacc.at[pl.ds(pl.multiple_of(c * CH, CH), CH), :],
                                     ys_hbm.at[pl.ds(dst, CH), :], ysem)

    def w_copies(item, jj, slot):
        e = ie_ref[item]
        glu = pl.ds(pl.multiple_of(jj * TF, TF), TF)
        lin = pl.ds(pl.multiple_of(D_FF + jj * TF, TF), TF)
        return (pltpu.make_async_copy(wup_hbm.at[e, :, glu], wg_buf.at[slot], wsem.at[slot]),
                pltpu.make_async_copy(wup_hbm.at[e, :, lin], wl_buf.at[slot], wsem.at[slot]),
                pltpu.make_async_copy(wdn_hbm.at[e, glu, :], wd_buf.at[slot], wsem.at[slot]))

    def start_w(item, jj, slot):
        for cp in w_copies(item, jj, slot):
            cp.start()

    def for_chunks(n, fn):
        def body(c, carry):
            fn(c)
            return carry
        lax.fori_loop(0, n, body, 0)

    @pl.when(it == 0)
    def _():
        pend_s[0] = 0
        for_chunks(chunks(0), lambda c: x_copy(0, c).start())
        start_w(0, 0, 0)

    n_ch = chunks(it)
    for_chunks(n_ch, lambda c: x_copy(it, c).wait())

    def unpack(c):
        rows = pl.ds(pl.multiple_of(c * CH, CH), CH)
        p = xraw[rows, :]
        rid = c * CH + lax.broadcasted_iota(jnp.int32, (CH, 1), 0)
        live = rid < nrows
        lo = lax.bitcast_convert_type(p << 16, F32)
        hi = lax.bitcast_convert_type(p & jnp.uint32(0xFFFF0000), F32)
        xb16[rows, 0:D_MODEL // 2] = jnp.where(live, lo, 0.0).astype(BF16)
        xb16[rows, D_MODEL // 2:] = jnp.where(live, hi, 0.0).astype(BF16)

    for_chunks(n_ch, unpack)

    @pl.when(it + 1 < n_live)
    def _():
        for_chunks(chunks(it + 1), lambda c: x_copy(it + 1, c).start())

    for_chunks(pend_s[0], lambda c: y_copy(it, 0).wait())
    n_full = n_ch // (SUB // CH)
    tail = n_ch - n_full * (SUB // CH)

    def hidden_tile(j, carry):
        slot = j % 2
        for cp in w_copies(it, j, slot):
            cp.wait()

        @pl.when(j + 1 < n_f)
        def _():
            start_w(it, j + 1, 1 - slot)

        @pl.when((j + 1 == n_f) & (it + 1 < n_live))
        def _():
            start_w(it + 1, 0, 1 - slot)

        bg = bup_ref[0, :, pl.ds(pl.multiple_of(j * TF, TF), TF)]
        bl = bup_ref[0, :, pl.ds(pl.multiple_of(D_FF + j * TF, TF), TF)]

        def ffn_rows(r0, n):
            rows = pl.ds(r0, n)
            x = xb16[rows, :]
            hg = jnp.dot(x, wg_buf[slot].astype(BF16), preferred_element_type=F32) + bg
            hl = jnp.dot(x, wl_buf[slot].astype(BF16), preferred_element_type=F32) + bl
            hg = jnp.minimum(hg, SWIGLU_LIMIT)
            hl = jnp.clip(hl, -SWIGLU_LIMIT, SWIGLU_LIMIT)
            a = hg * jax.nn.sigmoid(SWIGLU_ALPHA * hg) * (hl + 1.0)
            part = jnp.dot(a.astype(BF16), wd_buf[slot].astype(BF16), preferred_element_type=F32)
            yacc[rows, :] = jnp.where(j == 0, bd_ref[0], yacc[rows, :]) + part

            @pl.when(j == n_f - 1)
            def _():
                for cc in range(n // CH):
                    y_copy(it, r0 // CH + cc).start()

        def full(s, c):
            ffn_rows(pl.multiple_of(s * SUB, CH), SUB)
            return c

        fused = (tail > 0) & (n_full > 0)
        lax.fori_loop(0, n_full - fused.astype(jnp.int32), full, 0)
        for t in range(1, SUB // CH):
            @pl.when((tail == t) & (n_full == 0))
            def _():
                ffn_rows(0, t * CH)

            @pl.when((tail == t) & (n_full > 0))
            def _():
                ffn_rows(pl.multiple_of((n_full - 1) * SUB, CH), SUB + t * CH)
        return carry

    lax.fori_loop(0, n_f, hidden_tile, 0)
    pend_s[0] = n_ch

    @pl.when(it == n_live - 1)
    def _():
        for_chunks(n_ch, lambda c: y_copy(it, c).wait())
        pend_s[0] = 0


def _moe(xs, item_expert, item_row0, item_rows, n_live, w_up, b_up, w_down, b_down):
    assert (D_FF // TF) % 2 == 0
    any_spec = pl.BlockSpec(memory_space=pl.ANY)
    return pl.pallas_call(
        _moe_body,
        grid_spec=pltpu.PrefetchScalarGridSpec(
            num_scalar_prefetch=4,
            grid=(n_live[0],),
            in_specs=[
                any_spec, any_spec, any_spec,
                pl.BlockSpec((1, 1, 2 * D_FF), lambda it, ie, r0, nr, nl: (ie[it], 0, 0)),
                pl.BlockSpec((1, 1, D_MODEL), lambda it, ie, r0, nr, nl: (ie[it], 0, 0)),
            ],
            out_specs=any_spec,
            scratch_shapes=[
                pltpu.VMEM((TM, D_MODEL // 2), jnp.uint32),
                pltpu.VMEM((TM, D_MODEL), BF16),
                pltpu.VMEM((TM, D_MODEL), F32),
                pltpu.VMEM((2, D_MODEL, TF), w_up.dtype),
                pltpu.VMEM((2, D_MODEL, TF), w_up.dtype),
                pltpu.VMEM((2, TF, D_MODEL), w_down.dtype),
                pltpu.SMEM((1,), jnp.int32),
                pltpu.SemaphoreType.DMA(()),
                pltpu.SemaphoreType.DMA(()),
                pltpu.SemaphoreType.DMA((2,)),
            ],
        ),
        out_shape=jax.ShapeDtypeStruct((xs.shape[0], D_MODEL), F32),
        compiler_params=_cparams(("arbitrary",)),
        name="moe",
    )(item_expert, item_row0, item_rows, n_live, xs, w_up, w_down,
      b_up.reshape(N_EXPERTS, 1, 2 * D_FF), b_down.reshape(N_EXPERTS, 1, D_MODEL))


def _combine_body(blocks_per_batch, n_prompt_blocks, t_pad, n_p, dest_ref, dest_next_ref, ys_hbm, h_hbm,
                  gt_ref, gf_ref, yp_ref, ysm_ref, gbuf, hbuf, gsem, hsem):
    i = pl.program_id(0)
    n_blocks = pl.num_programs(0)
    cur = i % 2

    def h_copy(blk, slot):
        row = jnp.where(blk < n_prompt_blocks,
                        (blk // blocks_per_batch) * t_pad + N_META + (blk % blocks_per_batch) * TB, n_p)
        return pltpu.make_async_copy(h_hbm.at[pl.ds(pl.multiple_of(row, 8), TB), :], hbuf.at[slot],
                                     hsem.at[slot])

    def row_copy(slot, buf, k, c, u):
        return pltpu.make_async_copy(ys_hbm.at[pl.ds(slot, 1), :], gbuf.at[buf, k, c, pl.ds(u, 1), :],
                                     gsem.at[buf])

    def request(table_ref, blk, buf):
        h_copy(blk, buf).start()

        def issue(c, carry):
            for u in range(8):
                for k in range(TOP_K):
                    row_copy(_slot_of(table_ref, c, u, k), buf, k, c, u).start(priority=k % 2)
            return carry

        for c in range(TB // 8):
            issue(c, 0)

    def drain(c, carry):
        for _ in range(8 * TOP_K):
            row_copy(0, cur, 0, 0, 0).wait()
        return carry

    @pl.when(i == 0)
    def _():
        request(dest_ref, 0, 0)

    for nxt in range(2):
        @pl.when((i + 1 < n_blocks) & (cur == 1 - nxt))
        def _():
            request(dest_next_ref, i + 1, nxt)

    h_copy(i, cur).wait()
    lax.fori_loop(0, TB // 8, drain, 0)
    y = hbuf[cur]
    for k in range(TOP_K):
        y = y + gbuf[cur, k].reshape(TB, D_MODEL) * gt_ref[:, k:k + 1]
    ms = jnp.mean(y * y, axis=-1, keepdims=True)
    y = y * lax.rsqrt(ms + EPS) * gf_ref[...]

    @pl.when(i < n_prompt_blocks)
    def _():
        yp_ref[...] = y

    @pl.when(i >= n_prompt_blocks)
    def _():
        ysm_ref[...] = y


def _combine(dest_blk, ys, h_all, gates_t, g_final, blocks_per_batch, n_prompt_blocks, t_pad, n_p):
    return pl.pallas_call(
        functools.partial(_combine_body, blocks_per_batch, n_prompt_blocks, t_pad, n_p),
        grid=(n_prompt_blocks + 1,),
        in_specs=[
            pl.BlockSpec(SLOT_BLOCK, lambda i: (i, 0, 0), memory_space=pltpu.SMEM),
            pl.BlockSpec(SLOT_BLOCK, lambda i: (jnp.minimum(i + 1, n_prompt_blocks), 0, 0),
                         memory_space=pltpu.SMEM),
            pl.BlockSpec(memory_space=pl.ANY),
            pl.BlockSpec(memory_space=pl.ANY),
            pl.BlockSpec((TB, TOP_K), lambda i: (i, 0)),
            pl.BlockSpec((1, D_MODEL), lambda i: (0, 0)),
        ],
        out_specs=[
            pl.BlockSpec((TB, D_MODEL), lambda i: (jnp.minimum(i, n_prompt_blocks - 1), 0)),
            pl.BlockSpec((TB, D_MODEL), lambda i: (0, 0)),
        ],
        out_shape=[jax.ShapeDtypeStruct((n_prompt_blocks * TB, D_MODEL), F32),
                   jax.ShapeDtypeStruct((TB, D_MODEL), F32)],
        scratch_shapes=[pltpu.VMEM((2, TOP_K, TB // 8, 8, D_MODEL), F32), pltpu.VMEM((2, TB, D_MODEL), F32),
                        pltpu.SemaphoreType.DMA((2,)), pltpu.SemaphoreType.DMA((2,))],
        compiler_params=_cparams(("arbitrary",)),
        name="combine",
    )(dest_blk, dest_blk, ys, h_all, gates_t, g_final.reshape(1, D_MODEL))


def _work_items(counts, cap, max_items):
    items_per = (counts + TM - 1) // TM
    item_end = jnp.cumsum(items_per)
    item_start = item_end - items_per
    n_live = item_end[-1]
    it = jnp.arange(max_items, dtype=jnp.int32)
    it_live = jnp.minimum(it, n_live - 1)
    expert = jnp.minimum(jnp.searchsorted(item_end, it_live, side="right"), N_EXPERTS - 1).astype(jnp.int32)
    part = it_live - item_start[expert]
    rows = jnp.clip(counts[expert] - part * TM, 0, TM)
    rows = jnp.where(it < n_live, rows, 0).astype(jnp.int32)
    row0 = (expert * cap + part * TM).astype(jnp.int32)
    return expert, row0, rows, n_live.reshape(1).astype(jnp.int32)


def kernel(x_prompt, x_sample, cache_k, cache_v, state_conv, meta_tokens, g_mix, w_in, conv_w, conv_b,
           conv_ln_g, conv_ln_b, g_attn_out, w_out, g_ffn, w_router, b_router, w_up, b_up, w_down,
           b_down, g_final):
    n_batch, seq, _ = x_prompt.shape
    n_streams, dec_seq, _ = x_sample.shape
    depth = w_in.shape[0]
    assert depth == 1 and n_streams * dec_seq == BLK
    t_p = N_META + seq
    t_pad = -(-t_p // BLK) * BLK
    n_p = n_batch * t_pad
    n_s = n_streams * dec_seq
    n_all = n_p + n_s
    assert n_all % ROW_A == 0 and n_all % ROW_E == 0 and n_all % ROW_R == 0 and n_all % TB == 0

    l = 0
    u_all, q_all, kvb, k_p, v_p, k_s, v_s, x_all = _inproj(
        x_prompt, x_sample.reshape(n_s, D_MODEL), meta_tokens.astype(F32), g_mix[l], w_in[l].astype(BF16), t_pad)
    c_all = _conv(u_all, state_conv[l], conv_w[l], conv_b[l], conv_ln_g[l], conv_ln_b[l],
                  t_pad // BLK, n_p // BLK)
    g_heads = g_attn_out[l].reshape(N_HEADS, HEAD_DIM)
    o_all = _attn_prompt(q_all, kvb, g_heads, n_batch, t_pad)
    o_all = _attn_sample(q_all, kvb, cache_k[l:l + 1], cache_v[l:l + 1], g_heads, o_all, n_p, dec_seq, BLK)
    h_all, xn_packed, logits_t = _outproj(x_all, c_all, o_all, w_out[l].astype(BF16), g_ffn[l],
                                          w_router[l].T.astype(BF16), b_router[l])

    n_valid = n_batch * t_p + n_s
    cap = -(-n_valid // CH) * CH
    n_slots = N_EXPERTS * cap + TB * TOP_K
    max_items = -(-(n_valid * TOP_K) // TM) + N_EXPERTS
    dest, gates, counts = _route(logits_t, t_p, t_pad, n_p, cap)
    xs = _dispatch(_slot_table(dest, n_all // TB), xn_packed, n_slots)
    item_expert, item_row0, item_rows, n_live = _work_items(counts[:, 0], cap, max_items)
    ys = _moe(xs, item_expert, item_row0, item_rows, n_live, w_up[l], b_up[l], w_down[l], b_down[l])

    def out_tokens(a):
        a_p = a[:, :n_p].reshape(-1, n_batch, t_pad)[:, :, N_META:t_p].reshape(-1, n_batch * seq)
        return jnp.concatenate([a_p, a[:, n_p:]], axis=1)

    assert seq % TB == 0 and n_s == TB
    n_out_blocks = n_batch * seq // TB
    dest_out = _slot_table(out_tokens(dest), n_out_blocks + 1)
    y_p, y_s = _combine(dest_out, ys, h_all, out_tokens(gates).T, g_final, seq // TB, n_out_blocks, t_pad, n_p)

    y_prompt = y_p.reshape(n_batch, seq, D_MODEL)
    y_sample = y_s.reshape(n_streams, dec_seq, D_MODEL)
    k_prompt = k_p[None]
    v_prompt = v_p[None]
    conv_prompt = jnp.stack([u_all[b * t_pad + t_p - (CONV_WIDTH - 1):b * t_pad + t_p]
                             for b in range(n_batch)])[None]
    k_sample = k_s.reshape(1, n_streams, dec_seq, N_HEADS, HEAD_DIM)
    v_sample = v_s.reshape(1, n_streams, dec_seq, N_HEADS, HEAD_DIM)
    u_s = u_all[n_p:].reshape(n_streams, dec_seq, D_CONV)
    conv_sample = jnp.concatenate([state_conv[l], u_s], axis=1)[:, -(CONV_WIDTH - 1):][None]
    return (y_prompt, y_sample, k_prompt, v_prompt, conv_prompt, k_sample, v_sample, conv_sample)
```

```python
import functools
import math

import jax
import jax.numpy as jnp
from jax import lax
from jax.experimental import pallas as pl
from jax.experimental.pallas import tpu as pltpu

F32 = jnp.float32
BF16 = jnp.bfloat16

D_MODEL = 2048
N_META = 16
D_CONV = 1024
N_HEADS = 8
HEAD_DIM = 128
D_ATTN = N_HEADS * HEAD_DIM
D_IN = 2 * D_CONV + 3 * D_ATTN
CONV_WIDTH = 31
N_EXPERTS = 32
TOP_K = 4
D_FF = D_MODEL
SWIGLU_ALPHA = 1.702
SWIGLU_LIMIT = 7.0
EPS = 1e-5

BLK = 256
HPS = 8
HALO = 32
COL = 512
ROW_A = 640
ROW_E = 640
ROW_R = 640
TM = 2048
CH = 128
SUB = 6 * CH
TF = 256
TB = 256
LOG_ZERO = -104.0
VMEM_LIMIT = 56 * 1024 * 1024


def _cparams(sem, vmem=VMEM_LIMIT):
    return pltpu.CompilerParams(dimension_semantics=sem, vmem_limit_bytes=vmem)


def _row_tile_pieces(tile, n_batch, seq, t_p, t_pad, n_s):
    lo, hi = tile * ROW_A, (tile + 1) * ROW_A
    segments = []
    for b in range(n_batch):
        base = b * t_pad
        segments += [("meta", b, base, base + N_META), ("prompt", b, base + N_META, base + t_p),
                     ("zero", b, base + t_p, base + t_pad)]
    segments.append(("sample", 0, n_batch * t_pad, n_batch * t_pad + n_s))
    copies, zeros = [], []
    for kind, b, s0, s1 in segments:
        a0, a1 = max(lo, s0), min(hi, s1)
        if a0 >= a1:
            continue
        if kind == "zero":
            zeros.append((a0 - lo, a1 - a0))
        else:
            copies.append((kind, b, a0 - s0, a0 - lo, a1 - a0))
    return copies, zeros


def _inproj_body(n_batch, seq, t_p, t_pad, n_p, n_s, xp_hbm, xs_hbm, meta_hbm, g_ref, wa_ref, wg_ref,
                 wq_ref, w1_ref, w2_ref, u_ref, q_ref, kvb_ref, kp_hbm, vp_hbm, ks_hbm, vs_hbm, xo_hbm,
                 xbuf, xn_s, kv4_s, xsem, osem, ksem, vsem):
    i = pl.program_id(0)
    j = pl.program_id(1)
    n_tiles = (n_p + n_s) // ROW_A
    heads_per_tile = COL // HEAD_DIM
    k4_s, v4_s = kv4_s.at[0], kv4_s.at[1]

    def tile_copies(tile):
        slot = tile % 2
        out = []
        for kind, b, src_row, dst_row, rows in _row_tile_pieces(tile, n_batch, seq, t_p, t_pad, n_s)[0]:
            src = {"meta": meta_hbm, "prompt": xp_hbm.at[b], "sample": xs_hbm}[kind]
            out.append(pltpu.make_async_copy(src.at[pl.ds(src_row, rows)],
                                             xbuf.at[slot, pl.ds(dst_row, rows)], xsem.at[slot]))
        return out

    def request_tile(tile):
        for r0, rows in _row_tile_pieces(tile, n_batch, seq, t_p, t_pad, n_s)[1]:
            xbuf[tile % 2, r0:r0 + rows, :] = jnp.zeros((rows, D_MODEL), F32)
        for cp in tile_copies(tile):
            cp.start()

    def x_out_copy(tile):
        return pltpu.make_async_copy(xbuf.at[tile % 2], xo_hbm.at[pl.ds(tile * ROW_A, ROW_A)], osem.at[tile % 2])

    def chunk_copies(tile, src, dst_p, dst_s, sem):
        out = []
        per_batch, full, rem = t_pad // CH, t_p // CH, t_p % CH
        for c in range(ROW_A // CH):
            g = tile * (ROW_A // CH) + c
            is_p = g < n_p // CH
            b, wi = g // per_batch, g % per_batch
            out.append((is_p & (wi < full), pltpu.make_async_copy(
                src.at[pl.ds(c * CH, CH)], dst_p.at[b, pl.ds(pl.multiple_of(wi * CH, CH), CH)], sem)))
            if rem:
                out.append((is_p & (wi == full), pltpu.make_async_copy(
                    src.at[pl.ds(c * CH, rem)], dst_p.at[b, pl.ds(full * CH, rem)], sem)))
            out.append((jnp.logical_not(is_p), pltpu.make_async_copy(
                src.at[pl.ds(c * CH, CH)],
                dst_s.at[pl.ds(pl.multiple_of((g - n_p // CH) * CH, CH), CH)], sem)))
        return out

    def start_all(copies):
        for cond, cp in copies:
            @pl.when(cond)
            def _():
                cp.start()

    def wait_all(copies):
        for cond, cp in copies:
            @pl.when(cond)
            def _():
                cp.wait()

    def store_heads(dst, p, first_head):
        for hh in range(heads_per_tile):
            dst[:, first_head + hh, :] = p[:, hh * HEAD_DIM:(hh + 1) * HEAD_DIM]

    @pl.when(j == 0)
    def _():
        @pl.when(i == 0)
        def _():
            request_tile(0)

        for t in range(n_tiles):
            @pl.when(i == t)
            def _():
                for cp in tile_copies(t):
                    cp.wait()
                if t >= 1:
                    x_out_copy(t - 1).wait()
                if t + 1 < n_tiles:
                    request_tile(t + 1)
                x_out_copy(t).start()

        x = xbuf[i % 2]
        ms = jnp.mean(x * x, axis=-1, keepdims=True)
        xn_s[...] = (x * lax.rsqrt(ms + EPS) * g_ref[...]).astype(BF16)

        @pl.when(i > 0)
        def _():
            wait_all(chunk_copies(i - 1, k4_s, kp_hbm, ks_hbm, ksem))

    @pl.when((j == 1) & (i > 0))
    def _():
        wait_all(chunk_copies(i - 1, v4_s, vp_hbm, vs_hbm, vsem))

    xn = xn_s[...]
    proj = lambda w_ref: jnp.dot(xn, w_ref[...], preferred_element_type=F32)
    u_ref[...] = proj(wa_ref) * jax.nn.sigmoid(proj(wg_ref))
    q_ref[...] = (proj(wq_ref) * (1.0 / math.sqrt(HEAD_DIM))).astype(BF16)
    for t, w_ref in enumerate((w1_ref, w2_ref)):
        p = proj(w_ref)
        kvb_ref[0, :, t * COL:(t + 1) * COL] = p.astype(BF16)
        store_heads(kv4_s.at[j], p, t * heads_per_tile)

    @pl.when(j == 0)
    def _():
        start_all(chunk_copies(i, k4_s, kp_hbm, ks_hbm, ksem))

    @pl.when(j == 1)
    def _():
        start_all(chunk_copies(i, v4_s, vp_hbm, vs_hbm, vsem))

        @pl.when(i == pl.num_programs(0) - 1)
        def _():
            wait_all(chunk_copies(i, k4_s, kp_hbm, ks_hbm, ksem))
            wait_all(chunk_copies(i, v4_s, vp_hbm, vs_hbm, vsem))
            x_out_copy(n_tiles - 1).wait()


def _inproj(x_prompt, x_sample2d, meta, g_mix, w_in_bf, t_pad):
    n_batch, seq, _ = x_prompt.shape
    n_s = x_sample2d.shape[0]
    t_p = meta.shape[0] + seq
    n_p = n_batch * t_pad
    n = n_p + n_s
    assert D_IN == 10 * COL and COL == 4 * HEAD_DIM and ROW_A % CH == 0 and t_pad % CH == 0
    assert n % ROW_A == 0 and meta.shape[0] == N_META and N_META % 8 == 0 and seq % 8 == 0

    def w_spec(first, stride):
        return pl.BlockSpec((D_MODEL, COL), lambda i, j: (0, first + stride * j))

    wide = lambda dt: jax.ShapeDtypeStruct((n, D_CONV), dt)
    cache_p = jax.ShapeDtypeStruct((n_batch, t_p, N_HEADS, HEAD_DIM), F32)
    cache_s = jax.ShapeDtypeStruct((n - n_p, N_HEADS, HEAD_DIM), F32)
    any_spec = pl.BlockSpec(memory_space=pl.ANY)
    half_spec = pl.BlockSpec((ROW_A, COL), lambda i, j: (i, j))
    return pl.pallas_call(
        functools.partial(_inproj_body, n_batch, seq, t_p, t_pad, n_p, n_s),
        grid=(n // ROW_A, 2),
        in_specs=[
            any_spec, any_spec, any_spec,
            pl.BlockSpec((1, D_MODEL), lambda i, j: (0, 0)),
            w_spec(0, 1), w_spec(2, 1), w_spec(4, 1), w_spec(6, 2), w_spec(7, 2),
        ],
        out_specs=[half_spec, half_spec, pl.BlockSpec((1, ROW_A, D_ATTN), lambda i, j: (j, i, 0)),
                   any_spec, any_spec, any_spec, any_spec, any_spec],
        out_shape=[wide(F32), wide(BF16), jax.ShapeDtypeStruct((2, n, D_ATTN), BF16),
                   cache_p, cache_p, cache_s, cache_s, jax.ShapeDtypeStruct((n, D_MODEL), F32)],
        scratch_shapes=[pltpu.VMEM((2, ROW_A, D_MODEL), F32),
                        pltpu.VMEM((ROW_A, D_MODEL), BF16),
                        pltpu.VMEM((2, ROW_A, N_HEADS, HEAD_DIM), F32),
                        pltpu.SemaphoreType.DMA((2,)), pltpu.SemaphoreType.DMA((2,)),
                        pltpu.SemaphoreType.DMA(()), pltpu.SemaphoreType.DMA(())],
        compiler_params=_cparams(("arbitrary", "arbitrary")),
        name="inproj",
    )(x_prompt, x_sample2d, meta, g_mix.reshape(1, D_MODEL), w_in_bf, w_in_bf, w_in_bf, w_in_bf, w_in_bf)


def _conv_taps(ext_ref, w_ref, b_ref, dst_ref, src_row0, dst_row0, nrows):
    def chunk(cc, carry):
        lanes = pl.ds(pl.multiple_of(cc * 128, 128), 128)
        acc = jnp.broadcast_to(b_ref[:, lanes], (nrows, 128))
        for j in range(CONV_WIDTH):
            acc = acc + ext_ref[pl.ds(src_row0 + j, nrows), lanes] * w_ref[pl.ds(j, 1), lanes]
        dst_ref[pl.ds(dst_row0, nrows), lanes] = acc
        return carry

    lax.fori_loop(0, D_CONV // 128, chunk, 0)


def _conv_block(ext_ref, sh_ref, w_ref, b_ref, dst_ref, off):
    span = HALO + BLK

    def chunk(cc, carry):
        lanes = pl.ds(pl.multiple_of(cc * 128, 128), 128)
        for r in range(8):
            n = span - (8 if r else 0)
            sh_ref[r, 0:n, :] = ext_ref[pl.ds(r, n), lanes]
        for rc in range(BLK // 64):
            acc = jnp.broadcast_to(b_ref[:, lanes], (64, 128))
            for j in range(CONV_WIDTH):
                r = (off + j) % 8
                acc = acc + sh_ref[r, pl.ds(off + j - r + rc * 64, 64), :] * w_ref[pl.ds(j, 1), lanes]
            dst_ref[pl.ds(rc * 64, 64), lanes] = acc
        return carry

    lax.fori_loop(0, D_CONV // 128, chunk, 0)


def _conv_body(blocks_per_batch, n_prompt_blocks, u_ref, prev_ref, state_ref, w_ref, b_ref,
               lg_ref, lb_ref, c_ref, ext_s, conv_s, sh_s):
    i = pl.program_id(0)
    off = HALO - (CONV_WIDTH - 1)

    @pl.when(i < n_prompt_blocks)
    def _():
        first = (i % blocks_per_batch) == 0
        ext_s[0:HALO, :] = jnp.where(first, 0.0, prev_ref[...])
        ext_s[HALO:HALO + BLK, :] = u_ref[...]
        _conv_block(ext_s, sh_s, w_ref, b_ref, conv_s, off)

    @pl.when(i >= n_prompt_blocks)
    def _():
        def stream(s, carry):
            r0 = pl.multiple_of(s * 16, 16)
            ext_s[off:HALO, :] = state_ref[s]
            ext_s[HALO:HALO + 16, :] = u_ref[pl.ds(r0, 16), :]
            _conv_taps(ext_s, w_ref, b_ref, conv_s, off, r0, 16)
            return carry

        lax.fori_loop(0, BLK // 16, stream, 0)

    x = conv_s[...]
    mu = jnp.mean(x, axis=-1, keepdims=True)
    xc = x - mu
    var = jnp.mean(xc * xc, axis=-1, keepdims=True)
    y = xc * lax.rsqrt(var + EPS) * lg_ref[...] + lb_ref[...]
    c_ref[...] = (y * jax.nn.sigmoid(y)).astype(BF16)


def _conv(u_all, state_conv, conv_w, conv_b, ln_g, ln_b, blocks_per_batch, n_prompt_blocks):
    n = u_all.shape[0]
    row = lambda a: a.reshape(1, D_CONV)
    const = lambda shape: pl.BlockSpec(shape, lambda i: (0,) * len(shape))
    return pl.pallas_call(
        functools.partial(_conv_body, blocks_per_batch, n_prompt_blocks),
        grid=(n // BLK,),
        in_specs=[
            pl.BlockSpec((BLK, D_CONV), lambda i: (i, 0)),
            pl.BlockSpec((HALO, D_CONV), lambda i: (jnp.maximum(i * (BLK // HALO) - 1, 0), 0)),
            const(state_conv.shape),
            const((CONV_WIDTH, D_CONV)),
            const((1, D_CONV)), const((1, D_CONV)), const((1, D_CONV)),
        ],
        out_specs=pl.BlockSpec((BLK, D_CONV), lambda i: (i, 0)),
        out_shape=jax.ShapeDtypeStruct((n, D_CONV), BF16),
        scratch_shapes=[pltpu.VMEM((HALO + BLK, D_CONV), F32), pltpu.VMEM((BLK, D_CONV), F32),
                        pltpu.VMEM((8, HALO + BLK, 128), F32)],
        compiler_params=_cparams(("arbitrary",)),
        name="conv",
    )(u_all, u_all, state_conv, conv_w, row(conv_b), row(ln_g), row(ln_b))


def _suffix_matrix(n, extra):
    r = lax.broadcasted_iota(jnp.int32, (n, n + extra), 0)
    c = lax.broadcasted_iota(jnp.int32, (n, n + extra), 1)
    return jnp.where((r > c) | (c >= n), 1.0, 0.0).astype(BF16)


def _split_dot(x, m):
    hi = x.astype(BF16)
    lo = (x - hi.astype(F32)).astype(BF16)
    return (jnp.dot(hi, m, preferred_element_type=F32) + jnp.dot(lo, m, preferred_element_type=F32))


def _log_keep(z):
    return -(jnp.maximum(z, 0.0) + jnp.log(1.0 + jnp.exp(-jnp.abs(z))))


def _attn_prompt_body(q_ref, k_ref, v_ref, g_ref, o_ref, m_s):
    hg = pl.program_id(1)
    qi = pl.program_id(2)

    @pl.when((pl.program_id(0) == 0) & (hg == 0) & (qi == 0))
    def _():
        m_s[...] = _suffix_matrix(BLK, 128)

    m = m_s[...]

    def tile(hh, kb, carry, diagonal):
        lanes = slice(hh * HEAD_DIM, (hh + 1) * HEAD_DIM)
        rows = pl.ds(pl.multiple_of(kb * BLK, BLK), BLK)
        z = lax.dot_general(q_ref[:, lanes], k_ref[rows, lanes], (((1,), (1,)), ((), ())),
                            preferred_element_type=F32)
        lk = _log_keep(z)
        if diagonal:
            keep = (lax.broadcasted_iota(jnp.int32, (BLK, BLK), 1)
                    < lax.broadcasted_iota(jnp.int32, (BLK, BLK), 0))
            lk = jnp.where(keep, lk, 0.0)
        cs = _split_dot(lk, m)
        after = cs[:, :BLK]
        if carry is not None:
            after = after + jnp.concatenate([carry, carry], axis=1)
        w = jnp.exp(z + lk + after)
        if diagonal:
            w = jnp.where(keep, w, 0.0)
        pv = jnp.dot(w.astype(BF16), v_ref[rows, lanes], preferred_element_type=F32)
        return pv, cs[:, BLK:]

    first = [tile(hh, qi, None, True) for hh in range(HPS)]

    def cond(st):
        kb, _, carries = st
        live = jnp.max(carries[0])
        for c in carries[1:]:
            live = jnp.maximum(live, jnp.max(c))
        return (kb >= 0) & (live > LOG_ZERO)

    def step(st):
        kb, accs, carries = st
        out = [tile(hh, kb, carries[hh], False) for hh in range(HPS)]
        return (kb - 1, tuple(a + o[0] for a, o in zip(accs, out)),
                tuple(c + o[1] for c, o in zip(carries, out)))

    _, accs, _ = lax.while_loop(cond, step, (qi - 1, tuple(f[0] for f in first), tuple(f[1] for f in first)))
    for hh in range(HPS):
        acc = accs[hh]
        ms = jnp.mean(acc * acc, axis=-1, keepdims=True)
        o_ref[:, hh * HEAD_DIM:(hh + 1) * HEAD_DIM] = (
            acc * lax.rsqrt(ms + EPS) * g_ref[pl.ds(hg * HPS + hh, 1), :]).astype(BF16)


def _attn_prompt(q_all, kvb, g_heads, n_batch, t_pad):
    nq = t_pad // BLK
    wide = HPS * HEAD_DIM
    return pl.pallas_call(
        _attn_prompt_body,
        grid=(n_batch, N_HEADS // HPS, nq),
        in_specs=[
            pl.BlockSpec((BLK, wide), lambda b, h, i: (b * nq + i, h)),
            pl.BlockSpec((None, t_pad, wide), lambda b, h, i: (0, b, h)),
            pl.BlockSpec((None, t_pad, wide), lambda b, h, i: (1, b, h)),
            pl.BlockSpec((N_HEADS, HEAD_DIM), lambda b, h, i: (0, 0)),
        ],
        out_specs=pl.BlockSpec((BLK, wide), lambda b, h, i: (b * nq + i, h)),
        out_shape=jax.ShapeDtypeStruct((q_all.shape[0], D_ATTN), BF16),
        scratch_shapes=[pltpu.VMEM((BLK, BLK + 128), BF16)],
        compiler_params=_cparams(("arbitrary", "arbitrary", "arbitrary")),
        name="attn_prompt",
    )(q_all, kvb, kvb, g_heads)


def _attn_sample_body(n_kb, kblk, q_ref, kn_ref, vn_ref, kc_hbm, vc_hbm, g_ref, o_in_hbm, o_ref,
                      kbuf, vbuf, qbd_s, acc_s, carry_s, m_s, mn_s, ksem, vsem):
    s = pl.program_id(0)
    n_streams = pl.num_programs(0)
    dq = q_ref.shape[0]
    last = n_kb - 1

    def cache_copies(stream, jb, slot):
        rows = pl.ds(pl.multiple_of(jb * kblk, kblk), kblk)
        return (pltpu.make_async_copy(kc_hbm.at[0, stream, rows], kbuf.at[slot], ksem.at[slot]),
                pltpu.make_async_copy(vc_hbm.at[0, stream, rows], vbuf.at[slot], vsem.at[slot]))

    def start(stream, jb, slot):
        for cp in cache_copies(stream, jb, slot):
            cp.start()

    def wait(stream, jb, slot):
        for cp in cache_copies(stream, jb, slot):
            cp.wait()

    def cat_heads(buf, slot):
        return jnp.concatenate([buf[slot, :, hh, :] for hh in range(N_HEADS)], axis=1).astype(BF16)

    def scores(kcat):
        return jnp.dot(kcat, qbd_s[...], preferred_element_type=F32)

    def add_values(w, vcat):
        full = lax.dot_general(w.astype(BF16), vcat, (((0,), (0,)), ((), ())), preferred_element_type=F32)
        for hh in range(N_HEADS):
            acc_s[hh] += full[hh * dq:(hh + 1) * dq, hh * HEAD_DIM:(hh + 1) * HEAD_DIM]

    @pl.when(s == 0)
    def _():
        start(0, last, last % 2)
        r = lax.broadcasted_iota(jnp.int32, (kblk, kblk), 0)
        c = lax.broadcasted_iota(jnp.int32, (kblk, kblk), 1)
        m_s[...] = jnp.where(c > r, 1.0, 0.0).astype(BF16)
        mn_s[...] = m_s[0:dq, 0:dq]

    qrep = jnp.concatenate([q_ref[...].astype(F32)] * N_HEADS, axis=0)
    qt = qrep.T
    row_head = lax.broadcasted_iota(jnp.int32, (D_ATTN, 128), 0) // HEAD_DIM
    col_head = lax.broadcasted_iota(jnp.int32, (D_ATTN, 128), 1) // dq
    qbd_s[...] = jnp.where(row_head == col_head, qt, 0.0).astype(BF16)
    acc_s[...] = jnp.zeros_like(acc_s)
    z = scores(kn_ref[...])
    lk = _log_keep(z)
    key = lax.broadcasted_iota(jnp.int32, (dq, 128), 0)
    qry = lax.broadcasted_iota(jnp.int32, (dq, 128), 1) % dq
    keep = key < qry
    lk = jnp.where(keep, lk, 0.0)
    after = _split_dot_left(mn_s[...], lk)
    w = jnp.where(keep, jnp.exp(z + lk + after), 0.0)
    add_values(w, vn_ref[...])
    carry_s[...] = jnp.sum(lk, axis=0, keepdims=True)

    def cond(jb):
        return (jb >= 0) & (jnp.max(carry_s[...]) > LOG_ZERO)

    def step(jb):
        slot = jb % 2
        wait(s, jb, slot)

        @pl.when(jb > 0)
        def _():
            start(s, jb - 1, 1 - slot)

        z = scores(cat_heads(kbuf, slot))
        lk = _log_keep(z)
        after = _split_dot_left(m_s[...], lk) + carry_s[...]
        w = jnp.exp(z + lk + after)
        add_values(w, cat_heads(vbuf, slot))
        carry_s[...] += jnp.sum(lk, axis=0, keepdims=True)
        return jb - 1

    jb_end = lax.while_loop(cond, step, last)

    @pl.when(jb_end >= 0)
    def _():
        wait(s, jb_end, jb_end % 2)

    @pl.when(s + 1 < n_streams)
    def _():
        start(s + 1, last, last % 2)

    for hh in range(N_HEADS):
        a = acc_s[hh]
        ms = jnp.mean(a * a, axis=-1, keepdims=True)
        o_ref[:, hh * HEAD_DIM:(hh + 1) * HEAD_DIM] = (
            a * lax.rsqrt(ms + EPS) * g_ref[pl.ds(hh, 1), :]).astype(BF16)


def _split_dot_left(m, x):
    hi = x.astype(BF16)
    lo = (x - hi.astype(F32)).astype(BF16)
    return (jnp.dot(m, hi, preferred_element_type=F32) + jnp.dot(m, lo, preferred_element_type=F32))


def _attn_sample(q_all, kvb, cache_k, cache_v, g_heads, o_all, row0, dec_seq, kblk):
    n_streams = cache_k.shape[1]
    past = cache_k.shape[2]
    n_kb = past // kblk
    assert n_kb * kblk == past and row0 % dec_seq == 0
    row_spec = pl.BlockSpec((dec_seq, D_ATTN), lambda s: (row0 // dec_seq + s, 0))
    kv_spec = lambda which: pl.BlockSpec((None, dec_seq, D_ATTN), lambda s: (which, row0 // dec_seq + s, 0))
    cache_buf = pltpu.VMEM((2, kblk, N_HEADS, HEAD_DIM), cache_k.dtype)
    any_spec = pl.BlockSpec(memory_space=pl.ANY)
    return pl.pallas_call(
        functools.partial(_attn_sample_body, n_kb, kblk),
        grid=(n_streams,),
        in_specs=[row_spec, kv_spec(0), kv_spec(1), any_spec, any_spec,
                  pl.BlockSpec((N_HEADS, HEAD_DIM), lambda s: (0, 0)), any_spec],
        out_specs=row_spec,
        out_shape=jax.ShapeDtypeStruct(o_all.shape, o_all.dtype),
        input_output_aliases={6: 0},
        scratch_shapes=[
            cache_buf, cache_buf,
            pltpu.VMEM((D_ATTN, 128), BF16),
            pltpu.VMEM((N_HEADS, dec_seq, HEAD_DIM), F32),
            pltpu.VMEM((1, 128), F32),
            pltpu.VMEM((kblk, kblk), BF16),
            pltpu.VMEM((dec_seq, dec_seq), BF16),
            pltpu.SemaphoreType.DMA((2,)), pltpu.SemaphoreType.DMA((2,)),
        ],
        compiler_params=_cparams(("arbitrary",)),
        name="attn_sample",
    )(q_all, kvb, kvb, cache_k, cache_v, g_heads, o_all)


def _outproj_body(x_ref, c_ref, o_ref, w_ref, g_ref, wr_ref, br_ref, h_ref, xn_ref, lg_ref):
    h = (x_ref[...]
         + jnp.dot(c_ref[...], w_ref[0:D_CONV, :], preferred_element_type=F32)
         + jnp.dot(o_ref[...], w_ref[D_CONV:, :], preferred_element_type=F32))
    h_ref[...] = h
    ms = jnp.mean(h * h, axis=-1, keepdims=True)
    xn = (h * lax.rsqrt(ms + EPS) * g_ref[...]).astype(BF16)
    bits = lax.bitcast_convert_type(xn.astype(F32), jnp.uint32)
    xn_ref[...] = (bits[:, :D_MODEL // 2] >> 16) | bits[:, D_MODEL // 2:]
    lg_ref[...] = lax.dot_general(wr_ref[...], xn, (((1,), (1,)), ((), ())),
                                  preferred_element_type=F32) + br_ref[...]


def _outproj(x_all, c_all, o_all, w_out_bf, g_ffn, w_router_t_bf, b_router):
    n = x_all.shape[0]
    const = lambda shape: pl.BlockSpec(shape, lambda i: (0,) * len(shape))
    return pl.pallas_call(
        _outproj_body,
        grid=(n // ROW_E,),
        in_specs=[
            pl.BlockSpec((ROW_E, D_MODEL), lambda i: (i, 0)),
            pl.BlockSpec((ROW_E, D_CONV), lambda i: (i, 0)),
            pl.BlockSpec((ROW_E, D_ATTN), lambda i: (i, 0)),
            const((D_MODEL, D_MODEL)),
            const((1, D_MODEL)),
            const((N_EXPERTS, D_MODEL)),
            const((N_EXPERTS, 1)),
        ],
        out_specs=[
            pl.BlockSpec((ROW_E, D_MODEL), lambda i: (i, 0)),
            pl.BlockSpec((ROW_E, D_MODEL // 2), lambda i: (i, 0)),
            pl.BlockSpec((N_EXPERTS, ROW_E), lambda i: (0, i)),
        ],
        out_shape=[jax.ShapeDtypeStruct((n, D_MODEL), F32), jax.ShapeDtypeStruct((n, D_MODEL // 2), jnp.uint32),
                   jax.ShapeDtypeStruct((N_EXPERTS, n), F32)],
        compiler_params=_cparams(("arbitrary",)),
        name="outproj",
    )(x_all, c_all, o_all, w_out_bf, g_ffn.reshape(1, D_MODEL), w_router_t_bf,
      b_router.reshape(N_EXPERTS, 1))


def _route_body(t_p, t_pad, n_p, cap, lg_ref, d_ref, gt_ref, cnt_ref, tri_s, run_s):
    i = pl.program_id(0)
    tb = lg_ref.shape[1]

    @pl.when(i == 0)
    def _():
        r = lax.broadcasted_iota(jnp.int32, (tb, tb + 128), 0)
        c = lax.broadcasted_iota(jnp.int32, (tb, tb + 128), 1)
        tri_s[...] = jnp.where((r < c) | (c >= tb), 1.0, 0.0).astype(BF16)
        run_s[...] = jnp.zeros_like(run_s)

    tok = i * tb + lax.broadcasted_iota(jnp.int32, (1, tb), 1)
    valid = ((tok % t_pad) < t_p) | (tok >= n_p)
    eid = lax.broadcasted_iota(jnp.int32, (N_EXPERTS, tb), 0)
    lg = lg_ref[...]
    sel = jnp.zeros((N_EXPERTS, tb), F32)
    hot, top = [], []
    for _ in range(TOP_K):
        mx = jnp.max(lg, axis=0, keepdims=True)
        idx = jnp.min(jnp.where(lg == mx, eid, N_EXPERTS), axis=0, keepdims=True)
        one = eid == idx
        lg = jnp.where(one, -jnp.inf, lg)
        hot.append(one)
        top.append(mx)
        sel = sel + jnp.where(one & valid, 1.0, 0.0)
    ex = [jnp.exp(t - top[0]) for t in top]
    den = ex[0] + ex[1] + ex[2] + ex[3]
    cs = jnp.dot(sel.astype(BF16), tri_s[...], preferred_element_type=F32)
    run = run_s[...]
    slot = (cs[:, :tb] + jnp.concatenate([run] * (tb // 128), axis=1)
            + (eid * cap).astype(F32))
    for k in range(TOP_K):
        mine = jnp.sum(jnp.where(hot[k], slot, 0.0), axis=0, keepdims=True).astype(jnp.int32)
        trash = N_EXPERTS * cap + (tok % TB) * TOP_K + k
        d_ref[pl.ds(k, 1), :] = jnp.where(valid, mine, trash)
        gt_ref[pl.ds(k, 1), :] = ex[k] / den
    run = run + cs[:, tb:]
    run_s[...] = run
    cnt_ref[...] = run.astype(jnp.int32)


def _route(logits_t, t_p, t_pad, n_p, cap):
    n = logits_t.shape[1]
    blk = lambda: pl.BlockSpec((TOP_K, ROW_R), lambda i: (0, i))
    return pl.pallas_call(
        functools.partial(_route_body, t_p, t_pad, n_p, cap),
        grid=(n // ROW_R,),
        in_specs=[pl.BlockSpec((N_EXPERTS, ROW_R), lambda i: (0, i))],
        out_specs=[blk(), blk(), pl.BlockSpec((N_EXPERTS, 128), lambda i: (0, 0))],
        out_shape=[jax.ShapeDtypeStruct((TOP_K, n), jnp.int32),
                   jax.ShapeDtypeStruct((TOP_K, n), F32), jax.ShapeDtypeStruct((N_EXPERTS, 128), jnp.int32)],
        scratch_shapes=[pltpu.VMEM((ROW_R, ROW_R + 128), BF16), pltpu.VMEM((N_EXPERTS, 128), F32)],
        compiler_params=_cparams(("arbitrary",)),
        name="route",
    )(logits_t)


def _slot_table(dest, n_blocks):
    return dest.T.reshape(n_blocks, TB // 8, 8 * TOP_K)


def _slot_of(dest_ref, i, u, k):
    return dest_ref[0, i, u * TOP_K + k]


SLOT_BLOCK = (1, TB // 8, 8 * TOP_K)


def _dispatch_body(dest_ref, xp_ref, xs_hbm, sem):
    def row_copy(i, u, slot):
        return pltpu.make_async_copy(xp_ref.at[i, pl.ds(u, 1), :], xs_hbm.at[pl.ds(slot, 1), :], sem)

    def issue(i, c):
        for u in range(8):
            for k in range(TOP_K):
                row_copy(i, u, _slot_of(dest_ref, i, u, k)).start(priority=k % 2)
        return c

    def drain(i, c):
        for _ in range(8 * TOP_K):
            row_copy(0, 0, 0).wait()
        return c

    for i in range(TB // 8):
        issue(i, 0)
    lax.fori_loop(0, TB // 8, drain, 0)


def _dispatch(dest_tab, xn_packed, n_slots):
    n, width = xn_packed.shape
    return pl.pallas_call(
        _dispatch_body,
        grid=(n // TB,),
        in_specs=[
            pl.BlockSpec(SLOT_BLOCK, lambda i: (i, 0, 0), memory_space=pltpu.SMEM),
            pl.BlockSpec((TB // 8, 8, width), lambda i: (i, 0, 0)),
        ],
        out_specs=pl.BlockSpec(memory_space=pl.ANY),
        out_shape=jax.ShapeDtypeStruct((n_slots, width), jnp.uint32),
        scratch_shapes=[pltpu.SemaphoreType.DMA(())],
        compiler_params=_cparams(("arbitrary",)),
        name="dispatch",
    )(dest_tab, xn_packed.reshape(n // 8, 8, width))


def _moe_body(ie_ref, r0_ref, nr_ref, nlive_ref, xs_hbm, wup_hbm, wdn_hbm, bup_ref, bd_ref,
              ys_hbm, xraw, xb16, yacc, wg_buf, wl_buf, wd_buf, pend_s, xsem, ysem, wsem):
    it = pl.program_id(0)
    n_live = nlive_ref[0]
    n_f = D_FF // TF
    nrows = nr_ref[it]

    def chunks(item):
        return (nr_ref[item] + CH - 1) // CH

    def x_copy(item, c):
        src = pl.multiple_of(r0_ref[item] + c * CH, CH)
        return pltpu.make_async_copy(xs_hbm.at[pl.ds(src, CH), :],
                                     xraw.at[pl.ds(pl.multiple_of(c * CH, CH), CH), :], xsem)

    def y_copy(item, c):
        dst = pl.multiple_of(r0_ref[item] + c * CH, CH)
        return pltpu.make_async_copy(yacc.at[pl.ds(pl.multiple_of(c * CH, CH), CH), :],
                                     ys_hbm.at[pl.ds(dst, CH), :], ysem)

    def w_copies(item, jj, slot):
        e = ie_ref[item]
        glu = pl.ds(pl.multiple_of(jj * TF, TF), TF)
        lin = pl.ds(pl.multiple_of(D_FF + jj * TF, TF), TF)
        return (pltpu.make_async_copy(wup_hbm.at[e, :, glu], wg_buf.at[slot], wsem.at[slot]),
                pltpu.make_async_copy(wup_hbm.at[e, :, lin], wl_buf.at[slot], wsem.at[slot]),
                pltpu.make_async_copy(wdn_hbm.at[e, glu, :], wd_buf.at[slot], wsem.at[slot]))

    def start_w(item, jj, slot):
        for cp in w_copies(item, jj, slot):
            cp.start()

    def for_chunks(n, fn):
        def body(c, carry):
            fn(c)
            return carry
        lax.fori_loop(0, n, body, 0)

    @pl.when(it == 0)
    def _():
        pend_s[0] = 0
        for_chunks(chunks(0), lambda c: x_copy(0, c).start())
        start_w(0, 0, 0)

    n_ch = chunks(it)
    for_chunks(n_ch, lambda c: x_copy(it, c).wait())

    def unpack(c):
        rows = pl.ds(pl.multiple_of(c * CH, CH), CH)
        p = xraw[rows, :]
        rid = c * CH + lax.broadcasted_iota(jnp.int32, (CH, 1), 0)
        live = rid < nrows
        lo = lax.bitcast_convert_type(p << 16, F32)
        hi = lax.bitcast_convert_type(p & jnp.uint32(0xFFFF0000), F32)
        xb16[rows, 0:D_MODEL // 2] = jnp.where(live, lo, 0.0).astype(BF16)
        xb16[rows, D_MODEL // 2:] = jnp.where(live, hi, 0.0).astype(BF16)

    for_chunks(n_ch, unpack)

    @pl.when(it + 1 < n_live)
    def _():
        for_chunks(chunks(it + 1), lambda c: x_copy(it + 1, c).start(priority=1))

    for_chunks(pend_s[0], lambda c: y_copy(it, 0).wait())
    n_full = n_ch // (SUB // CH)
    tail = n_ch - n_full * (SUB // CH)

    def hidden_tile(j, carry):
        slot = j % 2
        for cp in w_copies(it, j, slot):
            cp.wait()

        @pl.when(j + 1 < n_f)
        def _():
            start_w(it, j + 1, 1 - slot)

        @pl.when((j + 1 == n_f) & (it + 1 < n_live))
        def _():
            start_w(it + 1, 0, 1 - slot)

        bg = bup_ref[0, :, pl.ds(pl.multiple_of(j * TF, TF), TF)]
        bl = bup_ref[0, :, pl.ds(pl.multiple_of(D_FF + j * TF, TF), TF)]

        def ffn_rows(r0, n):
            rows = pl.ds(r0, n)
            x = xb16[rows, :]
            hg = jnp.dot(x, wg_buf[slot].astype(BF16), preferred_element_type=F32) + bg
            hl = jnp.dot(x, wl_buf[slot].astype(BF16), preferred_element_type=F32) + bl
            hg = jnp.minimum(hg, SWIGLU_LIMIT)
            hl = jnp.clip(hl, -SWIGLU_LIMIT, SWIGLU_LIMIT)
            a = hg * jax.nn.sigmoid(SWIGLU_ALPHA * hg) * (hl + 1.0)
            part = jnp.dot(a.astype(BF16), wd_buf[slot].astype(BF16), preferred_element_type=F32)
            yacc[rows, :] = jnp.where(j == 0, bd_ref[0], yacc[rows, :]) + part

            @pl.when(j == n_f - 1)
            def _():
                for cc in range(n // CH):
                    y_copy(it, r0 // CH + cc).start(priority=1)

        def full(s, c):
            ffn_rows(pl.multiple_of(s * SUB, CH), SUB)
            return c

        lax.fori_loop(0, n_full, full, 0)
        for t in range(1, SUB // CH):
            @pl.when(tail == t)
            def _():
                ffn_rows(pl.multiple_of(n_full * SUB, CH), t * CH)
        return carry

    lax.fori_loop(0, n_f, hidden_tile, 0)
    pend_s[0] = n_ch

    @pl.when(it == n_live - 1)
    def _():
        for_chunks(n_ch, lambda c: y_copy(it, c).wait())
        pend_s[0] = 0


def _moe(xs, item_expert, item_row0, item_rows, n_live, w_up, b_up, w_down, b_down):
    assert (D_FF // TF) % 2 == 0
    any_spec = pl.BlockSpec(memory_space=pl.ANY)
    return pl.pallas_call(
        _moe_body,
        grid_spec=pltpu.PrefetchScalarGridSpec(
            num_scalar_prefetch=4,
            grid=(n_live[0],),
            in_specs=[
                any_spec, any_spec, any_spec,
                pl.BlockSpec((1, 1, 2 * D_FF), lambda it, ie, r0, nr, nl: (ie[it], 0, 0)),
                pl.BlockSpec((1, 1, D_MODEL), lambda it, ie, r0, nr, nl: (ie[it], 0, 0)),
            ],
            out_specs=any_spec,
            scratch_shapes=[
                pltpu.VMEM((TM, D_MODEL // 2), jnp.uint32),
                pltpu.VMEM((TM, D_MODEL), BF16),
                pltpu.VMEM((TM, D_MODEL), F32),
                pltpu.VMEM((2, D_MODEL, TF), w_up.dtype),
                pltpu.VMEM((2, D_MODEL, TF), w_up.dtype),
                pltpu.VMEM((2, TF, D_MODEL), w_down.dtype),
                pltpu.SMEM((1,), jnp.int32),
                pltpu.SemaphoreType.DMA(()),
                pltpu.SemaphoreType.DMA(()),
                pltpu.SemaphoreType.DMA((2,)),
            ],
        ),
        out_shape=jax.ShapeDtypeStruct((xs.shape[0], D_MODEL), F32),
        compiler_params=_cparams(("arbitrary",)),
        name="moe",
    )(item_expert, item_row0, item_rows, n_live, xs, w_up, w_down,
      b_up.reshape(N_EXPERTS, 1, 2 * D_FF), b_down.reshape(N_EXPERTS, 1, D_MODEL))


def _combine_body(blocks_per_batch, n_prompt_blocks, t_pad, n_p, dest_ref, dest_next_ref, ys_hbm, h_hbm,
                  gt_ref, gf_ref, yp_ref, ysm_ref, gbuf, hbuf, gsem, hsem):
    i = pl.program_id(0)
    n_blocks = pl.num_programs(0)
    cur = i % 2

    def h_copy(blk, slot):
        row = jnp.where(blk < n_prompt_blocks,
                        (blk // blocks_per_batch) * t_pad + N_META + (blk % blocks_per_batch) * TB, n_p)
        return pltpu.make_async_copy(h_hbm.at[pl.ds(pl.multiple_of(row, 8), TB), :], hbuf.at[slot],
                                     hsem.at[slot])

    def row_copy(slot, buf, k, c, u):
        return pltpu.make_async_copy(ys_hbm.at[pl.ds(slot, 1), :], gbuf.at[buf, k, c, pl.ds(u, 1), :],
                                     gsem.at[buf])

    def request(table_ref, blk, buf):
        h_copy(blk, buf).start()

        def issue(c, carry):
            for u in range(8):
                for k in range(TOP_K):
                    row_copy(_slot_of(table_ref, c, u, k), buf, k, c, u).start(priority=k % 2)
            return carry

        for c in range(TB // 8):
            issue(c, 0)

    def drain(c, carry):
        for _ in range(8 * TOP_K):
            row_copy(0, cur, 0, 0, 0).wait()
        return carry

    @pl.when(i == 0)
    def _():
        request(dest_ref, 0, 0)

    for nxt in range(2):
        @pl.when((i + 1 < n_blocks) & (cur == 1 - nxt))
        def _():
            request(dest_next_ref, i + 1, nxt)

    h_copy(i, cur).wait()
    lax.fori_loop(0, TB // 8, drain, 0)
    y = hbuf[cur]
    for k in range(TOP_K):
        y = y + gbuf[cur, k].reshape(TB, D_MODEL) * gt_ref[:, k:k + 1]
    ms = jnp.mean(y * y, axis=-1, keepdims=True)
    y = y * lax.rsqrt(ms + EPS) * gf_ref[...]

    @pl.when(i < n_prompt_blocks)
    def _():
        yp_ref[...] = y

    @pl.when(i >= n_prompt_blocks)
    def _():
        ysm_ref[...] = y


def _combine(dest_blk, ys, h_all, gates_t, g_final, blocks_per_batch, n_prompt_blocks, t_pad, n_p):
    return pl.pallas_call(
        functools.partial(_combine_body, blocks_per_batch, n_prompt_blocks, t_pad, n_p),
        grid=(n_prompt_blocks + 1,),
        in_specs=[
            pl.BlockSpec(SLOT_BLOCK, lambda i: (i, 0, 0), memory_space=pltpu.SMEM),
            pl.BlockSpec(SLOT_BLOCK, lambda i: (jnp.minimum(i + 1, n_prompt_blocks), 0, 0),
                         memory_space=pltpu.SMEM),
            pl.BlockSpec(memory_space=pl.ANY),
            pl.BlockSpec(memory_space=pl.ANY),
            pl.BlockSpec((TB, TOP_K), lambda i: (i, 0)),
            pl.BlockSpec((1, D_MODEL), lambda i: (0, 0)),
        ],
        out_specs=[
            pl.BlockSpec((TB, D_MODEL), lambda i: (jnp.minimum(i, n_prompt_blocks - 1), 0)),
            pl.BlockSpec((TB, D_MODEL), lambda i: (0, 0)),
        ],
        out_shape=[jax.ShapeDtypeStruct((n_prompt_blocks * TB, D_MODEL), F32),
                   jax.ShapeDtypeStruct((TB, D_MODEL), F32)],
        scratch_shapes=[pltpu.VMEM((2, TOP_K, TB // 8, 8, D_MODEL), F32), pltpu.VMEM((2, TB, D_MODEL), F32),
                        pltpu.SemaphoreType.DMA((2,)), pltpu.SemaphoreType.DMA((2,))],
        compiler_params=_cparams(("arbitrary",)),
        name="combine",
    )(dest_blk, dest_blk, ys, h_all, gates_t, g_final.reshape(1, D_MODEL))


def _work_items(counts, cap, max_items):
    items_per = (counts + TM - 1) // TM
    item_end = jnp.cumsum(items_per)
    item_start = item_end - items_per
    n_live = item_end[-1]
    it = jnp.arange(max_items, dtype=jnp.int32)
    it_live = jnp.minimum(it, n_live - 1)
    expert = jnp.minimum(jnp.searchsorted(item_end, it_live, side="right"), N_EXPERTS - 1).astype(jnp.int32)
    part = it_live - item_start[expert]
    rows = jnp.clip(counts[expert] - part * TM, 0, TM)
    rows = jnp.where(it < n_live, rows, 0).astype(jnp.int32)
    row0 = (expert * cap + part * TM).astype(jnp.int32)
    return expert, row0, rows, n_live.reshape(1).astype(jnp.int32)


def kernel(x_prompt, x_sample, cache_k, cache_v, state_conv, meta_tokens, g_mix, w_in, conv_w, conv_b,
           conv_ln_g, conv_ln_b, g_attn_out, w_out, g_ffn, w_router, b_router, w_up, b_up, w_down,
           b_down, g_final):
    n_batch, seq, _ = x_prompt.shape
    n_streams, dec_seq, _ = x_sample.shape
    depth = w_in.shape[0]
    assert depth == 1 and n_streams * dec_seq == BLK
    t_p = N_META + seq
    t_pad = -(-t_p // BLK) * BLK
    n_p = n_batch * t_pad
    n_s = n_streams * dec_seq
    n_all = n_p + n_s
    assert n_all % ROW_A == 0 and n_all % ROW_E == 0 and n_all % ROW_R == 0 and n_all % TB == 0

    l = 0
    u_all, q_all, kvb, k_p, v_p, k_s, v_s, x_all = _inproj(
        x_prompt, x_sample.reshape(n_s, D_MODEL), meta_tokens.astype(F32), g_mix[l], w_in[l].astype(BF16), t_pad)
    c_all = _conv(u_all, state_conv[l], conv_w[l], conv_b[l], conv_ln_g[l], conv_ln_b[l],
                  t_pad // BLK, n_p // BLK)
    g_heads = g_attn_out[l].reshape(N_HEADS, HEAD_DIM)
    o_all = _attn_prompt(q_all, kvb, g_heads, n_batch, t_pad)
    o_all = _attn_sample(q_all, kvb, cache_k[l:l + 1], cache_v[l:l + 1], g_heads, o_all, n_p, dec_seq, BLK)
    h_all, xn_packed, logits_t = _outproj(x_all, c_all, o_all, w_out[l].astype(BF16), g_ffn[l],
                                          w_router[l].T.astype(BF16), b_router[l])

    n_valid = n_batch * t_p + n_s
    cap = -(-n_valid // CH) * CH
    n_slots = N_EXPERTS * cap + TB * TOP_K
    max_items = -(-(n_valid * TOP_K) // TM) + N_EXPERTS
    dest, gates, counts = _route(logits_t, t_p, t_pad, n_p, cap)
    xs = _dispatch(_slot_table(dest, n_all // TB), xn_packed, n_slots)
    item_expert, item_row0, item_rows, n_live = _work_items(counts[:, 0], cap, max_items)
    ys = _moe(xs, item_expert, item_row0, item_rows, n_live, w_up[l], b_up[l], w_down[l], b_down[l])

    def out_tokens(a):
        a_p = a[:, :n_p].reshape(-1, n_batch, t_pad)[:, :, N_META:t_p].reshape(-1, n_batch * seq)
        return jnp.concatenate([a_p, a[:, n_p:]], axis=1)

    assert seq % TB == 0 and n_s == TB
    n_out_blocks = n_batch * seq // TB
    dest_out = _slot_table(out_tokens(dest), n_out_blocks + 1)
    y_p, y_s = _combine(dest_out, ys, h_all, out_tokens(gates).T, g_final, seq // TB, n_out_blocks, t_pad, n_p)

    y_prompt = y_p.reshape(n_batch, seq, D_MODEL)
    y_sample = y_s.reshape(n_streams, dec_seq, D_MODEL)
    k_prompt = k_p[None]
    v_prompt = v_p[None]
    conv_prompt = jnp.stack([u_all[b * t_pad + t_p - (CONV_WIDTH - 1):b * t_pad + t_p]
                             for b in range(n_batch)])[None]
    k_sample = k_s.reshape(1, n_streams, dec_seq, N_HEADS, HEAD_DIM)
    v_sample = v_s.reshape(1, n_streams, dec_seq, N_HEADS, HEAD_DIM)
    u_s = u_all[n_p:].reshape(n_streams, dec_seq, D_CONV)
    conv_sample = jnp.concatenate([state_conv[l], u_s], axis=1)[:, -(CONV_WIDTH - 1):][None]
    return (y_prompt, y_sample, k_prompt, v_prompt, conv_prompt, k_sample, v_sample, conv_sample)
```

```python
import functools
import math

import jax
import jax.numpy as jnp
from jax import lax
from jax.experimental import pallas as pl
from jax.experimental.pallas import tpu as pltpu

F32 = jnp.float32
BF16 = jnp.bfloat16

D_MODEL = 2048
N_META = 16
D_CONV = 1024
N_HEADS = 8
HEAD_DIM = 128
D_ATTN = N_HEADS * HEAD_DIM
D_IN = 2 * D_CONV + 3 * D_ATTN
CONV_WIDTH = 31
N_EXPERTS = 32
TOP_K = 4
D_FF = D_MODEL
SWIGLU_ALPHA = 1.702
SWIGLU_LIMIT = 7.0
EPS = 1e-5

BLK = 256
HPS = 8
HALO = 32
COL = 512
ROW_A = 640
ROW_E = 640
TM = 2048
CH = 128
SUB = 6 * CH
TF = 256
TB = 256
LOG_ZERO = -104.0
VMEM_LIMIT = 56 * 1024 * 1024


def _cparams(sem, vmem=VMEM_LIMIT):
    return pltpu.CompilerParams(dimension_semantics=sem, vmem_limit_bytes=vmem)


def _row_tile_pieces(tile, n_batch, seq, t_p, t_pad, n_s):
    lo, hi = tile * ROW_A, (tile + 1) * ROW_A
    segments = []
    for b in range(n_batch):
        base = b * t_pad
        segments += [("meta", b, base, base + N_META), ("prompt", b, base + N_META, base + t_p),
                     ("zero", b, base + t_p, base + t_pad)]
    segments.append(("sample", 0, n_batch * t_pad, n_batch * t_pad + n_s))
    copies, zeros = [], []
    for kind, b, s0, s1 in segments:
        a0, a1 = max(lo, s0), min(hi, s1)
        if a0 >= a1:
            continue
        if kind == "zero":
            zeros.append((a0 - lo, a1 - a0))
        else:
            copies.append((kind, b, a0 - s0, a0 - lo, a1 - a0))
    return copies, zeros


def _inproj_body(n_batch, seq, t_p, t_pad, n_p, n_s, xp_hbm, xs_hbm, meta_hbm, g_ref, wa_ref, wg_ref,
                 wq_ref, w1_ref, w2_ref, u_ref, q_ref, kvb_ref, kp_hbm, vp_hbm, ks_hbm, vs_hbm, xo_hbm,
                 xbuf, xn_s, kv4_s, xsem, osem, ksem, vsem):
    i = pl.program_id(0)
    j = pl.program_id(1)
    n_tiles = (n_p + n_s) // ROW_A
    heads_per_tile = COL // HEAD_DIM
    k4_s, v4_s = kv4_s.at[0], kv4_s.at[1]

    def tile_copies(tile):
        slot = tile % 2
        out = []
        for kind, b, src_row, dst_row, rows in _row_tile_pieces(tile, n_batch, seq, t_p, t_pad, n_s)[0]:
            src = {"meta": meta_hbm, "prompt": xp_hbm.at[b], "sample": xs_hbm}[kind]
            out.append(pltpu.make_async_copy(src.at[pl.ds(src_row, rows)],
                                             xbuf.at[slot, pl.ds(dst_row, rows)], xsem.at[slot]))
        return out

    def request_tile(tile):
        for r0, rows in _row_tile_pieces(tile, n_batch, seq, t_p, t_pad, n_s)[1]:
            xbuf[tile % 2, r0:r0 + rows, :] = jnp.zeros((rows, D_MODEL), F32)
        for cp in tile_copies(tile):
            cp.start()

    def x_out_copy(tile):
        return pltpu.make_async_copy(xbuf.at[tile % 2], xo_hbm.at[pl.ds(tile * ROW_A, ROW_A)], osem.at[tile % 2])

    def chunk_copies(tile, src, dst_p, dst_s, sem):
        out = []
        per_batch, full, rem = t_pad // CH, t_p // CH, t_p % CH
        for c in range(ROW_A // CH):
            g = tile * (ROW_A // CH) + c
            is_p = g < n_p // CH
            b, wi = g // per_batch, g % per_batch
            out.append((is_p & (wi < full), pltpu.make_async_copy(
                src.at[pl.ds(c * CH, CH)], dst_p.at[b, pl.ds(pl.multiple_of(wi * CH, CH), CH)], sem)))
            if rem:
                out.append((is_p & (wi == full), pltpu.make_async_copy(
                    src.at[pl.ds(c * CH, rem)], dst_p.at[b, pl.ds(full * CH, rem)], sem)))
            out.append((jnp.logical_not(is_p), pltpu.make_async_copy(
                src.at[pl.ds(c * CH, CH)],
                dst_s.at[pl.ds(pl.multiple_of((g - n_p // CH) * CH, CH), CH)], sem)))
        return out

    def start_all(copies):
        for cond, cp in copies:
            @pl.when(cond)
            def _():
                cp.start()

    def wait_all(copies):
        for cond, cp in copies:
            @pl.when(cond)
            def _():
                cp.wait()

    def store_heads(dst, p, first_head):
        for hh in range(heads_per_tile):
            dst[:, first_head + hh, :] = p[:, hh * HEAD_DIM:(hh + 1) * HEAD_DIM]

    @pl.when(j == 0)
    def _():
        @pl.when(i == 0)
        def _():
            request_tile(0)

        for t in range(n_tiles):
            @pl.when(i == t)
            def _():
                for cp in tile_copies(t):
                    cp.wait()
                if t >= 1:
                    x_out_copy(t - 1).wait()
                if t + 1 < n_tiles:
                    request_tile(t + 1)
                x_out_copy(t).start()

        x = xbuf[i % 2]
        ms = jnp.mean(x * x, axis=-1, keepdims=True)
        xn_s[...] = (x * lax.rsqrt(ms + EPS) * g_ref[...]).astype(BF16)

        @pl.when(i > 0)
        def _():
            wait_all(chunk_copies(i - 1, k4_s, kp_hbm, ks_hbm, ksem))

    @pl.when((j == 1) & (i > 0))
    def _():
        wait_all(chunk_copies(i - 1, v4_s, vp_hbm, vs_hbm, vsem))

    xn = xn_s[...]
    proj = lambda w_ref: jnp.dot(xn, w_ref[...], preferred_element_type=F32)
    u_ref[...] = proj(wa_ref) * jax.nn.sigmoid(proj(wg_ref))
    q_ref[...] = (proj(wq_ref) * (1.0 / math.sqrt(HEAD_DIM))).astype(BF16)
    for t, w_ref in enumerate((w1_ref, w2_ref)):
        p = proj(w_ref)
        kvb_ref[0, :, t * COL:(t + 1) * COL] = p.astype(BF16)
        store_heads(kv4_s.at[j], p, t * heads_per_tile)

    @pl.when(j == 0)
    def _():
        start_all(chunk_copies(i, k4_s, kp_hbm, ks_hbm, ksem))

    @pl.when(j == 1)
    def _():
        start_all(chunk_copies(i, v4_s, vp_hbm, vs_hbm, vsem))

        @pl.when(i == pl.num_programs(0) - 1)
        def _():
            wait_all(chunk_copies(i, k4_s, kp_hbm, ks_hbm, ksem))
            wait_all(chunk_copies(i, v4_s, vp_hbm, vs_hbm, vsem))
            x_out_copy(n_tiles - 1).wait()


def _inproj(x_prompt, x_sample2d, meta, g_mix, w_in_bf, t_pad):
    n_batch, seq, _ = x_prompt.shape
    n_s = x_sample2d.shape[0]
    t_p = meta.shape[0] + seq
    n_p = n_batch * t_pad
    n = n_p + n_s
    assert D_IN == 10 * COL and COL == 4 * HEAD_DIM and ROW_A % CH == 0 and t_pad % CH == 0
    assert n % ROW_A == 0 and meta.shape[0] == N_META and N_META % 8 == 0 and seq % 8 == 0

    def w_spec(first, stride):
        return pl.BlockSpec((D_MODEL, COL), lambda i, j: (0, first + stride * j))

    wide = lambda dt: jax.ShapeDtypeStruct((n, D_CONV), dt)
    cache_p = jax.ShapeDtypeStruct((n_batch, t_p, N_HEADS, HEAD_DIM), F32)
    cache_s = jax.ShapeDtypeStruct((n - n_p, N_HEADS, HEAD_DIM), F32)
    any_spec = pl.BlockSpec(memory_space=pl.ANY)
    half_spec = pl.BlockSpec((ROW_A, COL), lambda i, j: (i, j))
    return pl.pallas_call(
        functools.partial(_inproj_body, n_batch, seq, t_p, t_pad, n_p, n_s),
        grid=(n // ROW_A, 2),
        in_specs=[
            any_spec, any_spec, any_spec,
            pl.BlockSpec((1, D_MODEL), lambda i, j: (0, 0)),
            w_spec(0, 1), w_spec(2, 1), w_spec(4, 1), w_spec(6, 2), w_spec(7, 2),
        ],
        out_specs=[half_spec, half_spec, pl.BlockSpec((1, ROW_A, D_ATTN), lambda i, j: (j, i, 0)),
                   any_spec, any_spec, any_spec, any_spec, any_spec],
        out_shape=[wide(F32), wide(BF16), jax.ShapeDtypeStruct((2, n, D_ATTN), BF16),
                   cache_p, cache_p, cache_s, cache_s, jax.ShapeDtypeStruct((n, D_MODEL), F32)],
        scratch_shapes=[pltpu.VMEM((2, ROW_A, D_MODEL), F32),
                        pltpu.VMEM((ROW_A, D_MODEL), BF16),
                        pltpu.VMEM((2, ROW_A, N_HEADS, HEAD_DIM), F32),
                        pltpu.SemaphoreType.DMA((2,)), pltpu.SemaphoreType.DMA((2,)),
                        pltpu.SemaphoreType.DMA(()), pltpu.SemaphoreType.DMA(())],
        compiler_params=_cparams(("arbitrary", "arbitrary")),
        name="inproj",
    )(x_prompt, x_sample2d, meta, g_mix.reshape(1, D_MODEL), w_in_bf, w_in_bf, w_in_bf, w_in_bf, w_in_bf)


def _conv_taps(ext_ref, w_ref, b_ref, dst_ref, src_row0, dst_row0, nrows):
    def chunk(cc, carry):
        lanes = pl.ds(pl.multiple_of(cc * 128, 128), 128)
        acc = jnp.broadcast_to(b_ref[:, lanes], (nrows, 128))
        for j in range(CONV_WIDTH):
            acc = acc + ext_ref[pl.ds(src_row0 + j, nrows), lanes] * w_ref[pl.ds(j, 1), lanes]
        dst_ref[pl.ds(dst_row0, nrows), lanes] = acc
        return carry

    lax.fori_loop(0, D_CONV // 128, chunk, 0)


def _conv_block(ext_ref, sh_ref, w_ref, b_ref, dst_ref, off):
    span = HALO + BLK

    def chunk(cc, carry):
        lanes = pl.ds(pl.multiple_of(cc * 128, 128), 128)
        for r in range(8):
            n = span - (8 if r else 0)
            sh_ref[r, 0:n, :] = ext_ref[pl.ds(r, n), lanes]
        for rc in range(BLK // 64):
            acc = jnp.broadcast_to(b_ref[:, lanes], (64, 128))
            for j in range(CONV_WIDTH):
                r = (off + j) % 8
                acc = acc + sh_ref[r, pl.ds(off + j - r + rc * 64, 64), :] * w_ref[pl.ds(j, 1), lanes]
            dst_ref[pl.ds(rc * 64, 64), lanes] = acc
        return carry

    lax.fori_loop(0, D_CONV // 128, chunk, 0)


def _conv_body(blocks_per_batch, n_prompt_blocks, u_ref, prev_ref, state_ref, w_ref, b_ref,
               lg_ref, lb_ref, c_ref, ext_s, conv_s, sh_s):
    i = pl.program_id(0)
    off = HALO - (CONV_WIDTH - 1)

    @pl.when(i < n_prompt_blocks)
    def _():
        first = (i % blocks_per_batch) == 0
        ext_s[0:HALO, :] = jnp.where(first, 0.0, prev_ref[...])
        ext_s[HALO:HALO + BLK, :] = u_ref[...]
        _conv_block(ext_s, sh_s, w_ref, b_ref, conv_s, off)

    @pl.when(i >= n_prompt_blocks)
    def _():
        def stream(s, carry):
            r0 = pl.multiple_of(s * 16, 16)
            ext_s[off:HALO, :] = state_ref[s]
            ext_s[HALO:HALO + 16, :] = u_ref[pl.ds(r0, 16), :]
            _conv_taps(ext_s, w_ref, b_ref, conv_s, off, r0, 16)
            return carry

        lax.fori_loop(0, BLK // 16, stream, 0)

    x = conv_s[...]
    mu = jnp.mean(x, axis=-1, keepdims=True)
    xc = x - mu
    var = jnp.mean(xc * xc, axis=-1, keepdims=True)
    y = xc * lax.rsqrt(var + EPS) * lg_ref[...] + lb_ref[...]
    c_ref[...] = (y * jax.nn.sigmoid(y)).astype(BF16)


def _conv(u_all, state_conv, conv_w, conv_b, ln_g, ln_b, blocks_per_batch, n_prompt_blocks):
    n = u_all.shape[0]
    row = lambda a: a.reshape(1, D_CONV)
    const = lambda shape: pl.BlockSpec(shape, lambda i: (0,) * len(shape))
    return pl.pallas_call(
        functools.partial(_conv_body, blocks_per_batch, n_prompt_blocks),
        grid=(n // BLK,),
        in_specs=[
            pl.BlockSpec((BLK, D_CONV), lambda i: (i, 0)),
            pl.BlockSpec((HALO, D_CONV), lambda i: (jnp.maximum(i * (BLK // HALO) - 1, 0), 0)),
            const(state_conv.shape),
            const((CONV_WIDTH, D_CONV)),
            const((1, D_CONV)), const((1, D_CONV)), const((1, D_CONV)),
        ],
        out_specs=pl.BlockSpec((BLK, D_CONV), lambda i: (i, 0)),
        out_shape=jax.ShapeDtypeStruct((n, D_CONV), BF16),
        scratch_shapes=[pltpu.VMEM((HALO + BLK, D_CONV), F32), pltpu.VMEM((BLK, D_CONV), F32),
                        pltpu.VMEM((8, HALO + BLK, 128), F32)],
        compiler_params=_cparams(("arbitrary",)),
        name="conv",
    )(u_all, u_all, state_conv, conv_w, row(conv_b), row(ln_g), row(ln_b))


def _suffix_matrix(n, extra):
    r = lax.broadcasted_iota(jnp.int32, (n, n + extra), 0)
    c = lax.broadcasted_iota(jnp.int32, (n, n + extra), 1)
    return jnp.where((r > c) | (c >= n), 1.0, 0.0).astype(BF16)


def _split_dot(x, m):
    hi = x.astype(BF16)
    lo = (x - hi.astype(F32)).astype(BF16)
    return (jnp.dot(hi, m, preferred_element_type=F32) + jnp.dot(lo, m, preferred_element_type=F32))


def _log_keep(z):
    return -(jnp.maximum(z, 0.0) + jnp.log(1.0 + jnp.exp(-jnp.abs(z))))


def _attn_prompt_body(q_ref, k_ref, v_ref, g_ref, o_ref, m_s):
    hg = pl.program_id(1)
    qi = pl.program_id(2)

    @pl.when((pl.program_id(0) == 0) & (hg == 0) & (qi == 0))
    def _():
        m_s[...] = _suffix_matrix(BLK, 128)

    m = m_s[...]

    def tile(hh, kb, carry, diagonal):
        lanes = slice(hh * HEAD_DIM, (hh + 1) * HEAD_DIM)
        rows = pl.ds(pl.multiple_of(kb * BLK, BLK), BLK)
        z = lax.dot_general(q_ref[:, lanes], k_ref[rows, lanes], (((1,), (1,)), ((), ())),
                            preferred_element_type=F32)
        lk = _log_keep(z)
        if diagonal:
            keep = (lax.broadcasted_iota(jnp.int32, (BLK, BLK), 1)
                    < lax.broadcasted_iota(jnp.int32, (BLK, BLK), 0))
            lk = jnp.where(keep, lk, 0.0)
        cs = _split_dot(lk, m)
        after = cs[:, :BLK]
        if carry is not None:
            after = after + jnp.concatenate([carry, carry], axis=1)
        w = jnp.exp(z + lk + after)
        if diagonal:
            w = jnp.where(keep, w, 0.0)
        pv = jnp.dot(w.astype(BF16), v_ref[rows, lanes], preferred_element_type=F32)
        return pv, cs[:, BLK:]

    first = [tile(hh, qi, None, True) for hh in range(HPS)]

    def cond(st):
        kb, _, carries = st
        live = jnp.max(carries[0])
        for c in carries[1:]:
            live = jnp.maximum(live, jnp.max(c))
        return (kb >= 0) & (live > LOG_ZERO)

    def step(st):
        kb, accs, carries = st
        out = [tile(hh, kb, carries[hh], False) for hh in range(HPS)]
        return (kb - 1, tuple(a + o[0] for a, o in zip(accs, out)),
                tuple(c + o[1] for c, o in zip(carries, out)))

    _, accs, _ = lax.while_loop(cond, step, (qi - 1, tuple(f[0] for f in first), tuple(f[1] for f in first)))
    for hh in range(HPS):
        acc = accs[hh]
        ms = jnp.mean(acc * acc, axis=-1, keepdims=True)
        o_ref[:, hh * HEAD_DIM:(hh + 1) * HEAD_DIM] = (
            acc * lax.rsqrt(ms + EPS) * g_ref[pl.ds(hg * HPS + hh, 1), :]).astype(BF16)


def _attn_prompt(q_all, kvb, g_heads, n_batch, t_pad):
    nq = t_pad // BLK
    wide = HPS * HEAD_DIM
    return pl.pallas_call(
        _attn_prompt_body,
        grid=(n_batch, N_HEADS // HPS, nq),
        in_specs=[
            pl.BlockSpec((BLK, wide), lambda b, h, i: (b * nq + i, h)),
            pl.BlockSpec((None, t_pad, wide), lambda b, h, i: (0, b, h)),
            pl.BlockSpec((None, t_pad, wide), lambda b, h, i: (1, b, h)),
            pl.BlockSpec((N_HEADS, HEAD_DIM), lambda b, h, i: (0, 0)),
        ],
        out_specs=pl.BlockSpec((BLK, wide), lambda b, h, i: (b * nq + i, h)),
        out_shape=jax.ShapeDtypeStruct((q_all.shape[0], D_ATTN), BF16),
        scratch_shapes=[pltpu.VMEM((BLK, BLK + 128), BF16)],
        compiler_params=_cparams(("arbitrary", "arbitrary", "arbitrary")),
        name="attn_prompt",
    )(q_all, kvb, kvb, g_heads)


def _attn_sample_body(n_kb, kblk, q_ref, kn_ref, vn_ref, kc_hbm, vc_hbm, g_ref, o_in_hbm, o_ref,
                      kbuf, vbuf, qbd_s, acc_s, carry_s, m_s, mn_s, ksem, vsem):
    s = pl.program_id(0)
    n_streams = pl.num_programs(0)
    dq = q_ref.shape[0]
    last = n_kb - 1

    def cache_copies(stream, jb, slot):
        rows = pl.ds(pl.multiple_of(jb * kblk, kblk), kblk)
        return (pltpu.make_async_copy(kc_hbm.at[0, stream, rows], kbuf.at[slot], ksem.at[slot]),
                pltpu.make_async_copy(vc_hbm.at[0, stream, rows], vbuf.at[slot], vsem.at[slot]))

    def start(stream, jb, slot):
        for cp in cache_copies(stream, jb, slot):
            cp.start()

    def wait(stream, jb, slot):
        for cp in cache_copies(stream, jb, slot):
            cp.wait()

    def cat_heads(buf, slot):
        return jnp.concatenate([buf[slot, :, hh, :] for hh in range(N_HEADS)], axis=1).astype(BF16)

    def scores(kcat):
        return jnp.dot(kcat, qbd_s[...], preferred_element_type=F32)

    def add_values(w, vcat):
        full = lax.dot_general(w.astype(BF16), vcat, (((0,), (0,)), ((), ())), preferred_element_type=F32)
        for hh in range(N_HEADS):
            acc_s[hh] += full[hh * dq:(hh + 1) * dq, hh * HEAD_DIM:(hh + 1) * HEAD_DIM]

    @pl.when(s == 0)
    def _():
        start(0, last, last % 2)
        r = lax.broadcasted_iota(jnp.int32, (kblk, kblk), 0)
        c = lax.broadcasted_iota(jnp.int32, (kblk, kblk), 1)
        m_s[...] = jnp.where(c > r, 1.0, 0.0).astype(BF16)
        mn_s[...] = m_s[0:dq, 0:dq]

    qrep = jnp.concatenate([q_ref[...].astype(F32)] * N_HEADS, axis=0)
    qt = qrep.T
    row_head = lax.broadcasted_iota(jnp.int32, (D_ATTN, 128), 0) // HEAD_DIM
    col_head = lax.broadcasted_iota(jnp.int32, (D_ATTN, 128), 1) // dq
    qbd_s[...] = jnp.where(row_head == col_head, qt, 0.0).astype(BF16)
    acc_s[...] = jnp.zeros_like(acc_s)
    z = scores(kn_ref[...])
    lk = _log_keep(z)
    key = lax.broadcasted_iota(jnp.int32, (dq, 128), 0)
    qry = lax.broadcasted_iota(jnp.int32, (dq, 128), 1) % dq
    keep = key < qry
    lk = jnp.where(keep, lk, 0.0)
    after = _split_dot_left(mn_s[...], lk)
    w = jnp.where(keep, jnp.exp(z + lk + after), 0.0)
    add_values(w, vn_ref[...])
    carry_s[...] = jnp.sum(lk, axis=0, keepdims=True)

    def cond(jb):
        return (jb >= 0) & (jnp.max(carry_s[...]) > LOG_ZERO)

    def step(jb):
        slot = jb % 2
        wait(s, jb, slot)

        @pl.when(jb > 0)
        def _():
            start(s, jb - 1, 1 - slot)

        z = scores(cat_heads(kbuf, slot))
        lk = _log_keep(z)
        after = _split_dot_left(m_s[...], lk) + carry_s[...]
        w = jnp.exp(z + lk + after)
        add_values(w, cat_heads(vbuf, slot))
        carry_s[...] += jnp.sum(lk, axis=0, keepdims=True)
        return jb - 1

    jb_end = lax.while_loop(cond, step, last)

    @pl.when(jb_end >= 0)
    def _():
        wait(s, jb_end, jb_end % 2)

    @pl.when(s + 1 < n_streams)
    def _():
        start(s + 1, last, last % 2)

    for hh in range(N_HEADS):
        a = acc_s[hh]
        ms = jnp.mean(a * a, axis=-1, keepdims=True)
        o_ref[:, hh * HEAD_DIM:(hh + 1) * HEAD_DIM] = (
            a * lax.rsqrt(ms + EPS) * g_ref[pl.ds(hh, 1), :]).astype(BF16)


def _split_dot_left(m, x):
    hi = x.astype(BF16)
    lo = (x - hi.astype(F32)).astype(BF16)
    return (jnp.dot(m, hi, preferred_element_type=F32) + jnp.dot(m, lo, preferred_element_type=F32))


def _attn_sample(q_all, kvb, cache_k, cache_v, g_heads, o_all, row0, dec_seq, kblk):
    n_streams = cache_k.shape[1]
    past = cache_k.shape[2]
    n_kb = past // kblk
    assert n_kb * kblk == past and row0 % dec_seq == 0
    row_spec = pl.BlockSpec((dec_seq, D_ATTN), lambda s: (row0 // dec_seq + s, 0))
    kv_spec = lambda which: pl.BlockSpec((None, dec_seq, D_ATTN), lambda s: (which, row0 // dec_seq + s, 0))
    cache_buf = pltpu.VMEM((2, kblk, N_HEADS, HEAD_DIM), cache_k.dtype)
    any_spec = pl.BlockSpec(memory_space=pl.ANY)
    return pl.pallas_call(
        functools.partial(_attn_sample_body, n_kb, kblk),
        grid=(n_streams,),
        in_specs=[row_spec, kv_spec(0), kv_spec(1), any_spec, any_spec,
                  pl.BlockSpec((N_HEADS, HEAD_DIM), lambda s: (0, 0)), any_spec],
        out_specs=row_spec,
        out_shape=jax.ShapeDtypeStruct(o_all.shape, o_all.dtype),
        input_output_aliases={6: 0},
        scratch_shapes=[
            cache_buf, cache_buf,
            pltpu.VMEM((D_ATTN, 128), BF16),
            pltpu.VMEM((N_HEADS, dec_seq, HEAD_DIM), F32),
            pltpu.VMEM((1, 128), F32),
            pltpu.VMEM((kblk, kblk), BF16),
            pltpu.VMEM((dec_seq, dec_seq), BF16),
            pltpu.SemaphoreType.DMA((2,)), pltpu.SemaphoreType.DMA((2,)),
        ],
        compiler_params=_cparams(("arbitrary",)),
        name="attn_sample",
    )(q_all, kvb, kvb, cache_k, cache_v, g_heads, o_all)


def _outproj_body(route_args, x_ref, c_ref, o_ref, w_ref, g_ref, wr_ref, br_ref, h_ref, xn_ref,
                  d_ref, gt_ref, cnt_ref, tri_s, run_s):
    h = (x_ref[...]
         + jnp.dot(c_ref[...], w_ref[0:D_CONV, :], preferred_element_type=F32)
         + jnp.dot(o_ref[...], w_ref[D_CONV:, :], preferred_element_type=F32))
    h_ref[...] = h
    ms = jnp.mean(h * h, axis=-1, keepdims=True)
    xn = (h * lax.rsqrt(ms + EPS) * g_ref[...]).astype(BF16)
    bits = lax.bitcast_convert_type(xn.astype(F32), jnp.uint32)
    xn_ref[...] = (bits[:, :D_MODEL // 2] >> 16) | bits[:, D_MODEL // 2:]
    logits_t = lax.dot_general(wr_ref[...], xn, (((1,), (1,)), ((), ())),
                               preferred_element_type=F32) + br_ref[...]
    _route_tile(*route_args, logits_t, d_ref, gt_ref, cnt_ref, tri_s, run_s)


def _outproj(x_all, c_all, o_all, w_out_bf, g_ffn, w_router_t_bf, b_router, t_p, t_pad, n_p, cap):
    n = x_all.shape[0]
    const = lambda shape: pl.BlockSpec(shape, lambda i: (0,) * len(shape))
    per_token = pl.BlockSpec((TOP_K, ROW_E), lambda i: (0, i))
    return pl.pallas_call(
        functools.partial(_outproj_body, (t_p, t_pad, n_p, cap)),
        grid=(n // ROW_E,),
        in_specs=[
            pl.BlockSpec((ROW_E, D_MODEL), lambda i: (i, 0)),
            pl.BlockSpec((ROW_E, D_CONV), lambda i: (i, 0)),
            pl.BlockSpec((ROW_E, D_ATTN), lambda i: (i, 0)),
            const((D_MODEL, D_MODEL)),
            const((1, D_MODEL)),
            const((N_EXPERTS, D_MODEL)),
            const((N_EXPERTS, 1)),
        ],
        out_specs=[
            pl.BlockSpec((ROW_E, D_MODEL), lambda i: (i, 0)),
            pl.BlockSpec((ROW_E, D_MODEL // 2), lambda i: (i, 0)),
            per_token, per_token, pl.BlockSpec((N_EXPERTS, 128), lambda i: (0, 0)),
        ],
        out_shape=[jax.ShapeDtypeStruct((n, D_MODEL), F32), jax.ShapeDtypeStruct((n, D_MODEL // 2), jnp.uint32),
                   jax.ShapeDtypeStruct((TOP_K, n), jnp.int32), jax.ShapeDtypeStruct((TOP_K, n), F32),
                   jax.ShapeDtypeStruct((N_EXPERTS, 128), jnp.int32)],
        scratch_shapes=[pltpu.VMEM((ROW_E, ROW_E + 128), BF16), pltpu.VMEM((N_EXPERTS, 128), F32)],
        compiler_params=_cparams(("arbitrary",)),
        name="outproj",
    )(x_all, c_all, o_all, w_out_bf, g_ffn.reshape(1, D_MODEL), w_router_t_bf,
      b_router.reshape(N_EXPERTS, 1))


def _route_tile(t_p, t_pad, n_p, cap, lg, d_ref, gt_ref, cnt_ref, tri_s, run_s):
    i = pl.program_id(0)
    tb = lg.shape[1]

    @pl.when(i == 0)
    def _():
        r = lax.broadcasted_iota(jnp.int32, (tb, tb + 128), 0)
        c = lax.broadcasted_iota(jnp.int32, (tb, tb + 128), 1)
        tri_s[...] = jnp.where((r < c) | (c >= tb), 1.0, 0.0).astype(BF16)
        run_s[...] = jnp.zeros_like(run_s)

    tok = i * tb + lax.broadcasted_iota(jnp.int32, (1, tb), 1)
    valid = ((tok % t_pad) < t_p) | (tok >= n_p)
    eid = lax.broadcasted_iota(jnp.int32, (N_EXPERTS, tb), 0)
    sel = jnp.zeros((N_EXPERTS, tb), F32)
    hot, top = [], []
    for _ in range(TOP_K):
        mx = jnp.max(lg, axis=0, keepdims=True)
        idx = jnp.min(jnp.where(lg == mx, eid, N_EXPERTS), axis=0, keepdims=True)
        one = eid == idx
        lg = jnp.where(one, -jnp.inf, lg)
        hot.append(one)
        top.append(mx)
        sel = sel + jnp.where(one & valid, 1.0, 0.0)
    ex = [jnp.exp(t - top[0]) for t in top]
    den = ex[0] + ex[1] + ex[2] + ex[3]
    cs = jnp.dot(sel.astype(BF16), tri_s[...], preferred_element_type=F32)
    run = run_s[...]
    slot = (cs[:, :tb] + jnp.concatenate([run] * (tb // 128), axis=1)
            + (eid * cap).astype(F32))
    for k in range(TOP_K):
        mine = jnp.sum(jnp.where(hot[k], slot, 0.0), axis=0, keepdims=True).astype(jnp.int32)
        trash = N_EXPERTS * cap + (tok % TB) * TOP_K + k
        d_ref[pl.ds(k, 1), :] = jnp.where(valid, mine, trash)
        gt_ref[pl.ds(k, 1), :] = ex[k] / den
    run = run + cs[:, tb:]
    run_s[...] = run
    cnt_ref[...] = run.astype(jnp.int32)


def _slot_table(dest, n_blocks):
    return dest.T.reshape(n_blocks, TB // 8, 8 * TOP_K)


def _slot_of(dest_ref, i, u, k):
    return dest_ref[0, i, u * TOP_K + k]


SLOT_BLOCK = (1, TB // 8, 8 * TOP_K)


def _dispatch_body(dest_ref, xp_ref, xs_hbm, sem):
    def row_copy(i, u, slot):
        return pltpu.make_async_copy(xp_ref.at[i, pl.ds(u, 1), :], xs_hbm.at[pl.ds(slot, 1), :], sem)

    def issue(i, c):
        for u in range(8):
            for k in range(TOP_K):
                row_copy(i, u, _slot_of(dest_ref, i, u, k)).start(priority=k % 2)
        return c

    def drain(i, c):
        for _ in range(8 * TOP_K):
            row_copy(0, 0, 0).wait()
        return c

    for i in range(TB // 8):
        issue(i, 0)
    lax.fori_loop(0, TB // 8, drain, 0)


def _dispatch(dest_tab, xn_packed, n_slots):
    n, width = xn_packed.shape
    return pl.pallas_call(
        _dispatch_body,
        grid=(n // TB,),
        in_specs=[
            pl.BlockSpec(SLOT_BLOCK, lambda i: (i, 0, 0), memory_space=pltpu.SMEM),
            pl.BlockSpec((TB // 8, 8, width), lambda i: (i, 0, 0)),
        ],
        out_specs=pl.BlockSpec(memory_space=pl.ANY),
        out_shape=jax.ShapeDtypeStruct((n_slots, width), jnp.uint32),
        scratch_shapes=[pltpu.SemaphoreType.DMA(())],
        compiler_params=_cparams(("arbitrary",)),
        name="dispatch",
    )(dest_tab, xn_packed.reshape(n // 8, 8, width))


def _moe_body(ie_ref, r0_ref, nr_ref, nlive_ref, xs_hbm, wup_hbm, wdn_hbm, bup_ref, bd_ref,
              ys_hbm, xraw, xb16, yacc, wg_buf, wl_buf, wd_buf, pend_s, xsem, ysem, wsem):
    it = pl.program_id(0)
    n_live = nlive_ref[0]
    n_f = D_FF // TF
    nrows = nr_ref[it]

    def chunks(item):
        return (nr_ref[item] + CH - 1) // CH

    def x_copy(item, c):
        src = pl.multiple_of(r0_ref[item] + c * CH, CH)
        return pltpu.make_async_copy(xs_hbm.at[pl.ds(src, CH), :],
                                     xraw.at[pl.ds(pl.multiple_of(c * CH, CH), CH), :], xsem)

    def y_copy(item, c):
        dst = pl.multiple_of(r0_ref[item] + c * CH, CH)
        return pltpu.make_async_copy(yacc.at[pl.ds(pl.multiple_of(c * CH, CH), CH), :],
                                     ys_hbm.at[pl.ds(dst, CH), :], ysem)

    def w_copies(item, jj, slot):
        e = ie_ref[item]
        glu = pl.ds(pl.multiple_of(jj * TF, TF), TF)
        lin = pl.ds(pl.multiple_of(D_FF + jj * TF, TF), TF)
        return (pltpu.make_async_copy(wup_hbm.at[e, :, glu], wg_buf.at[slot], wsem.at[slot]),
                pltpu.make_async_copy(wup_hbm.at[e, :, lin], wl_buf.at[slot], wsem.at[slot]),
                pltpu.make_async_copy(wdn_hbm.at[e, glu, :], wd_buf.at[slot], wsem.at[slot]))

    def start_w(item, jj, slot):
        for cp in w_copies(item, jj, slot):
            cp.start()

    def for_chunks(n, fn):
        def body(c, carry):
            fn(c)
            return carry
        lax.fori_loop(0, n, body, 0)

    @pl.when(it == 0)
    def _():
        pend_s[0] = 0
        for_chunks(chunks(0), lambda c: x_copy(0, c).start())
        start_w(0, 0, 0)

    n_ch = chunks(it)
    for_chunks(n_ch, lambda c: x_copy(it, c).wait())

    def unpack(c):
        rows = pl.ds(pl.multiple_of(c * CH, CH), CH)
        p = xraw[rows, :]
        rid = c * CH + lax.broadcasted_iota(jnp.int32, (CH, 1), 0)
        live = rid < nrows
        lo = lax.bitcast_convert_type(p << 16, F32)
        hi = lax.bitcast_convert_type(p & jnp.uint32(0xFFFF0000), F32)
        xb16[rows, 0:D_MODEL // 2] = jnp.where(live, lo, 0.0).astype(BF16)
        xb16[rows, D_MODEL // 2:] = jnp.where(live, hi, 0.0).astype(BF16)

    for_chunks(n_ch, unpack)

    @pl.when(it + 1 < n_live)
    def _():
        for_chunks(chunks(it + 1), lambda c: x_copy(it + 1, c).start(priority=1))

    for_chunks(pend_s[0], lambda c: y_copy(it, 0).wait())
    n_full = n_ch // (SUB // CH)
    tail = n_ch - n_full * (SUB // CH)

    def hidden_tile(j, carry):
        slot = j % 2
        for cp in w_copies(it, j, slot):
            cp.wait()

        @pl.when(j + 1 < n_f)
        def _():
            start_w(it, j + 1, 1 - slot)

        @pl.when((j + 1 == n_f) & (it + 1 < n_live))
        def _():
            start_w(it + 1, 0, 1 - slot)

        bg = bup_ref[0, :, pl.ds(pl.multiple_of(j * TF, TF), TF)]
        bl = bup_ref[0, :, pl.ds(pl.multiple_of(D_FF + j * TF, TF), TF)]

        def ffn_rows(r0, n):
            rows = pl.ds(r0, n)
            x = xb16[rows, :]
            hg = jnp.dot(x, wg_buf[slot].astype(BF16), preferred_element_type=F32) + bg
            hl = jnp.dot(x, wl_buf[slot].astype(BF16), preferred_element_type=F32) + bl
            hg = jnp.minimum(hg, SWIGLU_LIMIT)
            hl = jnp.clip(hl, -SWIGLU_LIMIT, SWIGLU_LIMIT)
            a = hg * jax.nn.sigmoid(SWIGLU_ALPHA * hg) * (hl + 1.0)
            part = jnp.dot(a.astype(BF16), wd_buf[slot].astype(BF16), preferred_element_type=F32)
            yacc[rows, :] = jnp.where(j == 0, bd_ref[0], yacc[rows, :]) + part

            @pl.when(j == n_f - 1)
            def _():
                for cc in range(n // CH):
                    y_copy(it, r0 // CH + cc).start(priority=1)

        def full(s, c):
            ffn_rows(pl.multiple_of(s * SUB, CH), SUB)
            return c

        lax.fori_loop(0, n_full, full, 0)
        for t in range(1, SUB // CH):
            @pl.when(tail == t)
            def _():
                ffn_rows(pl.multiple_of(n_full * SUB, CH), t * CH)
        return carry

    lax.fori_loop(0, n_f, hidden_tile, 0)
    pend_s[0] = n_ch

    @pl.when(it == n_live - 1)
    def _():
        for_chunks(n_ch, lambda c: y_copy(it, c).wait())
        pend_s[0] = 0


def _moe(xs, item_expert, item_row0, item_rows, n_live, w_up, b_up, w_down, b_down):
    assert (D_FF // TF) % 2 == 0
    any_spec = pl.BlockSpec(memory_space=pl.ANY)
    return pl.pallas_call(
        _moe_body,
        grid_spec=pltpu.PrefetchScalarGridSpec(
            num_scalar_prefetch=4,
            grid=(n_live[0],),
            in_specs=[
                any_spec, any_spec, any_spec,
                pl.BlockSpec((1, 1, 2 * D_FF), lambda it, ie, r0, nr, nl: (ie[it], 0, 0)),
                pl.BlockSpec((1, 1, D_MODEL), lambda it, ie, r0, nr, nl: (ie[it], 0, 0)),
            ],
            out_specs=any_spec,
            scratch_shapes=[
                pltpu.VMEM((TM, D_MODEL // 2), jnp.uint32),
                pltpu.VMEM((TM, D_MODEL), BF16),
                pltpu.VMEM((TM, D_MODEL), F32),
                pltpu.VMEM((2, D_MODEL, TF), w_up.dtype),
                pltpu.VMEM((2, D_MODEL, TF), w_up.dtype),
                pltpu.VMEM((2, TF, D_MODEL), w_down.dtype),
                pltpu.SMEM((1,), jnp.int32),
                pltpu.SemaphoreType.DMA(()),
                pltpu.SemaphoreType.DMA(()),
                pltpu.SemaphoreType.DMA((2,)),
            ],
        ),
        out_shape=jax.ShapeDtypeStruct((xs.shape[0], D_MODEL), F32),
        compiler_params=_cparams(("arbitrary",)),
        name="moe",
    )(item_expert, item_row0, item_rows, n_live, xs, w_up, w_down,
      b_up.reshape(N_EXPERTS, 1, 2 * D_FF), b_down.reshape(N_EXPERTS, 1, D_MODEL))


def _combine_body(blocks_per_batch, n_prompt_blocks, t_pad, n_p, dest_ref, dest_next_ref, ys_hbm, h_hbm,
                  gt_ref, gf_ref, yp_ref, ysm_ref, gbuf, hbuf, gsem, hsem):
    i = pl.program_id(0)
    n_blocks = pl.num_programs(0)
    cur = i % 2

    def h_copy(blk, slot):
        row = jnp.where(blk < n_prompt_blocks,
                        (blk // blocks_per_batch) * t_pad + N_META + (blk % blocks_per_batch) * TB, n_p)
        return pltpu.make_async_copy(h_hbm.at[pl.ds(pl.multiple_of(row, 8), TB), :], hbuf.at[slot],
                                     hsem.at[slot])

    def row_copy(slot, buf, k, c, u):
        return pltpu.make_async_copy(ys_hbm.at[pl.ds(slot, 1), :], gbuf.at[buf, k, c, pl.ds(u, 1), :],
                                     gsem.at[buf])

    def request(table_ref, blk, buf):
        h_copy(blk, buf).start()

        def issue(c, carry):
            for u in range(8):
                for k in range(TOP_K):
                    row_copy(_slot_of(table_ref, c, u, k), buf, k, c, u).start(priority=k % 2)
            return carry

        for c in range(TB // 8):
            issue(c, 0)

    def drain(c, carry):
        for _ in range(8 * TOP_K):
            row_copy(0, cur, 0, 0, 0).wait()
        return carry

    @pl.when(i == 0)
    def _():
        request(dest_ref, 0, 0)

    for nxt in range(2):
        @pl.when((i + 1 < n_blocks) & (cur == 1 - nxt))
        def _():
            request(dest_next_ref, i + 1, nxt)

    h_copy(i, cur).wait()
    lax.fori_loop(0, TB // 8, drain, 0)
    y = hbuf[cur]
    for k in range(TOP_K):
        y = y + gbuf[cur, k].reshape(TB, D_MODEL) * gt_ref[:, k:k + 1]
    ms = jnp.mean(y * y, axis=-1, keepdims=True)
    y = y * lax.rsqrt(ms + EPS) * gf_ref[...]

    @pl.when(i < n_prompt_blocks)
    def _():
        yp_ref[...] = y

    @pl.when(i >= n_prompt_blocks)
    def _():
        ysm_ref[...] = y


def _combine(dest_blk, ys, h_all, gates_t, g_final, blocks_per_batch, n_prompt_blocks, t_pad, n_p):
    return pl.pallas_call(
        functools.partial(_combine_body, blocks_per_batch, n_prompt_blocks, t_pad, n_p),
        grid=(n_prompt_blocks + 1,),
        in_specs=[
            pl.BlockSpec(SLOT_BLOCK, lambda i: (i, 0, 0), memory_space=pltpu.SMEM),
            pl.BlockSpec(SLOT_BLOCK, lambda i: (jnp.minimum(i + 1, n_prompt_blocks), 0, 0),
                         memory_space=pltpu.SMEM),
            pl.BlockSpec(memory_space=pl.ANY),
            pl.BlockSpec(memory_space=pl.ANY),
            pl.BlockSpec((TB, TOP_K), lambda i: (i, 0)),
            pl.BlockSpec((1, D_MODEL), lambda i: (0, 0)),
        ],
        out_specs=[
            pl.BlockSpec((TB, D_MODEL), lambda i: (jnp.minimum(i, n_prompt_blocks - 1), 0)),
            pl.BlockSpec((TB, D_MODEL), lambda i: (0, 0)),
        ],
        out_shape=[jax.ShapeDtypeStruct((n_prompt_blocks * TB, D_MODEL), F32),
                   jax.ShapeDtypeStruct((TB, D_MODEL), F32)],
        scratch_shapes=[pltpu.VMEM((2, TOP_K, TB // 8, 8, D_MODEL), F32), pltpu.VMEM((2, TB, D_MODEL), F32),
                        pltpu.SemaphoreType.DMA((2,)), pltpu.SemaphoreType.DMA((2,))],
        compiler_params=_cparams(("arbitrary",)),
        name="combine",
    )(dest_blk, dest_blk, ys, h_all, gates_t, g_final.reshape(1, D_MODEL))


def _work_items(counts, cap, max_items):
    items_per = (counts + TM - 1) // TM
    item_end = jnp.cumsum(items_per)
    item_start = item_end - items_per
    n_live = item_end[-1]
    it = jnp.arange(max_items, dtype=jnp.int32)
    it_live = jnp.minimum(it, n_live - 1)
    expert = jnp.minimum(jnp.searchsorted(item_end, it_live, side="right"), N_EXPERTS - 1).astype(jnp.int32)
    part = it_live - item_start[expert]
    rows = jnp.clip(counts[expert] - part * TM, 0, TM)
    rows = jnp.where(it < n_live, rows, 0).astype(jnp.int32)
    row0 = (expert * cap + part * TM).astype(jnp.int32)
    return expert, row0, rows, n_live.reshape(1).astype(jnp.int32)


def kernel(x_prompt, x_sample, cache_k, cache_v, state_conv, meta_tokens, g_mix, w_in, conv_w, conv_b,
           conv_ln_g, conv_ln_b, g_attn_out, w_out, g_ffn, w_router, b_router, w_up, b_up, w_down,
           b_down, g_final):
    n_batch, seq, _ = x_prompt.shape
    n_streams, dec_seq, _ = x_sample.shape
    depth = w_in.shape[0]
    assert depth == 1 and n_streams * dec_seq == BLK
    t_p = N_META + seq
    t_pad = -(-t_p // BLK) * BLK
    n_p = n_batch * t_pad
    n_s = n_streams * dec_seq
    n_all = n_p + n_s
    assert n_all % ROW_A == 0 and n_all % ROW_E == 0 and ROW_E % 128 == 0 and n_all % TB == 0

    l = 0
    u_all, q_all, kvb, k_p, v_p, k_s, v_s, x_all = _inproj(
        x_prompt, x_sample.reshape(n_s, D_MODEL), meta_tokens.astype(F32), g_mix[l], w_in[l].astype(BF16), t_pad)
    c_all = _conv(u_all, state_conv[l], conv_w[l], conv_b[l], conv_ln_g[l], conv_ln_b[l],
                  t_pad // BLK, n_p // BLK)
    g_heads = g_attn_out[l].reshape(N_HEADS, HEAD_DIM)
    o_all = _attn_prompt(q_all, kvb, g_heads, n_batch, t_pad)
    o_all = _attn_sample(q_all, kvb, cache_k[l:l + 1], cache_v[l:l + 1], g_heads, o_all, n_p, dec_seq, BLK)
    n_valid = n_batch * t_p + n_s
    cap = -(-n_valid // CH) * CH
    n_slots = N_EXPERTS * cap + TB * TOP_K
    max_items = -(-(n_valid * TOP_K) // TM) + N_EXPERTS
    h_all, xn_packed, dest, gates, counts = _outproj(
        x_all, c_all, o_all, w_out[l].astype(BF16), g_ffn[l], w_router[l].T.astype(BF16), b_router[l],
        t_p, t_pad, n_p, cap)
    xs = _dispatch(_slot_table(dest, n_all // TB), xn_packed, n_slots)
    item_expert, item_row0, item_rows, n_live = _work_items(counts[:, 0], cap, max_items)
    ys = _moe(xs, item_expert, item_row0, item_rows, n_live, w_up[l], b_up[l], w_down[l], b_down[l])

    def out_tokens(a):
        a_p = a[:, :n_p].reshape(-1, n_batch, t_pad)[:, :, N_META:t_p].reshape(-1, n_batch * seq)
        return jnp.concatenate([a_p, a[:, n_p:]], axis=1)

    assert seq % TB == 0 and n_s == TB
    n_out_blocks = n_batch * seq // TB
    dest_out = _slot_table(out_tokens(dest), n_out_blocks + 1)
    y_p, y_s = _combine(dest_out, ys, h_all, out_tokens(gates).T, g_final, seq // TB, n_out_blocks, t_pad, n_p)

    y_prompt = y_p.reshape(n_batch, seq, D_MODEL)
    y_sample = y_s.reshape(n_streams, dec_seq, D_MODEL)
    k_prompt = k_p[None]
    v_prompt = v_p[None]
    conv_prompt = jnp.stack([u_all[b * t_pad + t_p - (CONV_WIDTH - 1):b * t_pad + t_p]
                             for b in range(n_batch)])[None]
    k_sample = k_s.reshape(1, n_streams, dec_seq, N_HEADS, HEAD_DIM)
    v_sample = v_s.reshape(1, n_streams, dec_seq, N_HEADS, HEAD_DIM)
    u_s = u_all[n_p:].reshape(n_streams, dec_seq, D_CONV)
    conv_sample = jnp.concatenate([state_conv[l], u_s], axis=1)[:, -(CONV_WIDTH - 1):][None]
    return (y_prompt, y_sample, k_prompt, v_prompt, conv_prompt, k_sample, v_sample, conv_sample)
```
